```python
import math
import jax, jax.numpy as jnp
from jax import lax
import numpy as np

D_MODEL = 4096
BATCH = 1
SEQ = 8192
DEPTH = 1

MLA_HEADS = 16
Q_LORA_RANK = 1024
KV_LORA_RANK = 512
QK_NOPE_DIM = 128
QK_ROPE_DIM = 64
QK_HEAD_DIM = QK_NOPE_DIM + QK_ROPE_DIM
V_HEAD_DIM = 128
MLA_WIDTH = MLA_HEADS * V_HEAD_DIM
ROPE_THETA = 10000.0
Q_BLOCK = 128

DIL_GROUPS = ((128, 1), (512, 4), (2048, 16))
N_DIL_GROUPS = 3
DSWA_HEADS_PER_GROUP = 8
DSWA_HEADS = N_DIL_GROUPS * DSWA_HEADS_PER_GROUP
DSWA_HEAD_DIM = 128
DSWA_QKV_WIDTH = DSWA_HEADS * DSWA_HEAD_DIM
DSWA_WIDTH = DSWA_HEADS_PER_GROUP * DSWA_HEAD_DIM

REL_BUCKETS = 32
REL_MAX_DISTANCE = 1024

DEEPNORM_ALPHA = (2.0 * DEPTH) ** 0.25
DEEPNORM_BETA = (8.0 * DEPTH) ** -0.25
LN_EPS = 1e-5
RMS_EPS = 1e-6
NEG_INF = -1e30

IN_SPLITS = (
    Q_LORA_RANK,
    KV_LORA_RANK + QK_ROPE_DIM,
    DSWA_QKV_WIDTH,
    DSWA_QKV_WIDTH,
    DSWA_QKV_WIDTH,
    MLA_WIDTH,
    DSWA_WIDTH,
    D_MODEL,
    D_MODEL,
)
IN_WIDTH = sum(IN_SPLITS)
IN_OFFSETS = tuple(int(v) for v in np.cumsum(IN_SPLITS)[:-1])
DSWA_V_START = IN_OFFSETS[3]
DSWA_V_END = IN_OFFSETS[4]

kernel_name = "hybrid_mla_dilated_window_encoder_layer"


def layer_norm(x, g, b):
    xf = x.astype(jnp.float32)
    mu = jnp.mean(xf, axis=-1, keepdims=True)
    xc = xf - mu
    var = jnp.mean(xc * xc, axis=-1, keepdims=True)
    return (xc * lax.rsqrt(var + LN_EPS) * g.astype(jnp.float32) + b.astype(jnp.float32)).astype(x.dtype)


def rms_norm(x, g):
    xf = x.astype(jnp.float32)
    y = xf * lax.rsqrt(jnp.mean(xf * xf, axis=-1, keepdims=True) + RMS_EPS)
    return (y * g.astype(jnp.float32)).astype(x.dtype)


def rope(t, cos, sin):
    half = t.shape[-1] // 2
    t1 = t[..., :half].astype(jnp.float32)
    t2 = t[..., half:].astype(jnp.float32)
    return jnp.concatenate([t1 * cos - t2 * sin, t2 * cos + t1 * sin], axis=-1).astype(t.dtype)


def t5_bucket(rel):
    nb = REL_BUCKETS // 2
    max_exact = nb // 2
    n = jnp.abs(rel)
    nf = jnp.maximum(n, 1).astype(jnp.float32)
    large = max_exact + (jnp.log(nf / max_exact) / math.log(REL_MAX_DISTANCE / max_exact)
                         * (nb - max_exact)).astype(jnp.int32)
    large = jnp.minimum(large, nb - 1)
    return jnp.where(rel > 0, nb, 0) + jnp.where(n < max_exact, n, large)


def mla_attention(q_a, kv_a, q_a_norm_g, w_q_b, kv_a_norm_g, w_kv_b):
    B, S, _ = q_a.shape
    q = (rms_norm(q_a, q_a_norm_g) @ w_q_b).reshape(B, S, MLA_HEADS, QK_HEAD_DIM)
    q_nope, q_pe = q[..., :QK_NOPE_DIM], q[..., QK_NOPE_DIM:]
    c_kv, k_pe = kv_a[..., :KV_LORA_RANK], kv_a[..., KV_LORA_RANK:]
    kv = (rms_norm(c_kv, kv_a_norm_g) @ w_kv_b).reshape(B, S, MLA_HEADS, QK_NOPE_DIM + V_HEAD_DIM)
    k_nope, v = kv[..., :QK_NOPE_DIM], kv[..., QK_NOPE_DIM:]

    pos = jnp.arange(S, dtype=jnp.float32)
    inv_freq = 1.0 / (ROPE_THETA ** (jnp.arange(0, QK_ROPE_DIM, 2, dtype=jnp.float32) / QK_ROPE_DIM))
    ang = pos[:, None] * inv_freq[None, :]
    cos, sin = jnp.cos(ang), jnp.sin(ang)
    q_pe = rope(q_pe, cos[None, :, None, :], sin[None, :, None, :])
    k_pe = rope(k_pe, cos[None], sin[None])

    scale = QK_HEAD_DIM ** -0.5
    n_blk = S // Q_BLOCK

    def to_blocks(t):
        return jnp.moveaxis(t.reshape(B, n_blk, Q_BLOCK, t.shape[2], t.shape[3]), 1, 0)

    def attend(blk):
        qn_b, qp_b = blk
        s = (jnp.einsum('bqhd,bkhd->bhqk', qn_b, k_nope)
             + jnp.einsum('bqhd,bkd->bhqk', qp_b, k_pe)).astype(jnp.float32) * scale
        p = jax.nn.softmax(s, axis=-1)
        return jnp.einsum('bhqk,bkhd->bqhd', p.astype(v.dtype), v)

    o = lax.map(attend, (to_blocks(q_nope), to_blocks(q_pe)))
    return jnp.moveaxis(o, 0, 1).reshape(B, S, MLA_WIDTH)


def dilated_band_attention(q, k, v, bias_tab, window, dil):
    B, S, H, Dh = q.shape
    half = window // (2 * dil)
    L = -(-S // (half * dil)) * half
    nb = L // half
    Sp = L * dil
    pad = ((0, 0), (0, Sp - S), (0, 0), (0, 0))

    def to_res(t):
        return jnp.pad(t, pad).reshape(B, L, dil, H, Dh).transpose(0, 2, 1, 3, 4)

    def band(t):
        widths = [(0, 0), (0, 0), (half, half)] + [(0, 0)] * (t.ndim - 3)
        tb = jnp.pad(t, widths).reshape(t.shape[:2] + (nb + 2, half) + t.shape[3:])
        return jnp.concatenate([tb[:, :, :-2], tb[:, :, 1:-1], tb[:, :, 2:]], axis=3)

    qr = to_res(q).reshape(B, dil, nb, half, H, Dh)
    kb = band(to_res(k))
    vb = band(to_res(v))
    valid = (jnp.arange(Sp) < S).reshape(L, dil).T[None]
    kvalid = band(valid)

    steps = jnp.arange(3 * half)[None, :] - half - jnp.arange(half)[:, None]
    in_band = jnp.abs(steps) <= half
    bias = jnp.transpose(bias_tab[t5_bucket(steps * dil)], (2, 0, 1)).astype(jnp.float32)

    s = jnp.einsum('brnqhd,brnkhd->brnhqk', qr, kb).astype(jnp.float32) * (Dh ** -0.5) + bias
    mask = in_band & kvalid[:, :, :, None, None, :]
    s = jnp.where(mask, s, NEG_INF)
    m = jnp.max(s, axis=-1, keepdims=True)
    p = jnp.exp(s - m)
    den = jnp.sum(p, axis=-1, keepdims=True)
    o = jnp.einsum('brnhqk,brnkhd->brnqhd', (p / den).astype(v.dtype), vb)
    lse = (m + jnp.log(den))[..., 0]

    o = o.reshape(B, dil, L, H, Dh).transpose(0, 2, 1, 3, 4).reshape(B, Sp, H, Dh)[:, :S]
    lse = lse.transpose(0, 1, 2, 4, 3).reshape(B, dil, L, H).transpose(0, 2, 1, 3).reshape(B, Sp, H)[:, :S]
    return o, lse


def dilated_mixture(dq, dk, dv, rel_bias):
    B, S, _ = dq.shape
    q = dq.reshape(B, S, DSWA_HEADS, DSWA_HEAD_DIM)
    k = dk.reshape(B, S, DSWA_HEADS, DSWA_HEAD_DIM)
    v = dv.reshape(B, S, DSWA_HEADS, DSWA_HEAD_DIM)
    outs, lses = [], []
    for g, (window, dil) in enumerate(DIL_GROUPS):
        hs = slice(g * DSWA_HEADS_PER_GROUP, (g + 1) * DSWA_HEADS_PER_GROUP)
        o, lse = dilated_band_attention(q[:, :, hs], k[:, :, hs], v[:, :, hs], rel_bias[:, hs], window, dil)
        outs.append(o)
        lses.append(lse)
    w = jax.nn.softmax(jnp.stack(lses, axis=0), axis=0)
    mix = jnp.sum(w[..., None] * jnp.stack(outs, axis=0).astype(jnp.float32), axis=0)
    return mix.astype(dq.dtype).reshape(B, S, DSWA_WIDTH)


def setup_inputs(seed: int = 0) -> dict:
    key = jax.random.key(seed)
    ks = jax.random.split(key, 16)
    f32 = jnp.float32
    x = jax.random.normal(ks[0], (BATCH, SEQ, D_MODEL), f32)
    emb_ln_g = 1.0 + 0.02 * jax.random.normal(ks[1], (D_MODEL,), f32)
    emb_ln_b = 0.02 * jax.random.normal(ks[2], (D_MODEL,), f32)
    rel_bias = 0.5 * jax.random.normal(ks[3], (REL_BUCKETS, DSWA_HEADS), f32)
    col_scale = jnp.ones((IN_WIDTH,), f32).at[DSWA_V_START:DSWA_V_END].set(DEEPNORM_BETA)
    w_in = jax.random.normal(ks[4], (DEPTH, D_MODEL, IN_WIDTH), f32) * (D_MODEL ** -0.5) * col_scale
    q_a_norm_g = 1.0 + 0.02 * jax.random.normal(ks[5], (DEPTH, Q_LORA_RANK), f32)
    w_q_b = jax.random.normal(ks[6], (DEPTH, Q_LORA_RANK, MLA_HEADS * QK_HEAD_DIM), f32) * (Q_LORA_RANK ** -0.5)
    kv_a_norm_g = 1.0 + 0.02 * jax.random.normal(ks[7], (DEPTH, KV_LORA_RANK), f32)
    kv_col_scale = jnp.tile(jnp.concatenate([jnp.ones((QK_NOPE_DIM,), f32),
                                             jnp.full((V_HEAD_DIM,), DEEPNORM_BETA, f32)]), MLA_HEADS)
    w_kv_b = (jax.random.normal(ks[8], (DEPTH, KV_LORA_RANK, MLA_HEADS * (QK_NOPE_DIM + V_HEAD_DIM)), f32)
              * (KV_LORA_RANK ** -0.5) * kv_col_scale)
    w_o_mla = jax.random.normal(ks[9], (DEPTH, MLA_WIDTH, D_MODEL), f32) * (MLA_WIDTH ** -0.5) * DEEPNORM_BETA
    w_o_dswa = jax.random.normal(ks[10], (DEPTH, DSWA_WIDTH, D_MODEL), f32) * (DSWA_WIDTH ** -0.5) * DEEPNORM_BETA
    w_out = jax.random.normal(ks[11], (DEPTH, D_MODEL, D_MODEL), f32) * (D_MODEL ** -0.5) * DEEPNORM_BETA
    ln_g = 1.0 + 0.02 * jax.random.normal(ks[12], (DEPTH, D_MODEL), f32)
    ln_b = 0.02 * jax.random.normal(ks[13], (DEPTH, D_MODEL), f32)
    return {"x": x, "emb_ln_g": emb_ln_g, "emb_ln_b": emb_ln_b, "rel_bias": rel_bias,
            "w_in": w_in, "q_a_norm_g": q_a_norm_g, "w_q_b": w_q_b, "kv_a_norm_g": kv_a_norm_g,
            "w_kv_b": w_kv_b, "w_o_mla": w_o_mla, "w_o_dswa": w_o_dswa, "w_out": w_out,
            "ln_g": ln_g, "ln_b": ln_b}


def reference(x, emb_ln_g, emb_ln_b, rel_bias, w_in, q_a_norm_g, w_q_b, kv_a_norm_g,
              w_kv_b, w_o_mla, w_o_dswa, w_out, ln_g, ln_b):
    h = layer_norm(x, emb_ln_g, emb_ln_b)
    for l in range(DEPTH):
        proj = h @ w_in[l]
        q_a, kv_a, dq, dk, dv, g_mla, g_dswa, r_mla, r_dswa = jnp.split(proj, IN_OFFSETS, axis=-1)
        a = mla_attention(q_a, kv_a, q_a_norm_g[l], w_q_b[l], kv_a_norm_g[l], w_kv_b[l])
        y_mla = (a * jax.nn.silu(g_mla)) @ w_o_mla[l]
        b = dilated_mixture(dq, dk, dv, rel_bias)
        y_dswa = (b * jax.nn.silu(g_dswa)) @ w_o_dswa[l]
        merged = jax.nn.sigmoid(r_mla) * y_mla + jax.nn.sigmoid(r_dswa) * y_dswa
        out = merged @ w_out[l]
        h = layer_norm(DEEPNORM_ALPHA * h + out, ln_g[l], ln_b[l])
    return h
```

```python
import functools
import math

import jax
import jax.numpy as jnp
import numpy as np
from jax import lax
from jax.experimental import pallas as pl
from jax.experimental.pallas import tpu as pltpu

D_MODEL = 4096
SEQ = 8192
DEPTH = 1

MLA_HEADS = 16
Q_LORA_RANK = 1024
KV_LORA_RANK = 512
QK_NOPE_DIM = 128
QK_ROPE_DIM = 64
QK_HEAD_DIM = QK_NOPE_DIM + QK_ROPE_DIM
V_HEAD_DIM = 128
MLA_WIDTH = MLA_HEADS * V_HEAD_DIM
ROPE_THETA = 10000.0

DIL_GROUPS = ((128, 1), (512, 4), (2048, 16))
DSWA_HEADS_PER_GROUP = 8
DSWA_HEADS = 24
DSWA_HEAD_DIM = 128
DSWA_QKV_WIDTH = DSWA_HEADS * DSWA_HEAD_DIM
DSWA_WIDTH = DSWA_HEADS_PER_GROUP * DSWA_HEAD_DIM
BAND_HALF = 64

REL_BUCKETS = 32
REL_MAX_DISTANCE = 1024

DEEPNORM_ALPHA = (2.0 * DEPTH) ** 0.25
LN_EPS = 1e-5
RMS_EPS = 1e-6
NEG_INF = -1e30

IN_SPLITS = (Q_LORA_RANK, KV_LORA_RANK + QK_ROPE_DIM, DSWA_QKV_WIDTH, DSWA_QKV_WIDTH,
             DSWA_QKV_WIDTH, MLA_WIDTH, DSWA_WIDTH, D_MODEL, D_MODEL)
IN_OFFSETS = tuple(int(v) for v in np.cumsum(IN_SPLITS)[:-1])

LANES = 128
QK_PAD_DIM = 256
MLA_A_WIDTH = Q_LORA_RANK + KV_LORA_RANK + LANES
GATE_WIDTH = MLA_WIDTH + DSWA_WIDTH + 2 * D_MODEL
VMEM_LIMIT = 56 * 1024 * 1024

BF16 = jnp.bfloat16
F32 = jnp.float32


def _params(sem):
    return pltpu.CompilerParams(dimension_semantics=sem, vmem_limit_bytes=VMEM_LIMIT)


def _dot(a, b):
    return jnp.dot(a, b, preferred_element_type=F32)


def _dot_nt(a, b):
    return lax.dot_general(a, b, (((1,), (1,)), ((), ())), preferred_element_type=F32)


def _sigmoid(x):
    return 1.0 / (1.0 + jnp.exp(-x))


def _layer_norm_rows(x, g, b):
    mu = jnp.mean(x, axis=-1, keepdims=True)
    xc = x - mu
    var = jnp.mean(xc * xc, axis=-1, keepdims=True)
    return xc * lax.rsqrt(var + LN_EPS) * g + b


def _rms_norm_rows(x, g):
    return x * lax.rsqrt(jnp.mean(x * x, axis=-1, keepdims=True) + RMS_EPS) * g


def _rope_lanes(t, tab):
    c, s1, s2 = tab[:, :LANES], tab[:, LANES:2 * LANES], tab[:, 2 * LANES:]
    return t * c + pltpu.roll(t, 32, 1) * s1 + pltpu.roll(t, LANES - 32, 1) * s2


def _ln_kernel(x_ref, g_ref, b_ref, o_ref):
    o_ref[...] = _layer_norm_rows(x_ref[...], g_ref[...], b_ref[...]).astype(o_ref.dtype)


def _input_layer_norm(x, g, b, tm=256):
    s, d = x.shape
    return pl.pallas_call(
        _ln_kernel,
        out_shape=jax.ShapeDtypeStruct((s, d), BF16),
        grid=(s // tm,),
        in_specs=[pl.BlockSpec((tm, d), lambda i: (i, 0)),
                  pl.BlockSpec((1, d), lambda i: (0, 0)),
                  pl.BlockSpec((1, d), lambda i: (0, 0))],
        out_specs=pl.BlockSpec((tm, d), lambda i: (i, 0)),
        compiler_params=_params(("parallel",)),
        name="input_ln",
    )(x, g.reshape(1, d), b.reshape(1, d))


def _proj_kernel(a_ref, w_ref, o_ref, *, n_silu_blocks, gated):
    acc = _dot(a_ref[...], w_ref[...])
    if gated:
        sig = _sigmoid(acc)
        acc = jnp.where(pl.program_id(0) < n_silu_blocks, acc * sig, sig)
    o_ref[...] = acc.astype(o_ref.dtype)


def _proj(h, w, *, dil=1, tm=512, tn=1024, gated=False, n_silu_blocks=0, name):
    s, k = h.shape
    n = w.shape[1]
    rows_per_class = s // dil
    lb = rows_per_class // tm
    h_view = h.reshape(rows_per_class, dil * k)
    kern = functools.partial(_proj_kernel, n_silu_blocks=n_silu_blocks, gated=gated)
    return pl.pallas_call(
        kern,
        out_shape=jax.ShapeDtypeStruct((s, n), BF16),
        grid=(n // tn, dil, lb),
        in_specs=[pl.BlockSpec((tm, k), lambda j, r, i: (i, r)),
                  pl.BlockSpec((k, tn), lambda j, r, i: (0, j))],
        out_specs=pl.BlockSpec((tm, tn), lambda j, r, i: (r * lb + i, j)),
        compiler_params=_params(("parallel", "parallel", "parallel")),
        name=name,
    )(h_view, w)


def _mla_a_kernel(h_ref, w_ref, gq_ref, gkv_ref, tab_ref, qn_ref, ckvn_ref, kpe_ref):
    acc = _dot(h_ref[...], w_ref[...])
    qn_ref[...] = _rms_norm_rows(acc[:, :Q_LORA_RANK], gq_ref[...]).astype(BF16)
    ckv = acc[:, Q_LORA_RANK:Q_LORA_RANK + KV_LORA_RANK]
    ckvn_ref[...] = _rms_norm_rows(ckv, gkv_ref[...]).astype(BF16)
    kpe_ref[...] = _rope_lanes(acc[:, Q_LORA_RANK + KV_LORA_RANK:], tab_ref[...]).astype(BF16)


def _mla_a_proj(h, w, gq, gkv, k_tab, tm=256):
    s, k = h.shape
    return pl.pallas_call(
        _mla_a_kernel,
        out_shape=(jax.ShapeDtypeStruct((s, Q_LORA_RANK), BF16),
                   jax.ShapeDtypeStruct((s, KV_LORA_RANK), BF16),
                   jax.ShapeDtypeStruct((s, LANES), BF16)),
        grid=(s // tm,),
        in_specs=[pl.BlockSpec((tm, k), lambda i: (i, 0)),
                  pl.BlockSpec((k, MLA_A_WIDTH), lambda i: (0, 0)),
                  pl.BlockSpec((1, Q_LORA_RANK), lambda i: (0, 0)),
                  pl.BlockSpec((1, KV_LORA_RANK), lambda i: (0, 0)),
                  pl.BlockSpec((tm, 3 * LANES), lambda i: (i, 0))],
        out_specs=(pl.BlockSpec((tm, Q_LORA_RANK), lambda i: (i, 0)),
                   pl.BlockSpec((tm, KV_LORA_RANK), lambda i: (i, 0)),
                   pl.BlockSpec((tm, LANES), lambda i: (i, 0))),
        compiler_params=_params(("parallel",)),
        name="mla_a_proj",
    )(h, w, gq.reshape(1, -1), gkv.reshape(1, -1), k_tab)


def _mla_q_kernel(qn_ref, w_ref, tab_ref, q_ref, *, nope_scale):
    qn = qn_ref[...]
    tab = tab_ref[...]
    for hd in range(MLA_HEADS):
        lo = hd * QK_PAD_DIM
        acc = _dot(qn, w_ref[:, lo:lo + QK_PAD_DIM])
        q_ref[:, lo:lo + LANES] = (acc[:, :LANES] * nope_scale).astype(BF16)
        q_ref[:, lo + LANES:lo + QK_PAD_DIM] = _rope_lanes(acc[:, LANES:], tab).astype(BF16)


def _mla_q_proj(qn, w, q_tab, nope_scale, tm=512):
    s, k = qn.shape
    n = w.shape[1]
    return pl.pallas_call(
        functools.partial(_mla_q_kernel, nope_scale=nope_scale),
        out_shape=jax.ShapeDtypeStruct((s, n), BF16),
        grid=(s // tm,),
        in_specs=[pl.BlockSpec((tm, k), lambda i: (i, 0)),
                  pl.BlockSpec((k, n), lambda i: (0, 0)),
                  pl.BlockSpec((tm, 3 * LANES), lambda i: (i, 0))],
        out_specs=pl.BlockSpec((tm, n), lambda i: (i, 0)),
        compiler_params=_params(("parallel",)),
        name="mla_q_proj",
    )(qn, w, q_tab)


def _mla_attn_kernel(q_ref, k_ref, v_ref, kpe_ref, o_ref, kcat_ref, *, tk):
    @pl.when(pl.program_id(1) == 0)
    def _():
        kcat_ref[:, :LANES] = k_ref[...]
        kcat_ref[:, LANES:] = kpe_ref[...]

    q = q_ref[...]
    tq = q.shape[0]
    n_kv = k_ref.shape[0] // tk

    def body(j, carry):
        m, l, acc = carry
        start = pl.multiple_of(j * tk, tk)
        s = _dot_nt(q, kcat_ref[pl.ds(start, tk), :])
        m_new = jnp.maximum(m, jnp.max(s, axis=1, keepdims=True))
        alpha = jnp.exp2(m - m_new)
        p = jnp.exp2(s - m_new)
        l = alpha * l + jnp.sum(p, axis=1, keepdims=True)
        acc = alpha * acc + _dot(p.astype(BF16), v_ref[pl.ds(start, tk), :])
        return m_new, l, acc

    m0 = jnp.full((tq, 1), -jnp.inf, F32)
    l0 = jnp.zeros((tq, 1), F32)
    acc0 = jnp.zeros((tq, V_HEAD_DIM), F32)
    _, l, acc = lax.fori_loop(0, n_kv, body, (m0, l0, acc0))
    o_ref[...] = (acc / l).astype(o_ref.dtype)


def _mla_attention(q, kv, kpe, tq=256, tk=512):
    s = q.shape[0]
    return pl.pallas_call(
        functools.partial(_mla_attn_kernel, tk=tk),
        out_shape=jax.ShapeDtypeStruct((s, MLA_WIDTH), BF16),
        grid=(MLA_HEADS, s // tq),
        in_specs=[pl.BlockSpec((tq, QK_PAD_DIM), lambda h, i: (i, h)),
                  pl.BlockSpec((s, QK_NOPE_DIM), lambda h, i: (0, h)),
                  pl.BlockSpec((s, V_HEAD_DIM), lambda h, i: (0, MLA_HEADS + h)),
                  pl.BlockSpec((s, LANES), lambda h, i: (0, 0))],
        out_specs=pl.BlockSpec((tq, V_HEAD_DIM), lambda h, i: (i, h)),
        scratch_shapes=[pltpu.VMEM((s, QK_PAD_DIM), BF16)],
        compiler_params=_params(("parallel", "arbitrary")),
        name="mla_attention",
    )(q, kv, kv, kpe)


def _band_bucket_index(dil, tq):
    nb = REL_BUCKETS // 2
    max_exact = nb // 2
    rel = (np.arange(3 * tq)[None, :] - tq - np.arange(tq)[:, None]) * dil
    n = np.abs(rel)
    nf = np.maximum(n, 1).astype(np.float64)
    large = max_exact + (np.log(nf / max_exact) / math.log(REL_MAX_DISTANCE / max_exact)
                         * (nb - max_exact)).astype(np.int32)
    large = np.minimum(large, nb - 1)
    return (np.where(rel > 0, nb, 0) + np.where(n < max_exact, n, large)).astype(np.int32)


def _band_bias_kernel(rb_ref, idx_ref, o_ref):
    hd = pl.program_id(0)
    idx = idx_ref[0]
    acc = jnp.zeros(idx.shape, F32)
    for b in range(REL_BUCKETS):
        acc = jnp.where(idx == b, rb_ref[b, hd], acc)
    o_ref[0] = acc


def _band_bias(rel_bias, tq):
    idx = jnp.asarray(np.stack([_band_bucket_index(dil, tq) for _, dil in DIL_GROUPS]))
    return pl.pallas_call(
        _band_bias_kernel,
        out_shape=jax.ShapeDtypeStruct((DSWA_HEADS, tq, 3 * tq), F32),
        grid=(DSWA_HEADS,),
        in_specs=[pl.BlockSpec(memory_space=pltpu.SMEM),
                  pl.BlockSpec((1, tq, 3 * tq), lambda h: (h // DSWA_HEADS_PER_GROUP, 0, 0))],
        out_specs=pl.BlockSpec((1, tq, 3 * tq), lambda h: (h, 0, 0)),
        compiler_params=_params(("parallel",)),
        name="band_bias",
    )(rel_bias, idx)


def _band_attn_kernel(q_ref, kp_ref, ko_ref, kn_ref, vp_ref, vo_ref, vn_ref, bias_ref,
                      o_ref, lse_ref, *, tiles_per_class):
    tq = q_ref.shape[0]
    pos = pl.program_id(0) % tiles_per_class
    prev_ok = pos != 0
    next_ok = pos != tiles_per_class - 1
    row = lax.broadcasted_iota(jnp.int32, (tq, 3 * tq), 0)
    col = lax.broadcasted_iota(jnp.int32, (tq, 3 * tq), 1)
    rel = col - tq - row
    mask = (jnp.abs(rel) <= BAND_HALF) & ((col >= tq) | prev_ok) & ((col < 2 * tq) | next_ok)
    lane = lax.broadcasted_iota(jnp.int32, (tq, LANES), 1)
    lse_tile = jnp.zeros((tq, LANES), F32)
    scale = DSWA_HEAD_DIM ** -0.5
    for hd in range(DSWA_HEADS_PER_GROUP):
        cols = slice(hd * DSWA_HEAD_DIM, (hd + 1) * DSWA_HEAD_DIM)
        kc = jnp.concatenate([kp_ref[:, cols], ko_ref[:, cols], kn_ref[:, cols]], axis=0)
        vc = jnp.concatenate([vp_ref[:, cols], vo_ref[:, cols], vn_ref[:, cols]], axis=0)
        s = _dot_nt(q_ref[:, cols], kc) * scale + bias_ref[hd]
        s = jnp.where(mask, s, NEG_INF)
        m = jnp.max(s, axis=1, keepdims=True)
        p = jnp.exp(s - m)
        den = jnp.sum(p, axis=1, keepdims=True)
        o_ref[:, cols] = _dot((p / den).astype(BF16), vc).astype(o_ref.dtype)
        lse_tile = jnp.where(lane == hd, m + jnp.log(den), lse_tile)
    lse_ref[...] = lse_tile


def _band_attention(qkv, bias, group, dil, tq=128):
    s = qkv.shape[0]
    nt = s // tq
    tiles_per_class = (s // dil) // tq

    def spec(col_block, shift):
        return pl.BlockSpec(
            (tq, DSWA_WIDTH),
            lambda n: (jnp.clip(n + shift, 0, nt - 1), col_block))

    return pl.pallas_call(
        functools.partial(_band_attn_kernel, tiles_per_class=tiles_per_class),
        out_shape=(jax.ShapeDtypeStruct((s, DSWA_WIDTH), BF16),
                   jax.ShapeDtypeStruct((s, LANES), F32)),
        grid=(nt,),
        in_specs=[spec(0, 0),
                  spec(1, -1), spec(1, 0), spec(1, 1),
                  spec(2, -1), spec(2, 0), spec(2, 1),
                  pl.BlockSpec((DSWA_HEADS_PER_GROUP, tq, 3 * tq), lambda n: (group, 0, 0))],
        out_specs=(pl.BlockSpec((tq, DSWA_WIDTH), lambda n: (n, 0)),
                   pl.BlockSpec((tq, LANES), lambda n: (n, 0))),
        compiler_params=_params(("parallel",)),
        name=f"band_attention_d{dil}",
    )(qkv, qkv, qkv, qkv, qkv, qkv, qkv, bias)


def _mix_kernel(o1_ref, o2_ref, o3_ref, l1_ref, l2_ref, l3_ref, sg_ref, out_ref):
    for hd in range(DSWA_HEADS_PER_GROUP):
        cols = slice(hd * DSWA_HEAD_DIM, (hd + 1) * DSWA_HEAD_DIM)
        l1 = l1_ref[:, hd:hd + 1]
        l2 = l2_ref[:, hd:hd + 1]
        l3 = l3_ref[:, hd:hd + 1]
        mx = jnp.maximum(jnp.maximum(l1, l2), l3)
        e1, e2, e3 = jnp.exp(l1 - mx), jnp.exp(l2 - mx), jnp.exp(l3 - mx)
        den = e1 + e2 + e3
        mix = ((e1 / den) * o1_ref[:, cols].astype(F32)
               + (e2 / den) * o2_ref[:, cols].astype(F32)
               + (e3 / den) * o3_ref[:, cols].astype(F32))
        out_ref[:, cols] = (mix * sg_ref[:, cols].astype(F32)).astype(out_ref.dtype)


def _group_mixture(outs, lses, silu_gates):
    s = SEQ
    coarse = DIL_GROUPS[2][1]
    mid = DIL_GROUPS[1][1]
    rows = s // coarse
    o1, o2, o3 = outs
    l1, l2, l3 = lses
    w, lw = DSWA_WIDTH, LANES
    gate_blocks = silu_gates.shape[1] // w
    gate_col = MLA_WIDTH // w
    return pl.pallas_call(
        _mix_kernel,
        out_shape=jax.ShapeDtypeStruct((rows, coarse * w), BF16),
        grid=(coarse,),
        in_specs=[pl.BlockSpec((rows, w), lambda r: (0, r)),
                  pl.BlockSpec((rows, w), lambda r: (r % mid, r // mid)),
                  pl.BlockSpec((rows, w), lambda r: (r, 0)),
                  pl.BlockSpec((rows, lw), lambda r: (0, r)),
                  pl.BlockSpec((rows, lw), lambda r: (r % mid, r // mid)),
                  pl.BlockSpec((rows, lw), lambda r: (r, 0)),
                  pl.BlockSpec((rows, w), lambda r: (0, r * gate_blocks + gate_col))],
        out_specs=pl.BlockSpec((rows, w), lambda r: (0, r)),
        compiler_params=_params(("parallel",)),
        name="group_mixture",
    )(o1.reshape(rows, coarse * w),
      o2.reshape(mid * rows, (coarse // mid) * w),
      o3,
      l1.reshape(rows, coarse * lw),
      l2.reshape(mid * rows, (coarse // mid) * lw),
      l3,
      silu_gates.reshape(rows, coarse * silu_gates.shape[1])).reshape(s, w)


def _out_kernel(a_ref, sg_ref, b_ref, wm_ref, wd_ref, rm_ref, rd_ref, wo_ref, o_ref, am_ref):
    k = pl.program_id(1)

    @pl.when(k == 0)
    def _():
        am_ref[...] = (a_ref[...].astype(F32) * sg_ref[...].astype(F32)).astype(BF16)

    y_mla = _dot(am_ref[...], wm_ref[...])
    y_dswa = _dot(b_ref[...], wd_ref[...])
    merged = rm_ref[...].astype(F32) * y_mla + rd_ref[...].astype(F32) * y_dswa
    contrib = _dot(merged.astype(BF16), wo_ref[...])

    @pl.when(k == 0)
    def _():
        o_ref[...] = contrib

    @pl.when(k != 0)
    def _():
        o_ref[...] += contrib


def _output_projection(a, gates, b_gated, w_o_mla, w_o_dswa, w_out, tm=256, tk=512):
    s = a.shape[0]
    d = w_out.shape[1]
    r_dswa_block = (MLA_WIDTH + DSWA_WIDTH + D_MODEL) // tk
    r_mla_block = (MLA_WIDTH + DSWA_WIDTH) // tk
    return pl.pallas_call(
        _out_kernel,
        out_shape=jax.ShapeDtypeStruct((s, d), F32),
        grid=(s // tm, d // tk),
        in_specs=[pl.BlockSpec((tm, MLA_WIDTH), lambda i, k: (i, 0)),
                  pl.BlockSpec((tm, MLA_WIDTH), lambda i, k: (i, 0)),
                  pl.BlockSpec((tm, DSWA_WIDTH), lambda i, k: (i, 0)),
                  pl.BlockSpec((MLA_WIDTH, tk), lambda i, k: (0, k)),
                  pl.BlockSpec((DSWA_WIDTH, tk), lambda i, k: (0, k)),
                  pl.BlockSpec((tm, tk), lambda i, k: (i, r_mla_block + k)),
                  pl.BlockSpec((tm, tk), lambda i, k: (i, r_dswa_block + k)),
                  pl.BlockSpec((tk, d), lambda i, k: (k, 0))],
        out_specs=pl.BlockSpec((tm, d), lambda i, k: (i, 0)),
        scratch_shapes=[pltpu.VMEM((tm, MLA_WIDTH), BF16)],
        compiler_params=_params(("parallel", "arbitrary")),
        name="output_projection",
    )(a, gates, b_gated, w_o_mla, w_o_dswa, gates, gates, w_out)


def _final_ln_kernel(x_ref, o_ref, g0_ref, b0_ref, g_ref, b_ref, out_ref):
    h = _layer_norm_rows(x_ref[...], g0_ref[...], b0_ref[...])
    out_ref[...] = _layer_norm_rows(DEEPNORM_ALPHA * h + o_ref[...], g_ref[...], b_ref[...])


def _final_layer_norm(x, out, g0, b0, g, b, tm=256):
    s, d = x.shape
    row = pl.BlockSpec((tm, d), lambda i: (i, 0))
    vec = pl.BlockSpec((1, d), lambda i: (0, 0))
    return pl.pallas_call(
        _final_ln_kernel,
        out_shape=jax.ShapeDtypeStruct((s, d), F32),
        grid=(s // tm,),
        in_specs=[row, row, vec, vec, vec, vec],
        out_specs=row,
        compiler_params=_params(("parallel",)),
        name="final_ln",
    )(x, out, g0.reshape(1, d), b0.reshape(1, d), g.reshape(1, d), b.reshape(1, d))


def _rope_tables(seq, scale):
    pos = jnp.arange(seq, dtype=F32)
    inv_freq = 1.0 / (ROPE_THETA ** (jnp.arange(0, QK_ROPE_DIM, 2, dtype=F32) / QK_ROPE_DIM))
    ang = pos[:, None] * inv_freq[None, :]
    cos, sin = jnp.cos(ang) * scale, jnp.sin(ang) * scale
    z32 = jnp.zeros_like(cos)
    z64 = jnp.zeros((seq, LANES - QK_ROPE_DIM), F32)
    return jnp.concatenate([cos, cos, z64, z32, sin, z64, -sin, z32, z64], axis=1)


def kernel(x, emb_ln_g, emb_ln_b, rel_bias, w_in, q_a_norm_g, w_q_b, kv_a_norm_g, w_kv_b,
           w_o_mla, w_o_dswa, w_out, ln_g, ln_b):
    assert DEPTH == 1 and x.shape == (1, SEQ, D_MODEL)
    x2 = x[0]
    win = w_in[0]

    off = IN_OFFSETS
    w_mla_a = jnp.pad(win[:, :off[1]], ((0, 0), (0, MLA_A_WIDTH - off[1]))).astype(BF16)
    w_groups = []
    for g in range(len(DIL_GROUPS)):
        lo = g * DSWA_WIDTH
        w_groups.append(jnp.concatenate(
            [win[:, off[1] + lo:off[1] + lo + DSWA_WIDTH],
             win[:, off[2] + lo:off[2] + lo + DSWA_WIDTH],
             win[:, off[3] + lo:off[3] + lo + DSWA_WIDTH]], axis=1).astype(BF16))
    w_gate = win[:, off[4]:].astype(BF16)
    w_q = jnp.pad(w_q_b[0].reshape(Q_LORA_RANK, MLA_HEADS, QK_HEAD_DIM),
                  ((0, 0), (0, 0), (0, QK_PAD_DIM - QK_HEAD_DIM))
                  ).reshape(Q_LORA_RANK, MLA_HEADS * QK_PAD_DIM).astype(BF16)
    w_kv = (w_kv_b[0].reshape(KV_LORA_RANK, MLA_HEADS, 2, QK_NOPE_DIM)
            .transpose(0, 2, 1, 3).reshape(KV_LORA_RANK, 2 * MLA_WIDTH).astype(BF16))
    w_om = w_o_mla[0].astype(BF16)
    w_od = w_o_dswa[0].astype(BF16)
    w_o = w_out[0].astype(BF16)

    q_scale = QK_HEAD_DIM ** -0.5 * math.log2(math.e)
    k_tab = _rope_tables(SEQ, 1.0)
    q_tab = _rope_tables(SEQ, q_scale)

    h = _input_layer_norm(x2, emb_ln_g, emb_ln_b)

    qn, ckvn, kpe = _mla_a_proj(h, w_mla_a, q_a_norm_g[0], kv_a_norm_g[0], k_tab)
    q = _mla_q_proj(qn, w_q, q_tab, q_scale)
    kv = _proj(ckvn, w_kv, tm=512, tn=1024, name="mla_kv_proj")
    a = _mla_attention(q, kv, kpe)

    gates = _proj(h, w_gate, gated=True, n_silu_blocks=(MLA_WIDTH + DSWA_WIDTH) // 1024,
                  tn=1024, name="gate_proj")

    bias = _band_bias(rel_bias, 128)
    outs, lses = [], []
    for g, (_, dil) in enumerate(DIL_GROUPS):
        qkv = _proj(h, w_groups[g], dil=dil, name=f"dswa_proj_d{dil}")
        o, lse = _band_attention(qkv, bias, g, dil)
        outs.append(o)
        lses.append(lse)
    b_gated = _group_mixture(outs, lses, gates)

    out = _output_projection(a, gates, b_gated, w_om, w_od, w_o)
    y = _final_layer_norm(x2, out, emb_ln_g, emb_ln_b, ln_g[0], ln_b[0])
    return y[None]
```

```python
import functools
import math

import jax
import jax.numpy as jnp
import numpy as np
from jax import lax
from jax.experimental import pallas as pl
from jax.experimental.pallas import tpu as pltpu

D_MODEL = 4096
SEQ = 8192
DEPTH = 1

MLA_HEADS = 16
Q_LORA_RANK = 1024
KV_LORA_RANK = 512
QK_NOPE_DIM = 128
QK_ROPE_DIM = 64
QK_HEAD_DIM = QK_NOPE_DIM + QK_ROPE_DIM
V_HEAD_DIM = 128
MLA_WIDTH = MLA_HEADS * V_HEAD_DIM
ROPE_THETA = 10000.0

DIL_GROUPS = ((128, 1), (512, 4), (2048, 16))
DSWA_HEADS_PER_GROUP = 8
DSWA_HEADS = 24
DSWA_HEAD_DIM = 128
DSWA_QKV_WIDTH = DSWA_HEADS * DSWA_HEAD_DIM
DSWA_WIDTH = DSWA_HEADS_PER_GROUP * DSWA_HEAD_DIM
BAND_HALF = 64

REL_BUCKETS = 32
REL_MAX_DISTANCE = 1024

DEEPNORM_ALPHA = (2.0 * DEPTH) ** 0.25
LN_EPS = 1e-5
RMS_EPS = 1e-6
NEG_INF = -1e30

IN_SPLITS = (Q_LORA_RANK, KV_LORA_RANK + QK_ROPE_DIM, DSWA_QKV_WIDTH, DSWA_QKV_WIDTH,
             DSWA_QKV_WIDTH, MLA_WIDTH, DSWA_WIDTH, D_MODEL, D_MODEL)
IN_OFFSETS = tuple(int(v) for v in np.cumsum(IN_SPLITS)[:-1])

LANES = 128
QK_PAD_DIM = 256
MLA_A_WIDTH = Q_LORA_RANK + KV_LORA_RANK + LANES
GATE_WIDTH = MLA_WIDTH + DSWA_WIDTH + 2 * D_MODEL
VMEM_LIMIT = 56 * 1024 * 1024

BF16 = jnp.bfloat16
F32 = jnp.float32


def _params(sem):
    return pltpu.CompilerParams(dimension_semantics=sem, vmem_limit_bytes=VMEM_LIMIT)


def _dot(a, b):
    return jnp.dot(a, b, preferred_element_type=F32)


def _dot_nt(a, b):
    return lax.dot_general(a, b, (((1,), (1,)), ((), ())), preferred_element_type=F32)


def _sigmoid(x):
    return 1.0 / (1.0 + jnp.exp(-x))


def _layer_norm_rows(x, g, b):
    mu = jnp.mean(x, axis=-1, keepdims=True)
    xc = x - mu
    var = jnp.mean(xc * xc, axis=-1, keepdims=True)
    return xc * lax.rsqrt(var + LN_EPS) * g + b


def _rms_norm_rows(x, g):
    return x * lax.rsqrt(jnp.mean(x * x, axis=-1, keepdims=True) + RMS_EPS) * g


def _rope_lanes(t, tab):
    c, s1, s2 = tab[:, :LANES], tab[:, LANES:2 * LANES], tab[:, 2 * LANES:]
    return t * c + pltpu.roll(t, 32, 1) * s1 + pltpu.roll(t, LANES - 32, 1) * s2


def _ln_kernel(x_ref, g_ref, b_ref, o_ref):
    o_ref[...] = _layer_norm_rows(x_ref[...], g_ref[...], b_ref[...]).astype(o_ref.dtype)


def _input_layer_norm(x, g, b, tm=256):
    s, d = x.shape
    return pl.pallas_call(
        _ln_kernel,
        out_shape=jax.ShapeDtypeStruct((s, d), BF16),
        grid=(s // tm,),
        in_specs=[pl.BlockSpec((tm, d), lambda i: (i, 0)),
                  pl.BlockSpec((1, d), lambda i: (0, 0)),
                  pl.BlockSpec((1, d), lambda i: (0, 0))],
        out_specs=pl.BlockSpec((tm, d), lambda i: (i, 0)),
        compiler_params=_params(("parallel",)),
        name="input_ln",
    )(x, g.reshape(1, d), b.reshape(1, d))


def _proj_kernel(a_ref, w_ref, o_ref, *, n_silu_blocks, gated):
    acc = _dot(a_ref[...], w_ref[...])
    if gated:
        sig = _sigmoid(acc)
        acc = jnp.where(pl.program_id(0) < n_silu_blocks, acc * sig, sig)
    o_ref[...] = acc.astype(o_ref.dtype)


def _proj(h, w, *, dil=1, tm=512, tn=1024, gated=False, n_silu_blocks=0, name):
    s, k = h.shape
    n = w.shape[1]
    rows_per_class = s // dil
    lb = rows_per_class // tm
    h_view = h.reshape(rows_per_class, dil * k)
    kern = functools.partial(_proj_kernel, n_silu_blocks=n_silu_blocks, gated=gated)
    return pl.pallas_call(
        kern,
        out_shape=jax.ShapeDtypeStruct((s, n), BF16),
        grid=(n // tn, dil, lb),
        in_specs=[pl.BlockSpec((tm, k), lambda j, r, i: (i, r)),
                  pl.BlockSpec((k, tn), lambda j, r, i: (0, j))],
        out_specs=pl.BlockSpec((tm, tn), lambda j, r, i: (r * lb + i, j)),
        compiler_params=_params(("parallel", "parallel", "parallel")),
        name=name,
    )(h_view, w)


def _mla_a_kernel(h_ref, w_ref, gq_ref, gkv_ref, tab_ref, qn_ref, ckvn_ref, kpe_ref):
    acc = _dot(h_ref[...], w_ref[...])
    qn_ref[...] = _rms_norm_rows(acc[:, :Q_LORA_RANK], gq_ref[...]).astype(BF16)
    ckv = acc[:, Q_LORA_RANK:Q_LORA_RANK + KV_LORA_RANK]
    ckvn_ref[...] = _rms_norm_rows(ckv, gkv_ref[...]).astype(BF16)
    kpe_ref[...] = _rope_lanes(acc[:, Q_LORA_RANK + KV_LORA_RANK:], tab_ref[...]).astype(BF16)


def _mla_a_proj(h, w, gq, gkv, k_tab, tm=256):
    s, k = h.shape
    return pl.pallas_call(
        _mla_a_kernel,
        out_shape=(jax.ShapeDtypeStruct((s, Q_LORA_RANK), BF16),
                   jax.ShapeDtypeStruct((s, KV_LORA_RANK), BF16),
                   jax.ShapeDtypeStruct((s, LANES), BF16)),
        grid=(s // tm,),
        in_specs=[pl.BlockSpec((tm, k), lambda i: (i, 0)),
                  pl.BlockSpec((k, MLA_A_WIDTH), lambda i: (0, 0)),
                  pl.BlockSpec((1, Q_LORA_RANK), lambda i: (0, 0)),
                  pl.BlockSpec((1, KV_LORA_RANK), lambda i: (0, 0)),
                  pl.BlockSpec((tm, 3 * LANES), lambda i: (i, 0))],
        out_specs=(pl.BlockSpec((tm, Q_LORA_RANK), lambda i: (i, 0)),
                   pl.BlockSpec((tm, KV_LORA_RANK), lambda i: (i, 0)),
                   pl.BlockSpec((tm, LANES), lambda i: (i, 0))),
        compiler_params=_params(("parallel",)),
        name="mla_a_proj",
    )(h, w, gq.reshape(1, -1), gkv.reshape(1, -1), k_tab)


def _mla_q_kernel(qn_ref, w_ref, tab_ref, q_ref, *, nope_scale):
    qn = qn_ref[...]
    tab = tab_ref[...]
    for hd in range(MLA_HEADS):
        lo = hd * QK_PAD_DIM
        acc = _dot(qn, w_ref[:, lo:lo + QK_PAD_DIM])
        q_ref[:, lo:lo + LANES] = (acc[:, :LANES] * nope_scale).astype(BF16)
        q_ref[:, lo + LANES:lo + QK_PAD_DIM] = _rope_lanes(acc[:, LANES:], tab).astype(BF16)


def _mla_q_proj(qn, w, q_tab, nope_scale, tm=512):
    s, k = qn.shape
    n = w.shape[1]
    return pl.pallas_call(
        functools.partial(_mla_q_kernel, nope_scale=nope_scale),
        out_shape=jax.ShapeDtypeStruct((s, n), BF16),
        grid=(s // tm,),
        in_specs=[pl.BlockSpec((tm, k), lambda i: (i, 0)),
                  pl.BlockSpec((k, n), lambda i: (0, 0)),
                  pl.BlockSpec((tm, 3 * LANES), lambda i: (i, 0))],
        out_specs=pl.BlockSpec((tm, n), lambda i: (i, 0)),
        compiler_params=_params(("parallel",)),
        name="mla_q_proj",
    )(qn, w, q_tab)


def _mla_attn_kernel(q_ref, k_ref, v_ref, kpe_ref, o_ref, kcat_ref, vcat_ref, *, tk):
    @pl.when(pl.program_id(1) == 0)
    def _():
        kcat_ref[:, :LANES] = k_ref[...]
        kcat_ref[:, LANES:] = kpe_ref[...]
        vcat_ref[:, :V_HEAD_DIM] = v_ref[...]
        vcat_ref[:, V_HEAD_DIM:] = jnp.ones((v_ref.shape[0], LANES), BF16)

    q = q_ref[...]
    tq = q.shape[0]
    n_kv = k_ref.shape[0] // tk

    def body(j, carry):
        m, acc = carry
        start = pl.multiple_of(j * tk, tk)
        s = _dot_nt(q, kcat_ref[pl.ds(start, tk), :])
        m_new = jnp.maximum(m, jnp.max(s, axis=1, keepdims=True))
        alpha = jnp.exp2(m - m_new)
        p = jnp.exp2(s - m_new).astype(BF16)
        acc = alpha * acc + _dot(p, vcat_ref[pl.ds(start, tk), :])
        return m_new, acc

    m0 = jnp.full((tq, 1), -jnp.inf, F32)
    acc0 = jnp.zeros((tq, V_HEAD_DIM + LANES), F32)
    _, acc = lax.fori_loop(0, n_kv, body, (m0, acc0), unroll=True)
    o_ref[...] = (acc[:, :V_HEAD_DIM] / acc[:, V_HEAD_DIM:]).astype(o_ref.dtype)


def _mla_attention(q, kv, kpe, tq=512, tk=1024):
    s = q.shape[0]
    return pl.pallas_call(
        functools.partial(_mla_attn_kernel, tk=tk),
        out_shape=jax.ShapeDtypeStruct((s, MLA_WIDTH), BF16),
        grid=(MLA_HEADS, s // tq),
        in_specs=[pl.BlockSpec((tq, QK_PAD_DIM), lambda h, i: (i, h)),
                  pl.BlockSpec((s, QK_NOPE_DIM), lambda h, i: (0, h)),
                  pl.BlockSpec((s, V_HEAD_DIM), lambda h, i: (0, MLA_HEADS + h)),
                  pl.BlockSpec((s, LANES), lambda h, i: (0, 0))],
        out_specs=pl.BlockSpec((tq, V_HEAD_DIM), lambda h, i: (i, h)),
        scratch_shapes=[pltpu.VMEM((s, QK_PAD_DIM), BF16),
                        pltpu.VMEM((s, V_HEAD_DIM + LANES), BF16)],
        compiler_params=_params(("parallel", "arbitrary")),
        name="mla_attention",
    )(q, kv, kv, kpe)


def _band_bucket_index(dil, tq):
    nb = REL_BUCKETS // 2
    max_exact = nb // 2
    rel = (np.arange(3 * tq)[None, :] - tq - np.arange(tq)[:, None]) * dil
    n = np.abs(rel)
    nf = np.maximum(n, 1).astype(np.float64)
    large = max_exact + (np.log(nf / max_exact) / math.log(REL_MAX_DISTANCE / max_exact)
                         * (nb - max_exact)).astype(np.int32)
    large = np.minimum(large, nb - 1)
    return (np.where(rel > 0, nb, 0) + np.where(n < max_exact, n, large)).astype(np.int32)


def _band_bias_kernel(rb_ref, idx_ref, o_ref):
    hd = pl.program_id(0)
    idx = idx_ref[0]
    acc = jnp.zeros(idx.shape, F32)
    for b in range(REL_BUCKETS):
        acc = jnp.where(idx == b, rb_ref[b, hd], acc)
    o_ref[0] = acc


def _band_bias(rel_bias, tq):
    idx = jnp.asarray(np.stack([_band_bucket_index(dil, tq) for _, dil in DIL_GROUPS]))
    return pl.pallas_call(
        _band_bias_kernel,
        out_shape=jax.ShapeDtypeStruct((DSWA_HEADS, tq, 3 * tq), F32),
        grid=(DSWA_HEADS,),
        in_specs=[pl.BlockSpec(memory_space=pltpu.SMEM),
                  pl.BlockSpec((1, tq, 3 * tq), lambda h: (h // DSWA_HEADS_PER_GROUP, 0, 0))],
        out_specs=pl.BlockSpec((1, tq, 3 * tq), lambda h: (h, 0, 0)),
        compiler_params=_params(("parallel",)),
        name="band_bias",
    )(rel_bias, idx)


def _band_attn_kernel(q_ref, kp_ref, ko_ref, kn_ref, vp_ref, vo_ref, vn_ref, bias_ref,
                      o_ref, lse_ref, *, tiles_per_class):
    tq = q_ref.shape[0]
    pos = pl.program_id(0) % tiles_per_class
    prev_ok = pos != 0
    next_ok = pos != tiles_per_class - 1
    row = lax.broadcasted_iota(jnp.int32, (tq, 3 * tq), 0)
    col = lax.broadcasted_iota(jnp.int32, (tq, 3 * tq), 1)
    rel = col - tq - row
    mask = (jnp.abs(rel) <= BAND_HALF) & ((col >= tq) | prev_ok) & ((col < 2 * tq) | next_ok)
    lane = lax.broadcasted_iota(jnp.int32, (tq, LANES), 1)
    lse_tile = jnp.zeros((tq, LANES), F32)
    scale = DSWA_HEAD_DIM ** -0.5
    for hd in range(DSWA_HEADS_PER_GROUP):
        cols = slice(hd * DSWA_HEAD_DIM, (hd + 1) * DSWA_HEAD_DIM)
        kc = jnp.concatenate([kp_ref[:, cols], ko_ref[:, cols], kn_ref[:, cols]], axis=0)
        vc = jnp.concatenate([vp_ref[:, cols], vo_ref[:, cols], vn_ref[:, cols]], axis=0)
        s = _dot_nt(q_ref[:, cols], kc) * scale + bias_ref[hd]
        s = jnp.where(mask, s, NEG_INF)
        m = jnp.max(s, axis=1, keepdims=True)
        p = jnp.exp(s - m)
        den = jnp.sum(p, axis=1, keepdims=True)
        o_ref[:, cols] = _dot((p / den).astype(BF16), vc).astype(o_ref.dtype)
        lse_tile = jnp.where(lane == hd, m + jnp.log(den), lse_tile)
    lse_ref[...] = lse_tile


def _band_attention(qkv, bias, group, dil, tq=128):
    s = qkv.shape[0]
    nt = s // tq
    tiles_per_class = (s // dil) // tq

    def spec(col_block, shift):
        return pl.BlockSpec(
            (tq, DSWA_WIDTH),
            lambda n: (jnp.clip(n + shift, 0, nt - 1), col_block))

    return pl.pallas_call(
        functools.partial(_band_attn_kernel, tiles_per_class=tiles_per_class),
        out_shape=(jax.ShapeDtypeStruct((s, DSWA_WIDTH), BF16),
                   jax.ShapeDtypeStruct((s, LANES), F32)),
        grid=(nt,),
        in_specs=[spec(0, 0),
                  spec(1, -1), spec(1, 0), spec(1, 1),
                  spec(2, -1), spec(2, 0), spec(2, 1),
                  pl.BlockSpec((DSWA_HEADS_PER_GROUP, tq, 3 * tq), lambda n: (group, 0, 0))],
        out_specs=(pl.BlockSpec((tq, DSWA_WIDTH), lambda n: (n, 0)),
                   pl.BlockSpec((tq, LANES), lambda n: (n, 0))),
        compiler_params=_params(("parallel",)),
        name=f"band_attention_d{dil}",
    )(qkv, qkv, qkv, qkv, qkv, qkv, qkv, bias)


def _mix_kernel(o1_ref, o2_ref, o3_ref, l1_ref, l2_ref, l3_ref, sg_ref, out_ref):
    for hd in range(DSWA_HEADS_PER_GROUP):
        cols = slice(hd * DSWA_HEAD_DIM, (hd + 1) * DSWA_HEAD_DIM)
        l1 = l1_ref[:, hd:hd + 1]
        l2 = l2_ref[:, hd:hd + 1]
        l3 = l3_ref[:, hd:hd + 1]
        mx = jnp.maximum(jnp.maximum(l1, l2), l3)
        e1, e2, e3 = jnp.exp(l1 - mx), jnp.exp(l2 - mx), jnp.exp(l3 - mx)
        den = e1 + e2 + e3
        mix = ((e1 / den) * o1_ref[:, cols].astype(F32)
               + (e2 / den) * o2_ref[:, cols].astype(F32)
               + (e3 / den) * o3_ref[:, cols].astype(F32))
        out_ref[:, cols] = (mix * sg_ref[:, cols].astype(F32)).astype(out_ref.dtype)


def _group_mixture(outs, lses, silu_gates):
    s = SEQ
    coarse = DIL_GROUPS[2][1]
    mid = DIL_GROUPS[1][1]
    rows = s // coarse
    o1, o2, o3 = outs
    l1, l2, l3 = lses
    w, lw = DSWA_WIDTH, LANES
    gate_blocks = silu_gates.shape[1] // w
    gate_col = MLA_WIDTH // w
    return pl.pallas_call(
        _mix_kernel,
        out_shape=jax.ShapeDtypeStruct((rows, coarse * w), BF16),
        grid=(coarse,),
        in_specs=[pl.BlockSpec((rows, w), lambda r: (0, r)),
                  pl.BlockSpec((rows, w), lambda r: (r % mid, r // mid)),
                  pl.BlockSpec((rows, w), lambda r: (r, 0)),
                  pl.BlockSpec((rows, lw), lambda r: (0, r)),
                  pl.BlockSpec((rows, lw), lambda r: (r % mid, r // mid)),
                  pl.BlockSpec((rows, lw), lambda r: (r, 0)),
                  pl.BlockSpec((rows, w), lambda r: (0, r * gate_blocks + gate_col))],
        out_specs=pl.BlockSpec((rows, w), lambda r: (0, r)),
        compiler_params=_params(("parallel",)),
        name="group_mixture",
    )(o1.reshape(rows, coarse * w),
      o2.reshape(mid * rows, (coarse // mid) * w),
      o3,
      l1.reshape(rows, coarse * lw),
      l2.reshape(mid * rows, (coarse // mid) * lw),
      l3,
      silu_gates.reshape(rows, coarse * silu_gates.shape[1])).reshape(s, w)


def _out_kernel(a_ref, sg_ref, b_ref, wm_ref, wd_ref, rm_ref, rd_ref, wo_ref, o_ref, am_ref):
    k = pl.program_id(1)

    @pl.when(k == 0)
    def _():
        am_ref[...] = (a_ref[...].astype(F32) * sg_ref[...].astype(F32)).astype(BF16)

    y_mla = _dot(am_ref[...], wm_ref[...])
    y_dswa = _dot(b_ref[...], wd_ref[...])
    merged = rm_ref[...].astype(F32) * y_mla + rd_ref[...].astype(F32) * y_dswa
    contrib = _dot(merged.astype(BF16), wo_ref[...])

    @pl.when(k == 0)
    def _():
        o_ref[...] = contrib

    @pl.when(k != 0)
    def _():
        o_ref[...] += contrib


def _output_projection(a, gates, b_gated, w_o_mla, w_o_dswa, w_out, tm=256, tk=512):
    s = a.shape[0]
    d = w_out.shape[1]
    r_dswa_block = (MLA_WIDTH + DSWA_WIDTH + D_MODEL) // tk
    r_mla_block = (MLA_WIDTH + DSWA_WIDTH) // tk
    return pl.pallas_call(
        _out_kernel,
        out_shape=jax.ShapeDtypeStruct((s, d), F32),
        grid=(s // tm, d // tk),
        in_specs=[pl.BlockSpec((tm, MLA_WIDTH), lambda i, k: (i, 0)),
                  pl.BlockSpec((tm, MLA_WIDTH), lambda i, k: (i, 0)),
                  pl.BlockSpec((tm, DSWA_WIDTH), lambda i, k: (i, 0)),
                  pl.BlockSpec((MLA_WIDTH, tk), lambda i, k: (0, k)),
                  pl.BlockSpec((DSWA_WIDTH, tk), lambda i, k: (0, k)),
                  pl.BlockSpec((tm, tk), lambda i, k: (i, r_mla_block + k)),
                  pl.BlockSpec((tm, tk), lambda i, k: (i, r_dswa_block + k)),
                  pl.BlockSpec((tk, d), lambda i, k: (k, 0))],
        out_specs=pl.BlockSpec((tm, d), lambda i, k: (i, 0)),
        scratch_shapes=[pltpu.VMEM((tm, MLA_WIDTH), BF16)],
        compiler_params=_params(("parallel", "arbitrary")),
        name="output_projection",
    )(a, gates, b_gated, w_o_mla, w_o_dswa, gates, gates, w_out)


def _final_ln_kernel(x_ref, o_ref, g0_ref, b0_ref, g_ref, b_ref, out_ref):
    h = _layer_norm_rows(x_ref[...], g0_ref[...], b0_ref[...])
    out_ref[...] = _layer_norm_rows(DEEPNORM_ALPHA * h + o_ref[...], g_ref[...], b_ref[...])


def _final_layer_norm(x, out, g0, b0, g, b, tm=256):
    s, d = x.shape
    row = pl.BlockSpec((tm, d), lambda i: (i, 0))
    vec = pl.BlockSpec((1, d), lambda i: (0, 0))
    return pl.pallas_call(
        _final_ln_kernel,
        out_shape=jax.ShapeDtypeStruct((s, d), F32),
        grid=(s // tm,),
        in_specs=[row, row, vec, vec, vec, vec],
        out_specs=row,
        compiler_params=_params(("parallel",)),
        name="final_ln",
    )(x, out, g0.reshape(1, d), b0.reshape(1, d), g.reshape(1, d), b.reshape(1, d))


def _rope_tables(seq, scale):
    pos = jnp.arange(seq, dtype=F32)
    inv_freq = 1.0 / (ROPE_THETA ** (jnp.arange(0, QK_ROPE_DIM, 2, dtype=F32) / QK_ROPE_DIM))
    ang = pos[:, None] * inv_freq[None, :]
    cos, sin = jnp.cos(ang) * scale, jnp.sin(ang) * scale
    z32 = jnp.zeros_like(cos)
    z64 = jnp.zeros((seq, LANES - QK_ROPE_DIM), F32)
    return jnp.concatenate([cos, cos, z64, z32, sin, z64, -sin, z32, z64], axis=1)


def kernel(x, emb_ln_g, emb_ln_b, rel_bias, w_in, q_a_norm_g, w_q_b, kv_a_norm_g, w_kv_b,
           w_o_mla, w_o_dswa, w_out, ln_g, ln_b):
    assert DEPTH == 1 and x.shape == (1, SEQ, D_MODEL)
    x2 = x[0]
    win = w_in[0]

    off = IN_OFFSETS
    w_mla_a = jnp.pad(win[:, :off[1]], ((0, 0), (0, MLA_A_WIDTH - off[1]))).astype(BF16)
    w_groups = []
    for g in range(len(DIL_GROUPS)):
        lo = g * DSWA_WIDTH
        w_groups.append(jnp.concatenate(
            [win[:, off[1] + lo:off[1] + lo + DSWA_WIDTH],
             win[:, off[2] + lo:off[2] + lo + DSWA_WIDTH],
             win[:, off[3] + lo:off[3] + lo + DSWA_WIDTH]], axis=1).astype(BF16))
    w_gate = win[:, off[4]:].astype(BF16)
    w_q = jnp.pad(w_q_b[0].reshape(Q_LORA_RANK, MLA_HEADS, QK_HEAD_DIM),
                  ((0, 0), (0, 0), (0, QK_PAD_DIM - QK_HEAD_DIM))
                  ).reshape(Q_LORA_RANK, MLA_HEADS * QK_PAD_DIM).astype(BF16)
    w_kv = (w_kv_b[0].reshape(KV_LORA_RANK, MLA_HEADS, 2, QK_NOPE_DIM)
            .transpose(0, 2, 1, 3).reshape(KV_LORA_RANK, 2 * MLA_WIDTH).astype(BF16))
    w_om = w_o_mla[0].astype(BF16)
    w_od = w_o_dswa[0].astype(BF16)
    w_o = w_out[0].astype(BF16)

    q_scale = QK_HEAD_DIM ** -0.5 * math.log2(math.e)
    k_tab = _rope_tables(SEQ, 1.0)
    q_tab = _rope_tables(SEQ, q_scale)

    h = _input_layer_norm(x2, emb_ln_g, emb_ln_b)

    qn, ckvn, kpe = _mla_a_proj(h, w_mla_a, q_a_norm_g[0], kv_a_norm_g[0], k_tab)
    q = _mla_q_proj(qn, w_q, q_tab, q_scale)
    kv = _proj(ckvn, w_kv, tm=512, tn=1024, name="mla_kv_proj")
    a = _mla_attention(q, kv, kpe)

    gates = _proj(h, w_gate, gated=True, n_silu_blocks=(MLA_WIDTH + DSWA_WIDTH) // 1024,
                  tn=1024, name="gate_proj")

    bias = _band_bias(rel_bias, 128)
    outs, lses = [], []
    for g, (_, dil) in enumerate(DIL_GROUPS):
        qkv = _proj(h, w_groups[g], dil=dil, name=f"dswa_proj_d{dil}")
        o, lse = _band_attention(qkv, bias, g, dil)
        outs.append(o)
        lses.append(lse)
    b_gated = _group_mixture(outs, lses, gates)

    out = _output_projection(a, gates, b_gated, w_om, w_od, w_o)
    y = _final_layer_norm(x2, out, emb_ln_g, emb_ln_b, ln_g[0], ln_b[0])
    return y[None]
```

```python
import functools
import math

import jax
import jax.numpy as jnp
import numpy as np
from jax import lax
from jax.experimental import pallas as pl
from jax.experimental.pallas import tpu as pltpu

D_MODEL = 4096
SEQ = 8192
DEPTH = 1

MLA_HEADS = 16
Q_LORA_RANK = 1024
KV_LORA_RANK = 512
QK_NOPE_DIM = 128
QK_ROPE_DIM = 64
QK_HEAD_DIM = QK_NOPE_DIM + QK_ROPE_DIM
V_HEAD_DIM = 128
MLA_WIDTH = MLA_HEADS * V_HEAD_DIM
ROPE_THETA = 10000.0

DIL_GROUPS = ((128, 1), (512, 4), (2048, 16))
DILATIONS = tuple(d for _, d in DIL_GROUPS)
DSWA_HEADS_PER_GROUP = 8
DSWA_HEADS = 24
DSWA_HEAD_DIM = 128
DSWA_QKV_WIDTH = DSWA_HEADS * DSWA_HEAD_DIM
DSWA_WIDTH = DSWA_HEADS_PER_GROUP * DSWA_HEAD_DIM
BAND_HALF = 64

REL_BUCKETS = 32
REL_MAX_DISTANCE = 1024

DEEPNORM_ALPHA = (2.0 * DEPTH) ** 0.25
LN_EPS = 1e-5
RMS_EPS = 1e-6
NEG_INF = -1e30

IN_SPLITS = (Q_LORA_RANK, KV_LORA_RANK + QK_ROPE_DIM, DSWA_QKV_WIDTH, DSWA_QKV_WIDTH,
             DSWA_QKV_WIDTH, MLA_WIDTH, DSWA_WIDTH, D_MODEL, D_MODEL)
IN_OFFSETS = tuple(int(v) for v in np.cumsum(IN_SPLITS)[:-1])
MLA_A_COLS = IN_OFFSETS[1]
REST_COLS = sum(IN_SPLITS) - MLA_A_COLS

LANES = 128
QK_PAD_DIM = 256
MLA_A_WIDTH = Q_LORA_RANK + KV_LORA_RANK + LANES
COL_BLOCK = 1024
GATE_COL0 = 3 * DSWA_QKV_WIDTH // COL_BLOCK
GATE_WIDTH = MLA_WIDTH + DSWA_WIDTH + 2 * D_MODEL
VMEM_LIMIT = 56 * 1024 * 1024

BF16 = jnp.bfloat16
F32 = jnp.float32


def _params(sem):
    return pltpu.CompilerParams(dimension_semantics=sem, vmem_limit_bytes=VMEM_LIMIT)


def _dot(a, b):
    return jnp.dot(a, b, preferred_element_type=F32)


def _dot_nt(a, b):
    return lax.dot_general(a, b, (((1,), (1,)), ((), ())), preferred_element_type=F32)


def _sigmoid(x):
    return 1.0 / (1.0 + jnp.exp(-x))


def _layer_norm_rows(x, g, b):
    mu = jnp.mean(x, axis=-1, keepdims=True)
    xc = x - mu
    var = jnp.mean(xc * xc, axis=-1, keepdims=True)
    return xc * lax.rsqrt(var + LN_EPS) * g + b


def _rms_norm_rows(x, g):
    return x * lax.rsqrt(jnp.mean(x * x, axis=-1, keepdims=True) + RMS_EPS) * g


def _rope_lanes(t, tab):
    c, s1, s2 = tab[:, :LANES], tab[:, LANES:2 * LANES], tab[:, 2 * LANES:]
    return t * c + pltpu.roll(t, 32, 1) * s1 + pltpu.roll(t, LANES - 32, 1) * s2


def _ln_kernel(x_ref, g_ref, b_ref, h1_ref, h4_ref, h16_ref, y_ref):
    d = x_ref.shape[1]
    y = _layer_norm_rows(x_ref[...], g_ref[...], b_ref[...])
    h1_ref[...] = y.astype(BF16)
    for c in range(d // LANES):
        y_ref[c] = y[:, c * LANES:(c + 1) * LANES]
    for dil, slab_ref in ((DILATIONS[1], h4_ref), (DILATIONS[2], h16_ref)):
        rows = slab_ref.shape[0]
        for r in range(dil):
            for c in range(d // LANES):
                lo = r * d + c * LANES
                slab_ref[:, lo:lo + LANES] = y_ref[c, pl.ds(r, rows, stride=dil), :].astype(BF16)


def _input_layer_norm(x, g, b, tm=256):
    s, d = x.shape
    d4, d16 = DILATIONS[1], DILATIONS[2]
    return pl.pallas_call(
        _ln_kernel,
        out_shape=(jax.ShapeDtypeStruct((s, d), BF16),
                   jax.ShapeDtypeStruct((s // d4, d4 * d), BF16),
                   jax.ShapeDtypeStruct((s // d16, d16 * d), BF16)),
        grid=(s // tm,),
        in_specs=[pl.BlockSpec((tm, d), lambda i: (i, 0)),
                  pl.BlockSpec((1, d), lambda i: (0, 0)),
                  pl.BlockSpec((1, d), lambda i: (0, 0))],
        out_specs=(pl.BlockSpec((tm, d), lambda i: (i, 0)),
                   pl.BlockSpec((tm // d4, d4 * d), lambda i: (i, 0)),
                   pl.BlockSpec((tm // d16, d16 * d), lambda i: (i, 0))),
        scratch_shapes=[pltpu.VMEM((d // LANES, tm, LANES), F32)],
        compiler_params=_params(("parallel",)),
        name="input_ln",
    )(x, g.reshape(1, d), b.reshape(1, d))


def _proj_kernel(a_ref, w_ref, o_ref, *, n_silu_blocks, gated):
    acc = _dot(a_ref[...], w_ref[...])
    if gated:
        sig = _sigmoid(acc)
        acc = jnp.where(pl.program_id(0) < n_silu_blocks, acc * sig, sig)
    o_ref[...] = acc.astype(o_ref.dtype)


def _proj(a, w, *, n_out, w_block, dil=1, tm=512, tn=COL_BLOCK, gated=False, n_silu_blocks=0,
          out_dtype=BF16, name):
    rows, kd = a.shape
    k = kd // dil
    nj = n_out // tn
    kern = functools.partial(_proj_kernel, n_silu_blocks=n_silu_blocks, gated=gated)
    return pl.pallas_call(
        kern,
        out_shape=jax.ShapeDtypeStruct((rows, dil * n_out), out_dtype),
        grid=(nj, dil, rows // tm),
        in_specs=[pl.BlockSpec((tm, k), lambda j, r, i: (i, r)),
                  pl.BlockSpec((k, tn), lambda j, r, i: (0, w_block(j)))],
        out_specs=pl.BlockSpec((tm, tn), lambda j, r, i: (i, r * nj + j)),
        compiler_params=_params(("parallel", "parallel", "parallel")),
        name=name,
    )(a, w)


def _mla_a_kernel(h_ref, w_ref, gq_ref, gkv_ref, tab_ref, qn_ref, ckvn_ref, kpe_ref):
    acc = _dot(h_ref[...], w_ref[...])
    qn_ref[...] = _rms_norm_rows(acc[:, :Q_LORA_RANK], gq_ref[...]).astype(BF16)
    ckv = acc[:, Q_LORA_RANK:Q_LORA_RANK + KV_LORA_RANK]
    ckvn_ref[...] = _rms_norm_rows(ckv, gkv_ref[...]).astype(BF16)
    kpe_ref[...] = _rope_lanes(acc[:, Q_LORA_RANK + KV_LORA_RANK:], tab_ref[...]).astype(BF16)


def _mla_a_proj(h, w, gq, gkv, k_tab, tm=256):
    s, k = h.shape
    return pl.pallas_call(
        _mla_a_kernel,
        out_shape=(jax.ShapeDtypeStruct((s, Q_LORA_RANK), BF16),
                   jax.ShapeDtypeStruct((s, KV_LORA_RANK), BF16),
                   jax.ShapeDtypeStruct((s, LANES), BF16)),
        grid=(s // tm,),
        in_specs=[pl.BlockSpec((tm, k), lambda i: (i, 0)),
                  pl.BlockSpec((k, MLA_A_WIDTH), lambda i: (0, 0)),
                  pl.BlockSpec((1, Q_LORA_RANK), lambda i: (0, 0)),
                  pl.BlockSpec((1, KV_LORA_RANK), lambda i: (0, 0)),
                  pl.BlockSpec((tm, 3 * LANES), lambda i: (i, 0))],
        out_specs=(pl.BlockSpec((tm, Q_LORA_RANK), lambda i: (i, 0)),
                   pl.BlockSpec((tm, KV_LORA_RANK), lambda i: (i, 0)),
                   pl.BlockSpec((tm, LANES), lambda i: (i, 0))),
        compiler_params=_params(("parallel",)),
        name="mla_a_proj",
    )(h, w, gq.reshape(1, -1), gkv.reshape(1, -1), k_tab)


def _mla_q_kernel(qn_ref, w_ref, tab_ref, q_ref, *, nope_scale):
    qn = qn_ref[...]
    tab = tab_ref[...]
    for hd in range(MLA_HEADS):
        lo = hd * QK_PAD_DIM
        acc = _dot(qn, w_ref[:, lo:lo + QK_PAD_DIM])
        q_ref[:, lo:lo + LANES] = (acc[:, :LANES] * nope_scale).astype(BF16)
        q_ref[:, lo + LANES:lo + QK_PAD_DIM] = _rope_lanes(acc[:, LANES:], tab).astype(BF16)


def _mla_q_proj(qn, w, q_tab, nope_scale, tm=512):
    s, k = qn.shape
    n = w.shape[1]
    return pl.pallas_call(
        functools.partial(_mla_q_kernel, nope_scale=nope_scale),
        out_shape=jax.ShapeDtypeStruct((s, n), BF16),
        grid=(s // tm,),
        in_specs=[pl.BlockSpec((tm, k), lambda i: (i, 0)),
                  pl.BlockSpec((k, n), lambda i: (0, 0)),
                  pl.BlockSpec((tm, 3 * LANES), lambda i: (i, 0))],
        out_specs=pl.BlockSpec((tm, n), lambda i: (i, 0)),
        compiler_params=_params(("parallel",)),
        name="mla_q_proj",
    )(qn, w, q_tab)


def _mla_attn_kernel(q_ref, k_ref, v_ref, kpe_ref, sg_ref, o_ref, kcat_ref, vcat_ref, *, tk):
    @pl.when(pl.program_id(1) == 0)
    def _():
        kcat_ref[:, :LANES] = k_ref[...]
        kcat_ref[:, LANES:] = kpe_ref[...]
        vcat_ref[:, :V_HEAD_DIM] = v_ref[...]
        vcat_ref[:, V_HEAD_DIM:] = jnp.ones((v_ref.shape[0], LANES), BF16)

    q = q_ref[...]
    tq = q.shape[0]
    n_kv = k_ref.shape[0] // tk

    def body(j, carry):
        m, acc = carry
        start = pl.multiple_of(j * tk, tk)
        s = _dot_nt(q, kcat_ref[pl.ds(start, tk), :])
        m_new = jnp.maximum(m, jnp.max(s, axis=1, keepdims=True))
        alpha = jnp.exp2(m - m_new)
        p = jnp.exp2(s - m_new).astype(BF16)
        acc = alpha * acc + _dot(p, vcat_ref[pl.ds(start, tk), :])
        return m_new, acc

    m0 = jnp.full((tq, 1), -jnp.inf, F32)
    acc0 = jnp.zeros((tq, V_HEAD_DIM + LANES), F32)
    _, acc = lax.fori_loop(0, n_kv, body, (m0, acc0), unroll=True)
    attn = acc[:, :V_HEAD_DIM] / acc[:, V_HEAD_DIM:]
    o_ref[...] = (attn * sg_ref[...].astype(F32)).astype(o_ref.dtype)


def _mla_attention(q, kv, kpe, gates, tq=1024, tk=1024):
    s = q.shape[0]
    return pl.pallas_call(
        functools.partial(_mla_attn_kernel, tk=tk),
        out_shape=jax.ShapeDtypeStruct((s, MLA_WIDTH), BF16),
        grid=(MLA_HEADS, s // tq),
        in_specs=[pl.BlockSpec((tq, QK_PAD_DIM), lambda h, i: (i, h)),
                  pl.BlockSpec((s, QK_NOPE_DIM), lambda h, i: (0, h)),
                  pl.BlockSpec((s, V_HEAD_DIM), lambda h, i: (0, MLA_HEADS + h)),
                  pl.BlockSpec((s, LANES), lambda h, i: (0, 0)),
                  pl.BlockSpec((tq, V_HEAD_DIM), lambda h, i: (i, h))],
        out_specs=pl.BlockSpec((tq, V_HEAD_DIM), lambda h, i: (i, h)),
        scratch_shapes=[pltpu.VMEM((s, QK_PAD_DIM), BF16),
                        pltpu.VMEM((s, V_HEAD_DIM + LANES), BF16)],
        compiler_params=_params(("parallel", "arbitrary")),
        name="mla_attention",
    )(q, kv, kv, kpe, gates)


def _band_bucket_index(dil, tq):
    nb = REL_BUCKETS // 2
    max_exact = nb // 2
    rel = (np.arange(tq + 2 * BAND_HALF)[None, :] - BAND_HALF - np.arange(tq)[:, None]) * dil
    n = np.abs(rel)
    nf = np.maximum(n, 1).astype(np.float64)
    large = max_exact + (np.log(nf / max_exact) / math.log(REL_MAX_DISTANCE / max_exact)
                         * (nb - max_exact)).astype(np.int32)
    large = np.minimum(large, nb - 1)
    return (np.where(rel > 0, nb, 0) + np.where(n < max_exact, n, large)).astype(np.int32)


def _band_bias_kernel(rb_ref, idx_ref, o_ref):
    hd = pl.program_id(0)
    idx = idx_ref[0]
    acc = jnp.zeros(idx.shape, F32)
    for b in range(REL_BUCKETS):
        acc = jnp.where(idx == b, rb_ref[b, hd], acc)
    o_ref[0] = acc


def _band_bias(rel_bias, tq):
    idx = jnp.asarray(np.stack([_band_bucket_index(dil, tq) for dil in DILATIONS]))
    tw = idx.shape[2]
    return pl.pallas_call(
        _band_bias_kernel,
        out_shape=jax.ShapeDtypeStruct((DSWA_HEADS, tq, tw), F32),
        grid=(DSWA_HEADS,),
        in_specs=[pl.BlockSpec(memory_space=pltpu.SMEM),
                  pl.BlockSpec((1, tq, tw), lambda h: (h // DSWA_HEADS_PER_GROUP, 0, 0))],
        out_specs=pl.BlockSpec((1, tq, tw), lambda h: (h, 0, 0)),
        compiler_params=_params(("parallel",)),
        name="band_bias",
    )(rel_bias, idx)


def _band_attn_kernel(q_ref, kp_ref, ko_ref, kn_ref, vp_ref, vo_ref, vn_ref, bias_ref,
                      o_ref, lse_ref):
    tq = q_ref.shape[0]
    tw = tq + 2 * BAND_HALF
    tile = pl.program_id(1)
    prev_ok = tile != 0
    next_ok = tile != pl.num_programs(1) - 1
    row = lax.broadcasted_iota(jnp.int32, (tq, tw), 0)
    col = lax.broadcasted_iota(jnp.int32, (tq, tw), 1)
    rel = col - BAND_HALF - row
    mask = ((jnp.abs(rel) <= BAND_HALF) & ((col >= BAND_HALF) | prev_ok)
            & ((col < tq + BAND_HALF) | next_ok))
    lane = lax.broadcasted_iota(jnp.int32, (tq, LANES), 1)
    scale = DSWA_HEAD_DIM ** -0.5
    heads = range(DSWA_HEADS_PER_GROUP)
    cols = [slice(hd * DSWA_HEAD_DIM, (hd + 1) * DSWA_HEAD_DIM) for hd in heads]
    kc = [jnp.concatenate([kp_ref[:, c], ko_ref[:, c], kn_ref[:, c]], axis=0) for c in cols]
    s = [_dot_nt(q_ref[:, cols[hd]], kc[hd]) * scale + bias_ref[hd] for hd in heads]
    s = [jnp.where(mask, sh, NEG_INF) for sh in s]
    m = [jnp.max(sh, axis=1, keepdims=True) for sh in s]
    p = [jnp.exp(sh - mh) for sh, mh in zip(s, m)]
    den = [jnp.sum(ph, axis=1, keepdims=True) for ph in p]
    pn = [(ph / dh).astype(BF16) for ph, dh in zip(p, den)]
    vc = [jnp.concatenate([vp_ref[:, c], vo_ref[:, c], vn_ref[:, c]], axis=0) for c in cols]
    for hd in heads:
        o_ref[:, cols[hd]] = _dot(pn[hd], vc[hd]).astype(o_ref.dtype)
    lse_tile = jnp.zeros((tq, LANES), F32)
    for hd in heads:
        lse_tile = jnp.where(lane == hd, m[hd] + jnp.log(den[hd]), lse_tile)
    lse_ref[...] = lse_tile


def _band_attention(qkv, bias, group, dil, tq=128):
    rows = qkv.shape[0]
    nt = rows // tq
    per_tile = tq // BAND_HALF
    n_edge = rows // BAND_HALF
    tw = tq + 2 * BAND_HALF

    def own(part):
        return pl.BlockSpec((tq, DSWA_WIDTH), lambda r, n: (n, 3 * r + part))

    def prev(part):
        return pl.BlockSpec((BAND_HALF, DSWA_WIDTH),
                            lambda r, n: (jnp.maximum(n * per_tile - 1, 0), 3 * r + part))

    def nxt(part):
        return pl.BlockSpec((BAND_HALF, DSWA_WIDTH),
                            lambda r, n: (jnp.minimum((n + 1) * per_tile, n_edge - 1), 3 * r + part))

    return pl.pallas_call(
        _band_attn_kernel,
        out_shape=(jax.ShapeDtypeStruct((rows, dil * DSWA_WIDTH), BF16),
                   jax.ShapeDtypeStruct((rows, dil * LANES), F32)),
        grid=(dil, nt),
        in_specs=[own(0),
                  prev(1), own(1), nxt(1),
                  prev(2), own(2), nxt(2),
                  pl.BlockSpec((DSWA_HEADS_PER_GROUP, tq, tw), lambda r, n: (group, 0, 0))],
        out_specs=(pl.BlockSpec((tq, DSWA_WIDTH), lambda r, n: (n, r)),
                   pl.BlockSpec((tq, LANES), lambda r, n: (n, r))),
        compiler_params=_params(("parallel", "parallel")),
        name=f"band_attention_d{dil}",
    )(qkv, qkv, qkv, qkv, qkv, qkv, qkv, bias)


def _mix_kernel(o1_ref, o4_ref, o16_ref, l1_ref, l4_ref, l16_ref, sg_ref, out_ref,
                o4n_ref, l4n_ref, o16n_ref, l16n_ref):
    for dil, o_ref, l_ref, on_ref, ln_ref in ((DILATIONS[1], o4_ref, l4_ref, o4n_ref, l4n_ref),
                                               (DILATIONS[2], o16_ref, l16_ref, o16n_ref, l16n_ref)):
        rows = o_ref.shape[0]
        for r in range(dil):
            for hd in range(DSWA_HEADS_PER_GROUP):
                lo = r * DSWA_WIDTH + hd * DSWA_HEAD_DIM
                on_ref[hd, pl.ds(r, rows, stride=dil), :] = (
                    o_ref[:, lo:lo + DSWA_HEAD_DIM].astype(F32))
            ln_ref[pl.ds(r, rows, stride=dil), :] = l_ref[:, r * LANES:(r + 1) * LANES]
    for hd in range(DSWA_HEADS_PER_GROUP):
        cols = slice(hd * DSWA_HEAD_DIM, (hd + 1) * DSWA_HEAD_DIM)
        l1 = l1_ref[:, hd:hd + 1]
        l2 = l4n_ref[:, hd:hd + 1]
        l3 = l16n_ref[:, hd:hd + 1]
        mx = jnp.maximum(jnp.maximum(l1, l2), l3)
        e1, e2, e3 = jnp.exp(l1 - mx), jnp.exp(l2 - mx), jnp.exp(l3 - mx)
        den = e1 + e2 + e3
        mix = ((e1 / den) * o1_ref[:, cols].astype(F32)
               + (e2 / den) * o4n_ref[hd]
               + (e3 / den) * o16n_ref[hd])
        out_ref[:, cols] = (mix * sg_ref[:, cols].astype(F32)).astype(out_ref.dtype)


def _group_mixture(outs, lses, gates, tm=256):
    o1, o4, o16 = outs
    l1, l4, l16 = lses
    s = o1.shape[0]
    d4, d16 = DILATIONS[1], DILATIONS[2]
    w = DSWA_WIDTH
    return pl.pallas_call(
        _mix_kernel,
        out_shape=jax.ShapeDtypeStruct((s, w), BF16),
        grid=(s // tm,),
        in_specs=[pl.BlockSpec((tm, w), lambda i: (i, 0)),
                  pl.BlockSpec((tm // d4, d4 * w), lambda i: (i, 0)),
                  pl.BlockSpec((tm // d16, d16 * w), lambda i: (i, 0)),
                  pl.BlockSpec((tm, LANES), lambda i: (i, 0)),
                  pl.BlockSpec((tm // d4, d4 * LANES), lambda i: (i, 0)),
                  pl.BlockSpec((tm // d16, d16 * LANES), lambda i: (i, 0)),
                  pl.BlockSpec((tm, w), lambda i: (i, MLA_WIDTH // w))],
        out_specs=pl.BlockSpec((tm, w), lambda i: (i, 0)),
        scratch_shapes=[pltpu.VMEM((DSWA_HEADS_PER_GROUP, tm, DSWA_HEAD_DIM), F32),
                        pltpu.VMEM((tm, LANES), F32),
                        pltpu.VMEM((DSWA_HEADS_PER_GROUP, tm, DSWA_HEAD_DIM), F32),
                        pltpu.VMEM((tm, LANES), F32)],
        compiler_params=_params(("parallel",)),
        name="group_mixture",
    )(o1, o4, o16, l1, l4, l16, gates)


def _merge_kernel(a_ref, b_ref, wm_ref, wd_ref, rm_ref, rd_ref, o_ref):
    y_mla = _dot(a_ref[...], wm_ref[...])
    y_dswa = _dot(b_ref[...], wd_ref[...])
    merged = rm_ref[...].astype(F32) * y_mla + rd_ref[...].astype(F32) * y_dswa
    o_ref[...] = merged.astype(o_ref.dtype)


def _merge_branches(a_gated, b_gated, gates, w_o_mla, w_o_dswa, tm=512, tn=COL_BLOCK):
    s = a_gated.shape[0]
    d = w_o_mla.shape[1]
    r_mla_block = (MLA_WIDTH + DSWA_WIDTH) // tn
    r_dswa_block = (MLA_WIDTH + DSWA_WIDTH + D_MODEL) // tn
    return pl.pallas_call(
        _merge_kernel,
        out_shape=jax.ShapeDtypeStruct((s, d), BF16),
        grid=(d // tn, s // tm),
        in_specs=[pl.BlockSpec((tm, MLA_WIDTH), lambda j, i: (i, 0)),
                  pl.BlockSpec((tm, DSWA_WIDTH), lambda j, i: (i, 0)),
                  pl.BlockSpec((MLA_WIDTH, tn), lambda j, i: (0, j)),
                  pl.BlockSpec((DSWA_WIDTH, tn), lambda j, i: (0, j)),
                  pl.BlockSpec((tm, tn), lambda j, i: (i, r_mla_block + j)),
                  pl.BlockSpec((tm, tn), lambda j, i: (i, r_dswa_block + j))],
        out_specs=pl.BlockSpec((tm, tn), lambda j, i: (i, j)),
        compiler_params=_params(("parallel", "parallel")),
        name="merge_branches",
    )(a_gated, b_gated, w_o_mla, w_o_dswa, gates, gates)


def _final_ln_kernel(x_ref, o_ref, g0_ref, b0_ref, g_ref, b_ref, out_ref):
    h = _layer_norm_rows(x_ref[...], g0_ref[...], b0_ref[...])
    out_ref[...] = _layer_norm_rows(DEEPNORM_ALPHA * h + o_ref[...], g_ref[...], b_ref[...])


def _final_layer_norm(x, out, g0, b0, g, b, tm=256):
    s, d = x.shape
    row = pl.BlockSpec((tm, d), lambda i: (i, 0))
    vec = pl.BlockSpec((1, d), lambda i: (0, 0))
    return pl.pallas_call(
        _final_ln_kernel,
        out_shape=jax.ShapeDtypeStruct((s, d), F32),
        grid=(s // tm,),
        in_specs=[row, row, vec, vec, vec, vec],
        out_specs=row,
        compiler_params=_params(("parallel",)),
        name="final_ln",
    )(x, out, g0.reshape(1, d), b0.reshape(1, d), g.reshape(1, d), b.reshape(1, d))


def _rope_tables(seq, scale):
    pos = jnp.arange(seq, dtype=F32)
    inv_freq = 1.0 / (ROPE_THETA ** (jnp.arange(0, QK_ROPE_DIM, 2, dtype=F32) / QK_ROPE_DIM))
    ang = pos[:, None] * inv_freq[None, :]
    cos, sin = jnp.cos(ang) * scale, jnp.sin(ang) * scale
    z32 = jnp.zeros_like(cos)
    z64 = jnp.zeros((seq, LANES - QK_ROPE_DIM), F32)
    return jnp.concatenate([cos, cos, z64, z32, sin, z64, -sin, z32, z64], axis=1)


def kernel(x, emb_ln_g, emb_ln_b, rel_bias, w_in, q_a_norm_g, w_q_b, kv_a_norm_g, w_kv_b,
           w_o_mla, w_o_dswa, w_out, ln_g, ln_b):
    assert DEPTH == 1 and x.shape == (1, SEQ, D_MODEL)
    x2 = x[0]
    win = w_in[0]

    w_mla_a = jnp.pad(win[:, :MLA_A_COLS], ((0, 0), (0, MLA_A_WIDTH - MLA_A_COLS))).astype(BF16)
    w_rest = win[:, MLA_A_COLS:].astype(BF16)
    w_q = jnp.pad(w_q_b[0].reshape(Q_LORA_RANK, MLA_HEADS, QK_HEAD_DIM),
                  ((0, 0), (0, 0), (0, QK_PAD_DIM - QK_HEAD_DIM))
                  ).reshape(Q_LORA_RANK, MLA_HEADS * QK_PAD_DIM).astype(BF16)
    w_kv = (w_kv_b[0].reshape(KV_LORA_RANK, MLA_HEADS, 2, QK_NOPE_DIM)
            .transpose(0, 2, 1, 3).reshape(KV_LORA_RANK, 2 * MLA_WIDTH).astype(BF16))
    w_om = w_o_mla[0].astype(BF16)
    w_od = w_o_dswa[0].astype(BF16)
    w_o = w_out[0].astype(BF16)

    q_scale = QK_HEAD_DIM ** -0.5 * math.log2(math.e)
    k_tab = _rope_tables(SEQ, 1.0)
    q_tab = _rope_tables(SEQ, q_scale)

    h_slabs = _input_layer_norm(x2, emb_ln_g, emb_ln_b)
    h = h_slabs[0]

    gates = _proj(h, w_rest, n_out=GATE_WIDTH, w_block=lambda j: GATE_COL0 + j, gated=True,
                  n_silu_blocks=(MLA_WIDTH + DSWA_WIDTH) // COL_BLOCK, name="gate_proj")

    qn, ckvn, kpe = _mla_a_proj(h, w_mla_a, q_a_norm_g[0], kv_a_norm_g[0], k_tab)
    q = _mla_q_proj(qn, w_q, q_tab, q_scale)
    kv = _proj(ckvn, w_kv, n_out=2 * MLA_WIDTH, w_block=lambda j: j, name="mla_kv_proj")
    a_gated = _mla_attention(q, kv, kpe, gates)

    bias = _band_bias(rel_bias, 128)
    outs, lses = [], []
    blocks_per_part = DSWA_QKV_WIDTH // COL_BLOCK
    for g, dil in enumerate(DILATIONS):
        qkv = _proj(h_slabs[g], w_rest, n_out=3 * DSWA_WIDTH, dil=dil,
                    w_block=lambda j, g=g: j * blocks_per_part + g, name=f"dswa_proj_d{dil}")
        o, lse = _band_attention(qkv, bias, g, dil)
        outs.append(o)
        lses.append(lse)
    b_gated = _group_mixture(outs, lses, gates)

    merged = _merge_branches(a_gated, b_gated, gates, w_om, w_od)
    out = _proj(merged, w_o, n_out=D_MODEL, w_block=lambda j: j, out_dtype=F32, name="out_proj")
    y = _final_layer_norm(x2, out, emb_ln_g, emb_ln_b, ln_g[0], ln_b[0])
    return y[None]
```

```python
import functools
import math

import jax
import jax.numpy as jnp
import numpy as np
from jax import lax
from jax.experimental import pallas as pl
from jax.experimental.pallas import tpu as pltpu

D_MODEL = 4096
SEQ = 8192
DEPTH = 1

MLA_HEADS = 16
Q_LORA_RANK = 1024
KV_LORA_RANK = 512
QK_NOPE_DIM = 128
QK_ROPE_DIM = 64
QK_HEAD_DIM = QK_NOPE_DIM + QK_ROPE_DIM
V_HEAD_DIM = 128
MLA_WIDTH = MLA_HEADS * V_HEAD_DIM
ROPE_THETA = 10000.0

DIL_GROUPS = ((128, 1), (512, 4), (2048, 16))
DILATIONS = tuple(d for _, d in DIL_GROUPS)
DSWA_HEADS_PER_GROUP = 8
DSWA_HEADS = 24
DSWA_HEAD_DIM = 128
DSWA_QKV_WIDTH = DSWA_HEADS * DSWA_HEAD_DIM
DSWA_WIDTH = DSWA_HEADS_PER_GROUP * DSWA_HEAD_DIM
BAND_HALF = 64

REL_BUCKETS = 32
REL_MAX_DISTANCE = 1024

DEEPNORM_ALPHA = (2.0 * DEPTH) ** 0.25
LN_EPS = 1e-5
RMS_EPS = 1e-6
NEG_INF = -1e30

IN_SPLITS = (Q_LORA_RANK, KV_LORA_RANK + QK_ROPE_DIM, DSWA_QKV_WIDTH, DSWA_QKV_WIDTH,
             DSWA_QKV_WIDTH, MLA_WIDTH, DSWA_WIDTH, D_MODEL, D_MODEL)
IN_OFFSETS = tuple(int(v) for v in np.cumsum(IN_SPLITS)[:-1])
MLA_A_COLS = IN_OFFSETS[1]
REST_COLS = sum(IN_SPLITS) - MLA_A_COLS

LANES = 128
QK_PAD_DIM = 256
MLA_A_WIDTH = Q_LORA_RANK + KV_LORA_RANK + LANES
COL_BLOCK = 1024
GATE_COL0 = 3 * DSWA_QKV_WIDTH // COL_BLOCK
GATE_WIDTH = MLA_WIDTH + DSWA_WIDTH + 2 * D_MODEL
VMEM_LIMIT = 56 * 1024 * 1024

BF16 = jnp.bfloat16
F32 = jnp.float32


def _params(sem):
    return pltpu.CompilerParams(dimension_semantics=sem, vmem_limit_bytes=VMEM_LIMIT)


def _dot(a, b):
    return jnp.dot(a, b, preferred_element_type=F32)


def _dot_nt(a, b):
    return lax.dot_general(a, b, (((1,), (1,)), ((), ())), preferred_element_type=F32)


def _sigmoid(x):
    return 0.5 * jnp.tanh(0.5 * x) + 0.5


def _layer_norm_rows(x, g, b):
    mu = jnp.mean(x, axis=-1, keepdims=True)
    xc = x - mu
    var = jnp.mean(xc * xc, axis=-1, keepdims=True)
    return xc * lax.rsqrt(var + LN_EPS) * g + b


def _rms_norm_rows(x, g):
    return x * lax.rsqrt(jnp.mean(x * x, axis=-1, keepdims=True) + RMS_EPS) * g


def _rope_lanes(t, tab):
    c, s1, s2 = tab[:, :LANES], tab[:, LANES:2 * LANES], tab[:, 2 * LANES:]
    return t * c + pltpu.roll(t, 32, 1) * s1 + pltpu.roll(t, LANES - 32, 1) * s2


def _ln_kernel(x_ref, g_ref, b_ref, h1_ref, h4_ref, h16_ref, y_ref):
    d = x_ref.shape[1]
    y = _layer_norm_rows(x_ref[...], g_ref[...], b_ref[...])
    h1_ref[...] = y.astype(BF16)
    for c in range(d // LANES):
        y_ref[c] = y[:, c * LANES:(c + 1) * LANES]
    for dil, slab_ref in ((DILATIONS[1], h4_ref), (DILATIONS[2], h16_ref)):
        rows = slab_ref.shape[0]
        for r in range(dil):
            for c in range(d // LANES):
                lo = r * d + c * LANES
                slab_ref[:, lo:lo + LANES] = y_ref[c, pl.ds(r, rows, stride=dil), :].astype(BF16)


def _input_layer_norm(x, g, b, tm=256):
    s, d = x.shape
    d4, d16 = DILATIONS[1], DILATIONS[2]
    return pl.pallas_call(
        _ln_kernel,
        out_shape=(jax.ShapeDtypeStruct((s, d), BF16),
                   jax.ShapeDtypeStruct((s // d4, d4 * d), BF16),
                   jax.ShapeDtypeStruct((s // d16, d16 * d), BF16)),
        grid=(s // tm,),
        in_specs=[pl.BlockSpec((tm, d), lambda i: (i, 0)),
                  pl.BlockSpec((1, d), lambda i: (0, 0)),
                  pl.BlockSpec((1, d), lambda i: (0, 0))],
        out_specs=(pl.BlockSpec((tm, d), lambda i: (i, 0)),
                   pl.BlockSpec((tm // d4, d4 * d), lambda i: (i, 0)),
                   pl.BlockSpec((tm // d16, d16 * d), lambda i: (i, 0))),
        scratch_shapes=[pltpu.VMEM((d // LANES, tm, LANES), F32)],
        compiler_params=_params(("parallel",)),
        name="input_ln",
    )(x, g.reshape(1, d), b.reshape(1, d))


def _proj_kernel(a_ref, w_ref, o_ref, *, n_silu_blocks, gated):
    acc = _dot(a_ref[...], w_ref[...])
    if gated:
        sig = _sigmoid(acc)
        acc = jnp.where(pl.program_id(0) < n_silu_blocks, acc * sig, sig)
    o_ref[...] = acc.astype(o_ref.dtype)


def _proj(a, w, *, n_out, w_block, dil=1, tm=1024, tn=COL_BLOCK, gated=False, n_silu_blocks=0,
          out_dtype=BF16, name):
    rows, kd = a.shape
    k = kd // dil
    nj = n_out // tn
    tm = min(tm, rows)
    kern = functools.partial(_proj_kernel, n_silu_blocks=n_silu_blocks, gated=gated)
    return pl.pallas_call(
        kern,
        out_shape=jax.ShapeDtypeStruct((rows, dil * n_out), out_dtype),
        grid=(nj, dil, rows // tm),
        in_specs=[pl.BlockSpec((tm, k), lambda j, r, i: (i, r)),
                  pl.BlockSpec((k, tn), lambda j, r, i: (0, w_block(j)))],
        out_specs=pl.BlockSpec((tm, tn), lambda j, r, i: (i, r * nj + j)),
        compiler_params=_params(("parallel", "parallel", "parallel")),
        name=name,
    )(a, w)


def _mla_a_kernel(h_ref, w_ref, gq_ref, gkv_ref, tab_ref, qn_ref, ckvn_ref, kpe_ref):
    acc = _dot(h_ref[...], w_ref[...])
    qn_ref[...] = _rms_norm_rows(acc[:, :Q_LORA_RANK], gq_ref[...]).astype(BF16)
    ckv = acc[:, Q_LORA_RANK:Q_LORA_RANK + KV_LORA_RANK]
    ckvn_ref[...] = _rms_norm_rows(ckv, gkv_ref[...]).astype(BF16)
    kpe_ref[...] = _rope_lanes(acc[:, Q_LORA_RANK + KV_LORA_RANK:], tab_ref[...]).astype(BF16)


def _mla_a_proj(h, w, gq, gkv, k_tab, tm=256):
    s, k = h.shape
    return pl.pallas_call(
        _mla_a_kernel,
        out_shape=(jax.ShapeDtypeStruct((s, Q_LORA_RANK), BF16),
                   jax.ShapeDtypeStruct((s, KV_LORA_RANK), BF16),
                   jax.ShapeDtypeStruct((s, LANES), BF16)),
        grid=(s // tm,),
        in_specs=[pl.BlockSpec((tm, k), lambda i: (i, 0)),
                  pl.BlockSpec((k, MLA_A_WIDTH), lambda i: (0, 0)),
                  pl.BlockSpec((1, Q_LORA_RANK), lambda i: (0, 0)),
                  pl.BlockSpec((1, KV_LORA_RANK), lambda i: (0, 0)),
                  pl.BlockSpec((tm, 3 * LANES), lambda i: (i, 0))],
        out_specs=(pl.BlockSpec((tm, Q_LORA_RANK), lambda i: (i, 0)),
                   pl.BlockSpec((tm, KV_LORA_RANK), lambda i: (i, 0)),
                   pl.BlockSpec((tm, LANES), lambda i: (i, 0))),
        compiler_params=_params(("parallel",)),
        name="mla_a_proj",
    )(h, w, gq.reshape(1, -1), gkv.reshape(1, -1), k_tab)


def _mla_q_kernel(qn_ref, w_ref, tab_ref, q_ref, *, nope_scale):
    qn = qn_ref[...]
    tab = tab_ref[...]
    for hd in range(MLA_HEADS):
        lo = hd * QK_PAD_DIM
        acc = _dot(qn, w_ref[:, lo:lo + QK_PAD_DIM])
        q_ref[:, lo:lo + LANES] = (acc[:, :LANES] * nope_scale).astype(BF16)
        q_ref[:, lo + LANES:lo + QK_PAD_DIM] = _rope_lanes(acc[:, LANES:], tab).astype(BF16)


def _mla_q_proj(qn, w, q_tab, nope_scale, tm=512):
    s, k = qn.shape
    n = w.shape[1]
    return pl.pallas_call(
        functools.partial(_mla_q_kernel, nope_scale=nope_scale),
        out_shape=jax.ShapeDtypeStruct((s, n), BF16),
        grid=(s // tm,),
        in_specs=[pl.BlockSpec((tm, k), lambda i: (i, 0)),
                  pl.BlockSpec((k, n), lambda i: (0, 0)),
                  pl.BlockSpec((tm, 3 * LANES), lambda i: (i, 0))],
        out_specs=pl.BlockSpec((tm, n), lambda i: (i, 0)),
        compiler_params=_params(("parallel",)),
        name="mla_q_proj",
    )(qn, w, q_tab)


def _mla_attn_kernel(q_ref, k_ref, v_ref, kpe_ref, sg_ref, o_ref, kcat_ref, vcat_ref, *, tk):
    @pl.when(pl.program_id(1) == 0)
    def _():
        kcat_ref[:, :LANES] = k_ref[...]
        kcat_ref[:, LANES:] = kpe_ref[...]
        vcat_ref[:, :V_HEAD_DIM] = v_ref[...]
        vcat_ref[:, V_HEAD_DIM:] = jnp.ones((v_ref.shape[0], LANES), BF16)

    q = q_ref[...]
    tq = q.shape[0]
    n_kv = k_ref.shape[0] // tk

    def body(j, carry):
        m, acc = carry
        start = pl.multiple_of(j * tk, tk)
        s = _dot_nt(q, kcat_ref[pl.ds(start, tk), :])
        m_new = jnp.maximum(m, jnp.max(s, axis=1, keepdims=True))
        alpha = jnp.exp2(m - m_new)
        p = jnp.exp2(s - m_new).astype(BF16)
        acc = alpha * acc + _dot(p, vcat_ref[pl.ds(start, tk), :])
        return m_new, acc

    m0 = jnp.full((tq, 1), -jnp.inf, F32)
    acc0 = jnp.zeros((tq, V_HEAD_DIM + LANES), F32)
    _, acc = lax.fori_loop(0, n_kv, body, (m0, acc0), unroll=True)
    attn = acc[:, :V_HEAD_DIM] / acc[:, V_HEAD_DIM:]
    o_ref[...] = (attn * sg_ref[...].astype(F32)).astype(o_ref.dtype)


def _mla_attention(q, kv, kpe, gates, tq=1024, tk=1024):
    s = q.shape[0]
    return pl.pallas_call(
        functools.partial(_mla_attn_kernel, tk=tk),
        out_shape=jax.ShapeDtypeStruct((s, MLA_WIDTH), BF16),
        grid=(MLA_HEADS, s // tq),
        in_specs=[pl.BlockSpec((tq, QK_PAD_DIM), lambda h, i: (i, h)),
                  pl.BlockSpec((s, QK_NOPE_DIM), lambda h, i: (0, h)),
                  pl.BlockSpec((s, V_HEAD_DIM), lambda h, i: (0, MLA_HEADS + h)),
                  pl.BlockSpec((s, LANES), lambda h, i: (0, 0)),
                  pl.BlockSpec((tq, V_HEAD_DIM), lambda h, i: (i, h))],
        out_specs=pl.BlockSpec((tq, V_HEAD_DIM), lambda h, i: (i, h)),
        scratch_shapes=[pltpu.VMEM((s, QK_PAD_DIM), BF16),
                        pltpu.VMEM((s, V_HEAD_DIM + LANES), BF16)],
        compiler_params=_params(("parallel", "arbitrary")),
        name="mla_attention",
    )(q, kv, kv, kpe, gates)


def _band_bucket_index(dil, tq):
    nb = REL_BUCKETS // 2
    max_exact = nb // 2
    rel = (np.arange(tq + 2 * BAND_HALF)[None, :] - BAND_HALF - np.arange(tq)[:, None]) * dil
    n = np.abs(rel)
    nf = np.maximum(n, 1).astype(np.float64)
    large = max_exact + (np.log(nf / max_exact) / math.log(REL_MAX_DISTANCE / max_exact)
                         * (nb - max_exact)).astype(np.int32)
    large = np.minimum(large, nb - 1)
    return (np.where(rel > 0, nb, 0) + np.where(n < max_exact, n, large)).astype(np.int32)


def _band_bias_kernel(rb_ref, idx_ref, o_ref):
    hd = pl.program_id(0)
    idx = idx_ref[0]
    acc = jnp.zeros(idx.shape, F32)
    for b in range(REL_BUCKETS):
        acc = jnp.where(idx == b, rb_ref[b, hd], acc)
    o_ref[0] = acc


def _band_bias(rel_bias, tq):
    idx = jnp.asarray(np.stack([_band_bucket_index(dil, tq) for dil in DILATIONS]))
    tw = idx.shape[2]
    return pl.pallas_call(
        _band_bias_kernel,
        out_shape=jax.ShapeDtypeStruct((DSWA_HEADS, tq, tw), F32),
        grid=(DSWA_HEADS,),
        in_specs=[pl.BlockSpec(memory_space=pltpu.SMEM),
                  pl.BlockSpec((1, tq, tw), lambda h: (h // DSWA_HEADS_PER_GROUP, 0, 0))],
        out_specs=pl.BlockSpec((1, tq, tw), lambda h: (h, 0, 0)),
        compiler_params=_params(("parallel",)),
        name="band_bias",
    )(rel_bias, idx)


def _band_attn_kernel(q_ref, kp_ref, ko_ref, kn_ref, vp_ref, vo_ref, vn_ref, bias_ref,
                      o_ref, lse_ref):
    tq = q_ref.shape[0]
    tw = tq + 2 * BAND_HALF
    tile = pl.program_id(1)
    prev_ok = tile != 0
    next_ok = tile != pl.num_programs(1) - 1
    row = lax.broadcasted_iota(jnp.int32, (tq, tw), 0)
    col = lax.broadcasted_iota(jnp.int32, (tq, tw), 1)
    rel = col - BAND_HALF - row
    mask = ((jnp.abs(rel) <= BAND_HALF) & ((col >= BAND_HALF) | prev_ok)
            & ((col < tq + BAND_HALF) | next_ok))
    lane = lax.broadcasted_iota(jnp.int32, (tq, LANES), 1)
    scale = DSWA_HEAD_DIM ** -0.5
    heads = range(DSWA_HEADS_PER_GROUP)
    cols = [slice(hd * DSWA_HEAD_DIM, (hd + 1) * DSWA_HEAD_DIM) for hd in heads]
    kc = [jnp.concatenate([kp_ref[:, c], ko_ref[:, c], kn_ref[:, c]], axis=0) for c in cols]
    s = [_dot_nt(q_ref[:, cols[hd]], kc[hd]) * scale + bias_ref[hd] for hd in heads]
    s = [jnp.where(mask, sh, NEG_INF) for sh in s]
    m = [jnp.max(sh, axis=1, keepdims=True) for sh in s]
    p = [jnp.exp(sh - mh) for sh, mh in zip(s, m)]
    den = [jnp.sum(ph, axis=1, keepdims=True) for ph in p]
    pn = [(ph / dh).astype(BF16) for ph, dh in zip(p, den)]
    vc = [jnp.concatenate([vp_ref[:, c], vo_ref[:, c], vn_ref[:, c]], axis=0) for c in cols]
    for hd in heads:
        o_ref[:, cols[hd]] = _dot(pn[hd], vc[hd]).astype(o_ref.dtype)
    lse_tile = jnp.zeros((tq, LANES), F32)
    for hd in heads:
        lse_tile = jnp.where(lane == hd, m[hd] + jnp.log(den[hd]), lse_tile)
    lse_ref[...] = lse_tile


def _band_attention(qkv, bias, group, dil, tq=128):
    rows = qkv.shape[0]
    nt = rows // tq
    per_tile = tq // BAND_HALF
    n_edge = rows // BAND_HALF
    tw = tq + 2 * BAND_HALF

    def own(part):
        return pl.BlockSpec((tq, DSWA_WIDTH), lambda r, n: (n, 3 * r + part))

    def prev(part):
        return pl.BlockSpec((BAND_HALF, DSWA_WIDTH),
                            lambda r, n: (jnp.maximum(n * per_tile - 1, 0), 3 * r + part))

    def nxt(part):
        return pl.BlockSpec((BAND_HALF, DSWA_WIDTH),
                            lambda r, n: (jnp.minimum((n + 1) * per_tile, n_edge - 1), 3 * r + part))

    return pl.pallas_call(
        _band_attn_kernel,
        out_shape=(jax.ShapeDtypeStruct((rows, dil * DSWA_WIDTH), BF16),
                   jax.ShapeDtypeStruct((rows, dil * LANES), F32)),
        grid=(dil, nt),
        in_specs=[own(0),
                  prev(1), own(1), nxt(1),
                  prev(2), own(2), nxt(2),
                  pl.BlockSpec((DSWA_HEADS_PER_GROUP, tq, tw), lambda r, n: (group, 0, 0))],
        out_specs=(pl.BlockSpec((tq, DSWA_WIDTH), lambda r, n: (n, r)),
                   pl.BlockSpec((tq, LANES), lambda r, n: (n, r))),
        compiler_params=_params(("parallel", "parallel")),
        name=f"band_attention_d{dil}",
    )(qkv, qkv, qkv, qkv, qkv, qkv, qkv, bias)


def _mix_kernel(o1_ref, o4_ref, o16_ref, l1_ref, l4_ref, l16_ref, sg_ref, out_ref,
                o4n_ref, l4n_ref, o16n_ref, l16n_ref):
    for dil, o_ref, l_ref, on_ref, ln_ref in ((DILATIONS[1], o4_ref, l4_ref, o4n_ref, l4n_ref),
                                               (DILATIONS[2], o16_ref, l16_ref, o16n_ref, l16n_ref)):
        rows = o_ref.shape[0]
        for r in range(dil):
            for hd in range(DSWA_HEADS_PER_GROUP):
                lo = r * DSWA_WIDTH + hd * DSWA_HEAD_DIM
                on_ref[hd, pl.ds(r, rows, stride=dil), :] = (
                    o_ref[:, lo:lo + DSWA_HEAD_DIM].astype(F32))
            ln_ref[pl.ds(r, rows, stride=dil), :] = l_ref[:, r * LANES:(r + 1) * LANES]
    for hd in range(DSWA_HEADS_PER_GROUP):
        cols = slice(hd * DSWA_HEAD_DIM, (hd + 1) * DSWA_HEAD_DIM)
        l1 = l1_ref[:, hd:hd + 1]
        l2 = l4n_ref[:, hd:hd + 1]
        l3 = l16n_ref[:, hd:hd + 1]
        mx = jnp.maximum(jnp.maximum(l1, l2), l3)
        e1, e2, e3 = jnp.exp(l1 - mx), jnp.exp(l2 - mx), jnp.exp(l3 - mx)
        den = e1 + e2 + e3
        mix = ((e1 / den) * o1_ref[:, cols].astype(F32)
               + (e2 / den) * o4n_ref[hd]
               + (e3 / den) * o16n_ref[hd])
        out_ref[:, cols] = (mix * sg_ref[:, cols].astype(F32)).astype(out_ref.dtype)


def _group_mixture(outs, lses, gates, tm=256):
    o1, o4, o16 = outs
    l1, l4, l16 = lses
    s = o1.shape[0]
    d4, d16 = DILATIONS[1], DILATIONS[2]
    w = DSWA_WIDTH
    return pl.pallas_call(
        _mix_kernel,
        out_shape=jax.ShapeDtypeStruct((s, w), BF16),
        grid=(s // tm,),
        in_specs=[pl.BlockSpec((tm, w), lambda i: (i, 0)),
                  pl.BlockSpec((tm // d4, d4 * w), lambda i: (i, 0)),
                  pl.BlockSpec((tm // d16, d16 * w), lambda i: (i, 0)),
                  pl.BlockSpec((tm, LANES), lambda i: (i, 0)),
                  pl.BlockSpec((tm // d4, d4 * LANES), lambda i: (i, 0)),
                  pl.BlockSpec((tm // d16, d16 * LANES), lambda i: (i, 0)),
                  pl.BlockSpec((tm, w), lambda i: (i, MLA_WIDTH // w))],
        out_specs=pl.BlockSpec((tm, w), lambda i: (i, 0)),
        scratch_shapes=[pltpu.VMEM((DSWA_HEADS_PER_GROUP, tm, DSWA_HEAD_DIM), F32),
                        pltpu.VMEM((tm, LANES), F32),
                        pltpu.VMEM((DSWA_HEADS_PER_GROUP, tm, DSWA_HEAD_DIM), F32),
                        pltpu.VMEM((tm, LANES), F32)],
        compiler_params=_params(("parallel",)),
        name="group_mixture",
    )(o1, o4, o16, l1, l4, l16, gates)


def _merge_kernel(a_ref, b_ref, wm_ref, wd_ref, rm_ref, rd_ref, o_ref):
    y_mla = _dot(a_ref[...], wm_ref[...])
    y_dswa = _dot(b_ref[...], wd_ref[...])
    merged = rm_ref[...].astype(F32) * y_mla + rd_ref[...].astype(F32) * y_dswa
    o_ref[...] = merged.astype(o_ref.dtype)


def _merge_branches(a_gated, b_gated, gates, w_o_mla, w_o_dswa, tm=512, tn=COL_BLOCK):
    s = a_gated.shape[0]
    d = w_o_mla.shape[1]
    r_mla_block = (MLA_WIDTH + DSWA_WIDTH) // tn
    r_dswa_block = (MLA_WIDTH + DSWA_WIDTH + D_MODEL) // tn
    return pl.pallas_call(
        _merge_kernel,
        out_shape=jax.ShapeDtypeStruct((s, d), BF16),
        grid=(d // tn, s // tm),
        in_specs=[pl.BlockSpec((tm, MLA_WIDTH), lambda j, i: (i, 0)),
                  pl.BlockSpec((tm, DSWA_WIDTH), lambda j, i: (i, 0)),
                  pl.BlockSpec((MLA_WIDTH, tn), lambda j, i: (0, j)),
                  pl.BlockSpec((DSWA_WIDTH, tn), lambda j, i: (0, j)),
                  pl.BlockSpec((tm, tn), lambda j, i: (i, r_mla_block + j)),
                  pl.BlockSpec((tm, tn), lambda j, i: (i, r_dswa_block + j))],
        out_specs=pl.BlockSpec((tm, tn), lambda j, i: (i, j)),
        compiler_params=_params(("parallel", "parallel")),
        name="merge_branches",
    )(a_gated, b_gated, w_o_mla, w_o_dswa, gates, gates)


def _final_ln_kernel(x_ref, o_ref, g0_ref, b0_ref, g_ref, b_ref, out_ref):
    h = _layer_norm_rows(x_ref[...], g0_ref[...], b0_ref[...])
    out_ref[...] = _layer_norm_rows(DEEPNORM_ALPHA * h + o_ref[...], g_ref[...], b_ref[...])


def _final_layer_norm(x, out, g0, b0, g, b, tm=256):
    s, d = x.shape
    row = pl.BlockSpec((tm, d), lambda i: (i, 0))
    vec = pl.BlockSpec((1, d), lambda i: (0, 0))
    return pl.pallas_call(
        _final_ln_kernel,
        out_shape=jax.ShapeDtypeStruct((s, d), F32),
        grid=(s // tm,),
        in_specs=[row, row, vec, vec, vec, vec],
        out_specs=row,
        compiler_params=_params(("parallel",)),
        name="final_ln",
    )(x, out, g0.reshape(1, d), b0.reshape(1, d), g.reshape(1, d), b.reshape(1, d))


def _shift_cast_kernel(a_ref, b_ref, o_ref, *, shift):
    tn = o_ref.shape[1]
    x = jnp.concatenate([a_ref[...], b_ref[...]], axis=1)
    o_ref[...] = x[:, shift:shift + tn].astype(o_ref.dtype)


def _rest_weights_bf16(w_in, tr=2048, tn=512):
    _, k, n_in = w_in.shape
    shift = MLA_A_COLS % LANES
    base = MLA_A_COLS - shift
    assert base % tn == 0 and REST_COLS % tn == 0
    return pl.pallas_call(
        functools.partial(_shift_cast_kernel, shift=shift),
        out_shape=jax.ShapeDtypeStruct((k, REST_COLS), BF16),
        grid=(k // tr, REST_COLS // tn),
        in_specs=[pl.BlockSpec((None, tr, tn), lambda i, j: (0, i, base // tn + j)),
                  pl.BlockSpec((None, tr, LANES),
                               lambda i, j: (0, i, (base + (j + 1) * tn) // LANES))],
        out_specs=pl.BlockSpec((tr, tn), lambda i, j: (i, j)),
        compiler_params=_params(("parallel", "parallel")),
        name="rest_weights_bf16",
    )(w_in, w_in)


def _rope_tables(seq, scale):
    pos = jnp.arange(seq, dtype=F32)
    inv_freq = 1.0 / (ROPE_THETA ** (jnp.arange(0, QK_ROPE_DIM, 2, dtype=F32) / QK_ROPE_DIM))
    ang = pos[:, None] * inv_freq[None, :]
    cos, sin = jnp.cos(ang) * scale, jnp.sin(ang) * scale
    z32 = jnp.zeros_like(cos)
    z64 = jnp.zeros((seq, LANES - QK_ROPE_DIM), F32)
    return jnp.concatenate([cos, cos, z64, z32, sin, z64, -sin, z32, z64], axis=1)


def kernel(x, emb_ln_g, emb_ln_b, rel_bias, w_in, q_a_norm_g, w_q_b, kv_a_norm_g, w_kv_b,
           w_o_mla, w_o_dswa, w_out, ln_g, ln_b):
    assert DEPTH == 1 and x.shape == (1, SEQ, D_MODEL)
    x2 = x[0]
    win = w_in[0]

    w_mla_a = jnp.pad(win[:, :MLA_A_COLS], ((0, 0), (0, MLA_A_WIDTH - MLA_A_COLS))).astype(BF16)
    w_rest = _rest_weights_bf16(w_in)
    w_q = jnp.pad(w_q_b[0].reshape(Q_LORA_RANK, MLA_HEADS, QK_HEAD_DIM),
                  ((0, 0), (0, 0), (0, QK_PAD_DIM - QK_HEAD_DIM))
                  ).reshape(Q_LORA_RANK, MLA_HEADS * QK_PAD_DIM).astype(BF16)
    w_kv = (w_kv_b[0].reshape(KV_LORA_RANK, MLA_HEADS, 2, QK_NOPE_DIM)
            .transpose(0, 2, 1, 3).reshape(KV_LORA_RANK, 2 * MLA_WIDTH).astype(BF16))
    w_om = w_o_mla[0].astype(BF16)
    w_od = w_o_dswa[0].astype(BF16)
    w_o = w_out[0].astype(BF16)

    q_scale = QK_HEAD_DIM ** -0.5 * math.log2(math.e)
    k_tab = _rope_tables(SEQ, 1.0)
    q_tab = _rope_tables(SEQ, q_scale)

    h_slabs = _input_layer_norm(x2, emb_ln_g, emb_ln_b)
    h = h_slabs[0]

    gates = _proj(h, w_rest, n_out=GATE_WIDTH, w_block=lambda j: GATE_COL0 + j, gated=True,
                  n_silu_blocks=(MLA_WIDTH + DSWA_WIDTH) // COL_BLOCK, name="gate_proj")

    qn, ckvn, kpe = _mla_a_proj(h, w_mla_a, q_a_norm_g[0], kv_a_norm_g[0], k_tab)
    q = _mla_q_proj(qn, w_q, q_tab, q_scale)
    kv = _proj(ckvn, w_kv, n_out=2 * MLA_WIDTH, w_block=lambda j: j, name="mla_kv_proj")
    a_gated = _mla_attention(q, kv, kpe, gates)

    bias = _band_bias(rel_bias, 128)
    outs, lses = [], []
    blocks_per_part = DSWA_QKV_WIDTH // COL_BLOCK
    for g, dil in enumerate(DILATIONS):
        qkv = _proj(h_slabs[g], w_rest, n_out=3 * DSWA_WIDTH, dil=dil,
                    w_block=lambda j, g=g: j * blocks_per_part + g, name=f"dswa_proj_d{dil}")
        o, lse = _band_attention(qkv, bias, g, dil)
        outs.append(o)
        lses.append(lse)
    b_gated = _group_mixture(outs, lses, gates)

    merged = _merge_branches(a_gated, b_gated, gates, w_om, w_od)
    out = _proj(merged, w_o, n_out=D_MODEL, w_block=lambda j: j, out_dtype=F32, name="out_proj")
    y = _final_layer_norm(x2, out, emb_ln_g, emb_ln_b, ln_g[0], ln_b[0])
    return y[None]
```

```python
import functools
import math

import jax
import jax.numpy as jnp
import numpy as np
from jax import lax
from jax.experimental import pallas as pl
from jax.experimental.pallas import tpu as pltpu

D_MODEL = 4096
SEQ = 8192
DEPTH = 1

MLA_HEADS = 16
Q_LORA_RANK = 1024
KV_LORA_RANK = 512
QK_NOPE_DIM = 128
QK_ROPE_DIM = 64
QK_HEAD_DIM = QK_NOPE_DIM + QK_ROPE_DIM
V_HEAD_DIM = 128
MLA_WIDTH = MLA_HEADS * V_HEAD_DIM
ROPE_THETA = 10000.0

DIL_GROUPS = ((128, 1), (512, 4), (2048, 16))
DILATIONS = tuple(d for _, d in DIL_GROUPS)
DSWA_HEADS_PER_GROUP = 8
DSWA_HEADS = 24
DSWA_HEAD_DIM = 128
DSWA_QKV_WIDTH = DSWA_HEADS * DSWA_HEAD_DIM
DSWA_WIDTH = DSWA_HEADS_PER_GROUP * DSWA_HEAD_DIM
BAND_HALF = 64

REL_BUCKETS = 32
REL_MAX_DISTANCE = 1024

DEEPNORM_ALPHA = (2.0 * DEPTH) ** 0.25
LN_EPS = 1e-5
RMS_EPS = 1e-6
NEG_INF = -1e30

IN_SPLITS = (Q_LORA_RANK, KV_LORA_RANK + QK_ROPE_DIM, DSWA_QKV_WIDTH, DSWA_QKV_WIDTH,
             DSWA_QKV_WIDTH, MLA_WIDTH, DSWA_WIDTH, D_MODEL, D_MODEL)
IN_OFFSETS = tuple(int(v) for v in np.cumsum(IN_SPLITS)[:-1])
MLA_A_COLS = IN_OFFSETS[1]
REST_COLS = sum(IN_SPLITS) - MLA_A_COLS

LANES = 128
QK_PAD_DIM = 256
COL_BLOCK = 1024
GATE_COL0 = 3 * DSWA_QKV_WIDTH // COL_BLOCK
GATE_WIDTH = MLA_WIDTH + DSWA_WIDTH + 2 * D_MODEL
VMEM_LIMIT = 56 * 1024 * 1024

BF16 = jnp.bfloat16
F32 = jnp.float32


def _params(sem):
    return pltpu.CompilerParams(dimension_semantics=sem, vmem_limit_bytes=VMEM_LIMIT)


def _dot(a, b):
    return jnp.dot(a, b, preferred_element_type=F32)


def _dot_nt(a, b):
    return lax.dot_general(a, b, (((1,), (1,)), ((), ())), preferred_element_type=F32)


def _sigmoid(x):
    return 0.5 * jnp.tanh(0.5 * x) + 0.5


def _layer_norm_rows(x, g, b):
    mu = jnp.mean(x, axis=-1, keepdims=True)
    xc = x - mu
    var = jnp.mean(xc * xc, axis=-1, keepdims=True)
    return xc * lax.rsqrt(var + LN_EPS) * g + b


def _rms_norm_rows(x, g):
    return x * lax.rsqrt(jnp.mean(x * x, axis=-1, keepdims=True) + RMS_EPS) * g


def _rope_lanes(t, tab):
    c, s1, s2 = tab[:, :LANES], tab[:, LANES:2 * LANES], tab[:, 2 * LANES:]
    return t * c + pltpu.roll(t, 32, 1) * s1 + pltpu.roll(t, LANES - 32, 1) * s2


def _ln_kernel(x_ref, g_ref, b_ref, h1_ref, h4_ref, h16_ref, y_ref):
    d = x_ref.shape[1]
    y = _layer_norm_rows(x_ref[...], g_ref[...], b_ref[...])
    h1_ref[...] = y.astype(BF16)
    for c in range(d // LANES):
        y_ref[c] = y[:, c * LANES:(c + 1) * LANES]
    for dil, slab_ref in ((DILATIONS[1], h4_ref), (DILATIONS[2], h16_ref)):
        rows = slab_ref.shape[0]
        for r in range(dil):
            for c in range(d // LANES):
                lo = r * d + c * LANES
                slab_ref[:, lo:lo + LANES] = y_ref[c, pl.ds(r, rows, stride=dil), :].astype(BF16)


def _input_layer_norm(x, g, b, tm=256):
    s, d = x.shape
    d4, d16 = DILATIONS[1], DILATIONS[2]
    return pl.pallas_call(
        _ln_kernel,
        out_shape=(jax.ShapeDtypeStruct((s, d), BF16),
                   jax.ShapeDtypeStruct((s // d4, d4 * d), BF16),
                   jax.ShapeDtypeStruct((s // d16, d16 * d), BF16)),
        grid=(s // tm,),
        in_specs=[pl.BlockSpec((tm, d), lambda i: (i, 0)),
                  pl.BlockSpec((1, d), lambda i: (0, 0)),
                  pl.BlockSpec((1, d), lambda i: (0, 0))],
        out_specs=(pl.BlockSpec((tm, d), lambda i: (i, 0)),
                   pl.BlockSpec((tm // d4, d4 * d), lambda i: (i, 0)),
                   pl.BlockSpec((tm // d16, d16 * d), lambda i: (i, 0))),
        scratch_shapes=[pltpu.VMEM((d // LANES, tm, LANES), F32)],
        compiler_params=_params(("parallel",)),
        name="input_ln",
    )(x, g.reshape(1, d), b.reshape(1, d))


def _proj_kernel(a_ref, w_ref, o_ref, *, n_silu_blocks, gated, w_transposed):
    acc = (_dot_nt if w_transposed else _dot)(a_ref[...], w_ref[...])
    if gated:
        sig = _sigmoid(acc)
        acc = jnp.where(pl.program_id(0) < n_silu_blocks, acc * sig, sig)
    o_ref[...] = acc.astype(o_ref.dtype)


def _proj(a, w, *, n_out, w_block, dil=1, tm=1024, tn=COL_BLOCK, gated=False, n_silu_blocks=0,
          w_transposed=False, out_dtype=BF16, name):
    rows, kd = a.shape
    k = kd // dil
    nj = n_out // tn
    tm = min(tm, rows)
    kern = functools.partial(_proj_kernel, n_silu_blocks=n_silu_blocks, gated=gated,
                             w_transposed=w_transposed)
    if w_transposed:
        w_spec = pl.BlockSpec((tn, k), lambda j, r, i: (w_block(j), 0))
    else:
        w_spec = pl.BlockSpec((k, tn), lambda j, r, i: (0, w_block(j)))
    return pl.pallas_call(
        kern,
        out_shape=jax.ShapeDtypeStruct((rows, dil * n_out), out_dtype),
        grid=(nj, dil, rows // tm),
        in_specs=[pl.BlockSpec((tm, k), lambda j, r, i: (i, r)), w_spec],
        out_specs=pl.BlockSpec((tm, tn), lambda j, r, i: (i, r * nj + j)),
        compiler_params=_params(("parallel", "parallel", "parallel")),
        name=name,
    )(a, w)


def _mla_a_kernel(h_ref, w_ref, gq_ref, gkv_ref, tab_ref, qn_ref, ckvn_ref, kpe_ref):
    acc = _dot_nt(h_ref[...], w_ref[...])
    qn_ref[...] = _rms_norm_rows(acc[:, :Q_LORA_RANK], gq_ref[...]).astype(BF16)
    ckv = acc[:, Q_LORA_RANK:Q_LORA_RANK + KV_LORA_RANK]
    ckvn_ref[...] = _rms_norm_rows(ckv, gkv_ref[...]).astype(BF16)
    k_pe = acc[:, Q_LORA_RANK + KV_LORA_RANK:]
    k_pe = jnp.concatenate([k_pe, jnp.zeros((k_pe.shape[0], LANES - QK_ROPE_DIM), F32)], axis=1)
    kpe_ref[...] = _rope_lanes(k_pe, tab_ref[...]).astype(BF16)


def _mla_a_proj(h, w, gq, gkv, k_tab, tm=256):
    s, k = h.shape
    return pl.pallas_call(
        _mla_a_kernel,
        out_shape=(jax.ShapeDtypeStruct((s, Q_LORA_RANK), BF16),
                   jax.ShapeDtypeStruct((s, KV_LORA_RANK), BF16),
                   jax.ShapeDtypeStruct((s, LANES), BF16)),
        grid=(s // tm,),
        in_specs=[pl.BlockSpec((tm, k), lambda i: (i, 0)),
                  pl.BlockSpec((MLA_A_COLS, k), lambda i: (0, 0)),
                  pl.BlockSpec((1, Q_LORA_RANK), lambda i: (0, 0)),
                  pl.BlockSpec((1, KV_LORA_RANK), lambda i: (0, 0)),
                  pl.BlockSpec((tm, 3 * LANES), lambda i: (i, 0))],
        out_specs=(pl.BlockSpec((tm, Q_LORA_RANK), lambda i: (i, 0)),
                   pl.BlockSpec((tm, KV_LORA_RANK), lambda i: (i, 0)),
                   pl.BlockSpec((tm, LANES), lambda i: (i, 0))),
        compiler_params=_params(("parallel",)),
        name="mla_a_proj",
    )(h, w, gq.reshape(1, -1), gkv.reshape(1, -1), k_tab)


def _mla_q_kernel(qn_ref, w_ref, tab_ref, q_ref, *, nope_scale):
    qn = qn_ref[...]
    tab = tab_ref[...]
    for hd in range(MLA_HEADS):
        lo = hd * QK_PAD_DIM
        acc = _dot(qn, w_ref[:, lo:lo + QK_PAD_DIM])
        q_ref[:, lo:lo + LANES] = (acc[:, :LANES] * nope_scale).astype(BF16)
        q_ref[:, lo + LANES:lo + QK_PAD_DIM] = _rope_lanes(acc[:, LANES:], tab).astype(BF16)


def _mla_q_proj(qn, w, q_tab, nope_scale, tm=512):
    s, k = qn.shape
    n = w.shape[1]
    return pl.pallas_call(
        functools.partial(_mla_q_kernel, nope_scale=nope_scale),
        out_shape=jax.ShapeDtypeStruct((s, n), BF16),
        grid=(s // tm,),
        in_specs=[pl.BlockSpec((tm, k), lambda i: (i, 0)),
                  pl.BlockSpec((k, n), lambda i: (0, 0)),
                  pl.BlockSpec((tm, 3 * LANES), lambda i: (i, 0))],
        out_specs=pl.BlockSpec((tm, n), lambda i: (i, 0)),
        compiler_params=_params(("parallel",)),
        name="mla_q_proj",
    )(qn, w, q_tab)


def _mla_attn_kernel(q_ref, k_ref, v_ref, kpe_ref, sg_ref, o_ref, kcat_ref, vcat_ref, *, tk):
    @pl.when(pl.program_id(1) == 0)
    def _():
        kcat_ref[:, :LANES] = k_ref[...]
        kcat_ref[:, LANES:] = kpe_ref[...]
        vcat_ref[:, :V_HEAD_DIM] = v_ref[...]
        vcat_ref[:, V_HEAD_DIM:] = jnp.ones((v_ref.shape[0], LANES), BF16)

    q = q_ref[...]
    tq = q.shape[0]
    n_kv = k_ref.shape[0] // tk

    def body(j, carry):
        m, acc = carry
        start = pl.multiple_of(j * tk, tk)
        s = _dot_nt(q, kcat_ref[pl.ds(start, tk), :])
        m_new = jnp.maximum(m, jnp.max(s, axis=1, keepdims=True))
        alpha = jnp.exp2(m - m_new)
        p = jnp.exp2(s - m_new).astype(BF16)
        acc = alpha * acc + _dot(p, vcat_ref[pl.ds(start, tk), :])
        return m_new, acc

    m0 = jnp.full((tq, 1), -jnp.inf, F32)
    acc0 = jnp.zeros((tq, V_HEAD_DIM + LANES), F32)
    _, acc = lax.fori_loop(0, n_kv, body, (m0, acc0), unroll=True)
    attn = acc[:, :V_HEAD_DIM] / acc[:, V_HEAD_DIM:]
    o_ref[...] = (attn * sg_ref[...].astype(F32)).astype(o_ref.dtype)


def _mla_attention(q, kv, kpe, gates, tq=1024, tk=1024):
    s = q.shape[0]
    return pl.pallas_call(
        functools.partial(_mla_attn_kernel, tk=tk),
        out_shape=jax.ShapeDtypeStruct((s, MLA_WIDTH), BF16),
        grid=(MLA_HEADS, s // tq),
        in_specs=[pl.BlockSpec((tq, QK_PAD_DIM), lambda h, i: (i, h)),
                  pl.BlockSpec((s, QK_NOPE_DIM), lambda h, i: (0, h)),
                  pl.BlockSpec((s, V_HEAD_DIM), lambda h, i: (0, MLA_HEADS + h)),
                  pl.BlockSpec((s, LANES), lambda h, i: (0, 0)),
                  pl.BlockSpec((tq, V_HEAD_DIM), lambda h, i: (i, h))],
        out_specs=pl.BlockSpec((tq, V_HEAD_DIM), lambda h, i: (i, h)),
        scratch_shapes=[pltpu.VMEM((s, QK_PAD_DIM), BF16),
                        pltpu.VMEM((s, V_HEAD_DIM + LANES), BF16)],
        compiler_params=_params(("parallel", "arbitrary")),
        name="mla_attention",
    )(q, kv, kv, kpe, gates)


def _band_bucket_index(dil, tq):
    nb = REL_BUCKETS // 2
    max_exact = nb // 2
    rel = (np.arange(tq + 2 * BAND_HALF)[None, :] - BAND_HALF - np.arange(tq)[:, None]) * dil
    n = np.abs(rel)
    nf = np.maximum(n, 1).astype(np.float64)
    large = max_exact + (np.log(nf / max_exact) / math.log(REL_MAX_DISTANCE / max_exact)
                         * (nb - max_exact)).astype(np.int32)
    large = np.minimum(large, nb - 1)
    return (np.where(rel > 0, nb, 0) + np.where(n < max_exact, n, large)).astype(np.int32)


def _band_bias_kernel(rb_ref, idx_ref, o_ref):
    hd = pl.program_id(0)
    idx = idx_ref[0]
    acc = jnp.zeros(idx.shape, F32)
    for b in range(REL_BUCKETS):
        acc = jnp.where(idx == b, rb_ref[b, hd], acc)
    o_ref[0] = acc


def _band_bias(rel_bias, tq):
    idx = jnp.asarray(np.stack([_band_bucket_index(dil, tq) for dil in DILATIONS]))
    tw = idx.shape[2]
    return pl.pallas_call(
        _band_bias_kernel,
        out_shape=jax.ShapeDtypeStruct((DSWA_HEADS, tq, tw), F32),
        grid=(DSWA_HEADS,),
        in_specs=[pl.BlockSpec(memory_space=pltpu.SMEM),
                  pl.BlockSpec((1, tq, tw), lambda h: (h // DSWA_HEADS_PER_GROUP, 0, 0))],
        out_specs=pl.BlockSpec((1, tq, tw), lambda h: (h, 0, 0)),
        compiler_params=_params(("parallel",)),
        name="band_bias",
    )(rel_bias, idx)


def _band_attn_kernel(q_ref, kp_ref, ko_ref, kn_ref, vp_ref, vo_ref, vn_ref, bias_ref,
                      o_ref, lse_ref):
    tq = q_ref.shape[0]
    tw = tq + 2 * BAND_HALF
    tile = pl.program_id(1)
    prev_ok = tile != 0
    next_ok = tile != pl.num_programs(1) - 1
    row = lax.broadcasted_iota(jnp.int32, (tq, tw), 0)
    col = lax.broadcasted_iota(jnp.int32, (tq, tw), 1)
    rel = col - BAND_HALF - row
    mask = ((jnp.abs(rel) <= BAND_HALF) & ((col >= BAND_HALF) | prev_ok)
            & ((col < tq + BAND_HALF) | next_ok))
    lane = lax.broadcasted_iota(jnp.int32, (tq, LANES), 1)
    scale = DSWA_HEAD_DIM ** -0.5
    heads = range(DSWA_HEADS_PER_GROUP)
    cols = [slice(hd * DSWA_HEAD_DIM, (hd + 1) * DSWA_HEAD_DIM) for hd in heads]
    kc = [jnp.concatenate([kp_ref[:, c], ko_ref[:, c], kn_ref[:, c]], axis=0) for c in cols]
    s = [_dot_nt(q_ref[:, cols[hd]], kc[hd]) * scale + bias_ref[hd] for hd in heads]
    s = [jnp.where(mask, sh, NEG_INF) for sh in s]
    m = [jnp.max(sh, axis=1, keepdims=True) for sh in s]
    p = [jnp.exp(sh - mh) for sh, mh in zip(s, m)]
    den = [jnp.sum(ph, axis=1, keepdims=True) for ph in p]
    pn = [(ph / dh).astype(BF16) for ph, dh in zip(p, den)]
    vc = [jnp.concatenate([vp_ref[:, c], vo_ref[:, c], vn_ref[:, c]], axis=0) for c in cols]
    for hd in heads:
        o_ref[:, cols[hd]] = _dot(pn[hd], vc[hd]).astype(o_ref.dtype)
    lse_tile = jnp.zeros((tq, LANES), F32)
    for hd in heads:
        lse_tile = jnp.where(lane == hd, m[hd] + jnp.log(den[hd]), lse_tile)
    lse_ref[...] = lse_tile


def _band_attention(qkv, bias, group, dil, tq=128):
    rows = qkv.shape[0]
    nt = rows // tq
    per_tile = tq // BAND_HALF
    n_edge = rows // BAND_HALF
    tw = tq + 2 * BAND_HALF

    def own(part):
        return pl.BlockSpec((tq, DSWA_WIDTH), lambda r, n: (n, 3 * r + part))

    def prev(part):
        return pl.BlockSpec((BAND_HALF, DSWA_WIDTH),
                            lambda r, n: (jnp.maximum(n * per_tile - 1, 0), 3 * r + part))

    def nxt(part):
        return pl.BlockSpec((BAND_HALF, DSWA_WIDTH),
                            lambda r, n: (jnp.minimum((n + 1) * per_tile, n_edge - 1), 3 * r + part))

    return pl.pallas_call(
        _band_attn_kernel,
        out_shape=(jax.ShapeDtypeStruct((rows, dil * DSWA_WIDTH), BF16),
                   jax.ShapeDtypeStruct((rows, dil * LANES), F32)),
        grid=(dil, nt),
        in_specs=[own(0),
                  prev(1), own(1), nxt(1),
                  prev(2), own(2), nxt(2),
                  pl.BlockSpec((DSWA_HEADS_PER_GROUP, tq, tw), lambda r, n: (group, 0, 0))],
        out_specs=(pl.BlockSpec((tq, DSWA_WIDTH), lambda r, n: (n, r)),
                   pl.BlockSpec((tq, LANES), lambda r, n: (n, r))),
        compiler_params=_params(("parallel", "parallel")),
        name=f"band_attention_d{dil}",
    )(qkv, qkv, qkv, qkv, qkv, qkv, qkv, bias)


def _mix_kernel(o1_ref, o4_ref, o16_ref, l1_ref, l4_ref, l16_ref, sg_ref, out_ref,
                o4n_ref, l4n_ref, o16n_ref, l16n_ref):
    for dil, o_ref, l_ref, on_ref, ln_ref in ((DILATIONS[1], o4_ref, l4_ref, o4n_ref, l4n_ref),
                                               (DILATIONS[2], o16_ref, l16_ref, o16n_ref, l16n_ref)):
        rows = o_ref.shape[0]
        for r in range(dil):
            for hd in range(DSWA_HEADS_PER_GROUP):
                lo = r * DSWA_WIDTH + hd * DSWA_HEAD_DIM
                on_ref[hd, pl.ds(r, rows, stride=dil), :] = (
                    o_ref[:, lo:lo + DSWA_HEAD_DIM].astype(F32))
            ln_ref[pl.ds(r, rows, stride=dil), :] = l_ref[:, r * LANES:(r + 1) * LANES]
    for hd in range(DSWA_HEADS_PER_GROUP):
        cols = slice(hd * DSWA_HEAD_DIM, (hd + 1) * DSWA_HEAD_DIM)
        l1 = l1_ref[:, hd:hd + 1]
        l2 = l4n_ref[:, hd:hd + 1]
        l3 = l16n_ref[:, hd:hd + 1]
        mx = jnp.maximum(jnp.maximum(l1, l2), l3)
        e1, e2, e3 = jnp.exp(l1 - mx), jnp.exp(l2 - mx), jnp.exp(l3 - mx)
        den = e1 + e2 + e3
        mix = ((e1 / den) * o1_ref[:, cols].astype(F32)
               + (e2 / den) * o4n_ref[hd]
               + (e3 / den) * o16n_ref[hd])
        out_ref[:, cols] = (mix * sg_ref[:, cols].astype(F32)).astype(out_ref.dtype)


def _group_mixture(outs, lses, gates, tm=256):
    o1, o4, o16 = outs
    l1, l4, l16 = lses
    s = o1.shape[0]
    d4, d16 = DILATIONS[1], DILATIONS[2]
    w = DSWA_WIDTH
    return pl.pallas_call(
        _mix_kernel,
        out_shape=jax.ShapeDtypeStruct((s, w), BF16),
        grid=(s // tm,),
        in_specs=[pl.BlockSpec((tm, w), lambda i: (i, 0)),
                  pl.BlockSpec((tm // d4, d4 * w), lambda i: (i, 0)),
                  pl.BlockSpec((tm // d16, d16 * w), lambda i: (i, 0)),
                  pl.BlockSpec((tm, LANES), lambda i: (i, 0)),
                  pl.BlockSpec((tm // d4, d4 * LANES), lambda i: (i, 0)),
                  pl.BlockSpec((tm // d16, d16 * LANES), lambda i: (i, 0)),
                  pl.BlockSpec((tm, w), lambda i: (i, MLA_WIDTH // w))],
        out_specs=pl.BlockSpec((tm, w), lambda i: (i, 0)),
        scratch_shapes=[pltpu.VMEM((DSWA_HEADS_PER_GROUP, tm, DSWA_HEAD_DIM), F32),
                        pltpu.VMEM((tm, LANES), F32),
                        pltpu.VMEM((DSWA_HEADS_PER_GROUP, tm, DSWA_HEAD_DIM), F32),
                        pltpu.VMEM((tm, LANES), F32)],
        compiler_params=_params(("parallel",)),
        name="group_mixture",
    )(o1, o4, o16, l1, l4, l16, gates)


def _merge_kernel(a_ref, b_ref, wm_ref, wd_ref, rm_ref, rd_ref, o_ref):
    y_mla = _dot(a_ref[...], wm_ref[...])
    y_dswa = _dot(b_ref[...], wd_ref[...])
    merged = rm_ref[...].astype(F32) * y_mla + rd_ref[...].astype(F32) * y_dswa
    o_ref[...] = merged.astype(o_ref.dtype)


def _merge_branches(a_gated, b_gated, gates, w_o_mla, w_o_dswa, tm=512, tn=COL_BLOCK):
    s = a_gated.shape[0]
    d = w_o_mla.shape[1]
    r_mla_block = (MLA_WIDTH + DSWA_WIDTH) // tn
    r_dswa_block = (MLA_WIDTH + DSWA_WIDTH + D_MODEL) // tn
    return pl.pallas_call(
        _merge_kernel,
        out_shape=jax.ShapeDtypeStruct((s, d), BF16),
        grid=(d // tn, s // tm),
        in_specs=[pl.BlockSpec((tm, MLA_WIDTH), lambda j, i: (i, 0)),
                  pl.BlockSpec((tm, DSWA_WIDTH), lambda j, i: (i, 0)),
                  pl.BlockSpec((MLA_WIDTH, tn), lambda j, i: (0, j)),
                  pl.BlockSpec((DSWA_WIDTH, tn), lambda j, i: (0, j)),
                  pl.BlockSpec((tm, tn), lambda j, i: (i, r_mla_block + j)),
                  pl.BlockSpec((tm, tn), lambda j, i: (i, r_dswa_block + j))],
        out_specs=pl.BlockSpec((tm, tn), lambda j, i: (i, j)),
        compiler_params=_params(("parallel", "parallel")),
        name="merge_branches",
    )(a_gated, b_gated, w_o_mla, w_o_dswa, gates, gates)


def _final_ln_kernel(x_ref, o_ref, g0_ref, b0_ref, g_ref, b_ref, out_ref):
    h = _layer_norm_rows(x_ref[...], g0_ref[...], b0_ref[...])
    out_ref[...] = _layer_norm_rows(DEEPNORM_ALPHA * h + o_ref[...], g_ref[...], b_ref[...])


def _final_layer_norm(x, out, g0, b0, g, b, tm=256):
    s, d = x.shape
    row = pl.BlockSpec((tm, d), lambda i: (i, 0))
    vec = pl.BlockSpec((1, d), lambda i: (0, 0))
    return pl.pallas_call(
        _final_ln_kernel,
        out_shape=jax.ShapeDtypeStruct((s, d), F32),
        grid=(s // tm,),
        in_specs=[row, row, vec, vec, vec, vec],
        out_specs=row,
        compiler_params=_params(("parallel",)),
        name="final_ln",
    )(x, out, g0.reshape(1, d), b0.reshape(1, d), g.reshape(1, d), b.reshape(1, d))


def _cast_kernel(w_ref, o_ref):
    o_ref[...] = w_ref[...].astype(o_ref.dtype)


def _cast_rows_bf16(w_t, row0, n_rows, name):
    k = w_t.shape[1]
    tr = math.gcd(MLA_A_COLS, REST_COLS)
    assert row0 % tr == 0 and n_rows % tr == 0 and tr % 16 == 0
    return pl.pallas_call(
        _cast_kernel,
        out_shape=jax.ShapeDtypeStruct((n_rows, k), BF16),
        grid=(n_rows // tr,),
        in_specs=[pl.BlockSpec((tr, k), lambda j: (row0 // tr + j, 0))],
        out_specs=pl.BlockSpec((tr, k), lambda j: (j, 0)),
        compiler_params=_params(("parallel",)),
        name=name,
    )(w_t)


def _rope_tables(seq, scale):
    pos = jnp.arange(seq, dtype=F32)
    inv_freq = 1.0 / (ROPE_THETA ** (jnp.arange(0, QK_ROPE_DIM, 2, dtype=F32) / QK_ROPE_DIM))
    ang = pos[:, None] * inv_freq[None, :]
    cos, sin = jnp.cos(ang) * scale, jnp.sin(ang) * scale
    z32 = jnp.zeros_like(cos)
    z64 = jnp.zeros((seq, LANES - QK_ROPE_DIM), F32)
    return jnp.concatenate([cos, cos, z64, z32, sin, z64, -sin, z32, z64], axis=1)


def kernel(x, emb_ln_g, emb_ln_b, rel_bias, w_in, q_a_norm_g, w_q_b, kv_a_norm_g, w_kv_b,
           w_o_mla, w_o_dswa, w_out, ln_g, ln_b):
    assert DEPTH == 1 and x.shape == (1, SEQ, D_MODEL)
    x2 = x[0]

    w_in_t = jnp.swapaxes(w_in, 1, 2)[0]
    w_mla_a = _cast_rows_bf16(w_in_t, 0, MLA_A_COLS, "mla_a_weights_bf16")
    w_rest = _cast_rows_bf16(w_in_t, MLA_A_COLS, REST_COLS, "rest_weights_bf16")
    w_q = jnp.pad(w_q_b[0].reshape(Q_LORA_RANK, MLA_HEADS, QK_HEAD_DIM),
                  ((0, 0), (0, 0), (0, QK_PAD_DIM - QK_HEAD_DIM))
                  ).reshape(Q_LORA_RANK, MLA_HEADS * QK_PAD_DIM).astype(BF16)
    w_kv = (w_kv_b[0].reshape(KV_LORA_RANK, MLA_HEADS, 2, QK_NOPE_DIM)
            .transpose(0, 2, 1, 3).reshape(KV_LORA_RANK, 2 * MLA_WIDTH).astype(BF16))
    w_om = w_o_mla[0].astype(BF16)
    w_od = w_o_dswa[0].astype(BF16)
    w_o = w_out[0].astype(BF16)

    q_scale = QK_HEAD_DIM ** -0.5 * math.log2(math.e)
    k_tab = _rope_tables(SEQ, 1.0)
    q_tab = _rope_tables(SEQ, q_scale)

    h_slabs = _input_layer_norm(x2, emb_ln_g, emb_ln_b)
    h = h_slabs[0]

    gates = _proj(h, w_rest, n_out=GATE_WIDTH, w_block=lambda j: GATE_COL0 + j, gated=True,
                  n_silu_blocks=(MLA_WIDTH + DSWA_WIDTH) // COL_BLOCK, w_transposed=True,
                  name="gate_proj")

    qn, ckvn, kpe = _mla_a_proj(h, w_mla_a, q_a_norm_g[0], kv_a_norm_g[0], k_tab)
    q = _mla_q_proj(qn, w_q, q_tab, q_scale)
    kv = _proj(ckvn, w_kv, n_out=2 * MLA_WIDTH, w_block=lambda j: j, name="mla_kv_proj")
    a_gated = _mla_attention(q, kv, kpe, gates)

    bias = _band_bias(rel_bias, 128)
    outs, lses = [], []
    blocks_per_part = DSWA_QKV_WIDTH // COL_BLOCK
    for g, dil in enumerate(DILATIONS):
        qkv = _proj(h_slabs[g], w_rest, n_out=3 * DSWA_WIDTH, dil=dil, w_transposed=True,
                    w_block=lambda j, g=g: j * blocks_per_part + g, name=f"dswa_proj_d{dil}")
        o, lse = _band_attention(qkv, bias, g, dil)
        outs.append(o)
        lses.append(lse)
    b_gated = _group_mixture(outs, lses, gates)

    merged = _merge_branches(a_gated, b_gated, gates, w_om, w_od)
    out = _proj(merged, w_o, n_out=D_MODEL, w_block=lambda j: j, out_dtype=F32, name="out_proj")
    y = _final_layer_norm(x2, out, emb_ln_g, emb_ln_b, ln_g[0], ln_b[0])
    return y[None]
```

```python
import functools
import math

import jax
import jax.numpy as jnp
import numpy as np
from jax import lax
from jax.experimental import pallas as pl
from jax.experimental.pallas import tpu as pltpu

D_MODEL = 4096
SEQ = 8192
DEPTH = 1

MLA_HEADS = 16
Q_LORA_RANK = 1024
KV_LORA_RANK = 512
QK_NOPE_DIM = 128
QK_ROPE_DIM = 64
QK_HEAD_DIM = QK_NOPE_DIM + QK_ROPE_DIM
V_HEAD_DIM = 128
MLA_WIDTH = MLA_HEADS * V_HEAD_DIM
ROPE_THETA = 10000.0

DIL_GROUPS = ((128, 1), (512, 4), (2048, 16))
DILATIONS = tuple(d for _, d in DIL_GROUPS)
DSWA_HEADS_PER_GROUP = 8
DSWA_HEADS = 24
DSWA_HEAD_DIM = 128
DSWA_QKV_WIDTH = DSWA_HEADS * DSWA_HEAD_DIM
DSWA_WIDTH = DSWA_HEADS_PER_GROUP * DSWA_HEAD_DIM
BAND_HALF = 64
BAND_TILE = 256

REL_BUCKETS = 32
REL_MAX_DISTANCE = 1024

DEEPNORM_ALPHA = (2.0 * DEPTH) ** 0.25
LN_EPS = 1e-5
RMS_EPS = 1e-6
NEG_INF = -1e30

IN_SPLITS = (Q_LORA_RANK, KV_LORA_RANK + QK_ROPE_DIM, DSWA_QKV_WIDTH, DSWA_QKV_WIDTH,
             DSWA_QKV_WIDTH, MLA_WIDTH, DSWA_WIDTH, D_MODEL, D_MODEL)
IN_OFFSETS = tuple(int(v) for v in np.cumsum(IN_SPLITS)[:-1])
MLA_A_COLS = IN_OFFSETS[1]
REST_COLS = sum(IN_SPLITS) - MLA_A_COLS

LANES = 128
QK_PAD_DIM = 256
COL_BLOCK = 1024
GATE_COL0 = 3 * DSWA_QKV_WIDTH // COL_BLOCK
GATE_WIDTH = MLA_WIDTH + DSWA_WIDTH + 2 * D_MODEL
VMEM_LIMIT = 56 * 1024 * 1024

BF16 = jnp.bfloat16
F32 = jnp.float32


def _params(sem):
    return pltpu.CompilerParams(dimension_semantics=sem, vmem_limit_bytes=VMEM_LIMIT)


def _dot(a, b):
    return jnp.dot(a, b, preferred_element_type=F32)


def _dot_nt(a, b):
    return lax.dot_general(a, b, (((1,), (1,)), ((), ())), preferred_element_type=F32)


def _sigmoid(x):
    return 0.5 * jnp.tanh(0.5 * x) + 0.5


def _layer_norm_rows(x, g, b):
    mu = jnp.mean(x, axis=-1, keepdims=True)
    xc = x - mu
    var = jnp.mean(xc * xc, axis=-1, keepdims=True)
    return xc * lax.rsqrt(var + LN_EPS) * g + b


def _rms_norm_rows(x, g):
    return x * lax.rsqrt(jnp.mean(x * x, axis=-1, keepdims=True) + RMS_EPS) * g


def _rope_lanes(t, tab):
    c, s1, s2 = tab[:, :LANES], tab[:, LANES:2 * LANES], tab[:, 2 * LANES:]
    return t * c + pltpu.roll(t, 32, 1) * s1 + pltpu.roll(t, LANES - 32, 1) * s2


def _ln_kernel(x_ref, g_ref, b_ref, h1_ref, h4_ref, h16_ref, y_ref):
    d = x_ref.shape[1]
    y = _layer_norm_rows(x_ref[...], g_ref[...], b_ref[...])
    h1_ref[...] = y.astype(BF16)
    for c in range(d // LANES):
        y_ref[c] = y[:, c * LANES:(c + 1) * LANES]
    for dil, slab_ref in ((DILATIONS[1], h4_ref), (DILATIONS[2], h16_ref)):
        rows = slab_ref.shape[0]
        for r in range(dil):
            for c in range(d // LANES):
                lo = r * d + c * LANES
                slab_ref[:, lo:lo + LANES] = y_ref[c, pl.ds(r, rows, stride=dil), :].astype(BF16)


def _input_layer_norm(x, g, b, tm=256):
    s, d = x.shape
    d4, d16 = DILATIONS[1], DILATIONS[2]
    return pl.pallas_call(
        _ln_kernel,
        out_shape=(jax.ShapeDtypeStruct((s, d), BF16),
                   jax.ShapeDtypeStruct((s // d4, d4 * d), BF16),
                   jax.ShapeDtypeStruct((s // d16, d16 * d), BF16)),
        grid=(s // tm,),
        in_specs=[pl.BlockSpec((tm, d), lambda i: (i, 0)),
                  pl.BlockSpec((1, d), lambda i: (0, 0)),
                  pl.BlockSpec((1, d), lambda i: (0, 0))],
        out_specs=(pl.BlockSpec((tm, d), lambda i: (i, 0)),
                   pl.BlockSpec((tm // d4, d4 * d), lambda i: (i, 0)),
                   pl.BlockSpec((tm // d16, d16 * d), lambda i: (i, 0))),
        scratch_shapes=[pltpu.VMEM((d // LANES, tm, LANES), F32)],
        compiler_params=_params(("parallel",)),
        name="input_ln",
    )(x, g.reshape(1, d), b.reshape(1, d))


def _proj_kernel(a_ref, w_ref, o_ref, *, n_silu_blocks, gated, w_transposed):
    acc = (_dot_nt if w_transposed else _dot)(a_ref[...], w_ref[...])
    if gated:
        sig = _sigmoid(acc)
        acc = jnp.where(pl.program_id(0) < n_silu_blocks, acc * sig, sig)
    o_ref[...] = acc.astype(o_ref.dtype)


def _proj(a, w, *, n_out, w_block, dil=1, tm=1024, tn=COL_BLOCK, gated=False, n_silu_blocks=0,
          w_transposed=False, out_dtype=BF16, name):
    rows, kd = a.shape
    k = kd // dil
    nj = n_out // tn
    tm = min(tm, rows)
    kern = functools.partial(_proj_kernel, n_silu_blocks=n_silu_blocks, gated=gated,
                             w_transposed=w_transposed)
    if w_transposed:
        w_spec = pl.BlockSpec((tn, k), lambda j, r, i: (w_block(j), 0))
    else:
        w_spec = pl.BlockSpec((k, tn), lambda j, r, i: (0, w_block(j)))
    return pl.pallas_call(
        kern,
        out_shape=jax.ShapeDtypeStruct((rows, dil * n_out), out_dtype),
        grid=(nj, dil, rows // tm),
        in_specs=[pl.BlockSpec((tm, k), lambda j, r, i: (i, r)), w_spec],
        out_specs=pl.BlockSpec((tm, tn), lambda j, r, i: (i, r * nj + j)),
        compiler_params=_params(("parallel", "parallel", "parallel")),
        name=name,
    )(a, w)


def _mla_a_kernel(h_ref, w_ref, gq_ref, gkv_ref, tab_ref, qn_ref, ckvn_ref, kpe_ref):
    acc = _dot_nt(h_ref[...], w_ref[...])
    qn_ref[...] = _rms_norm_rows(acc[:, :Q_LORA_RANK], gq_ref[...]).astype(BF16)
    ckv = acc[:, Q_LORA_RANK:Q_LORA_RANK + KV_LORA_RANK]
    ckvn_ref[...] = _rms_norm_rows(ckv, gkv_ref[...]).astype(BF16)
    k_pe = acc[:, Q_LORA_RANK + KV_LORA_RANK:]
    k_pe = jnp.concatenate([k_pe, jnp.zeros((k_pe.shape[0], LANES - QK_ROPE_DIM), F32)], axis=1)
    kpe_ref[...] = _rope_lanes(k_pe, tab_ref[...]).astype(BF16)


def _mla_a_proj(h, w, gq, gkv, k_tab, tm=256):
    s, k = h.shape
    return pl.pallas_call(
        _mla_a_kernel,
        out_shape=(jax.ShapeDtypeStruct((s, Q_LORA_RANK), BF16),
                   jax.ShapeDtypeStruct((s, KV_LORA_RANK), BF16),
                   jax.ShapeDtypeStruct((s, LANES), BF16)),
        grid=(s // tm,),
        in_specs=[pl.BlockSpec((tm, k), lambda i: (i, 0)),
                  pl.BlockSpec((MLA_A_COLS, k), lambda i: (0, 0)),
                  pl.BlockSpec((1, Q_LORA_RANK), lambda i: (0, 0)),
                  pl.BlockSpec((1, KV_LORA_RANK), lambda i: (0, 0)),
                  pl.BlockSpec((tm, 3 * LANES), lambda i: (i, 0))],
        out_specs=(pl.BlockSpec((tm, Q_LORA_RANK), lambda i: (i, 0)),
                   pl.BlockSpec((tm, KV_LORA_RANK), lambda i: (i, 0)),
                   pl.BlockSpec((tm, LANES), lambda i: (i, 0))),
        compiler_params=_params(("parallel",)),
        name="mla_a_proj",
    )(h, w, gq.reshape(1, -1), gkv.reshape(1, -1), k_tab)


def _mla_q_kernel(qn_ref, w_ref, tab_ref, q_ref, *, q_scale):
    qn = qn_ref[...]
    tab = tab_ref[...]
    for hd in range(MLA_HEADS):
        lo = hd * QK_PAD_DIM
        acc = _dot(qn, w_ref[:, lo:lo + QK_PAD_DIM]) * q_scale
        q_ref[:, lo:lo + LANES] = acc[:, :LANES].astype(BF16)
        q_ref[:, lo + LANES:lo + QK_PAD_DIM] = _rope_lanes(acc[:, LANES:], tab).astype(BF16)


def _mla_q_proj(qn, w, rope_tab, q_scale, tm=512):
    s, k = qn.shape
    n = w.shape[1]
    return pl.pallas_call(
        functools.partial(_mla_q_kernel, q_scale=q_scale),
        out_shape=jax.ShapeDtypeStruct((s, n), BF16),
        grid=(s // tm,),
        in_specs=[pl.BlockSpec((tm, k), lambda i: (i, 0)),
                  pl.BlockSpec((k, n), lambda i: (0, 0)),
                  pl.BlockSpec((tm, 3 * LANES), lambda i: (i, 0))],
        out_specs=pl.BlockSpec((tm, n), lambda i: (i, 0)),
        compiler_params=_params(("parallel",)),
        name="mla_q_proj",
    )(qn, w, rope_tab)


def _mla_attn_kernel(q_ref, k_ref, v_ref, kpe_ref, sg_ref, o_ref, kcat_ref, vcat_ref, *, tk):
    @pl.when(pl.program_id(1) == 0)
    def _():
        kcat_ref[:, :LANES] = k_ref[...]
        kcat_ref[:, LANES:] = kpe_ref[...]
        vcat_ref[:, :V_HEAD_DIM] = v_ref[...]
        vcat_ref[:, V_HEAD_DIM:] = jnp.ones((v_ref.shape[0], LANES), BF16)

    q = q_ref[...]
    tq = q.shape[0]
    n_kv = k_ref.shape[0] // tk

    def body(j, carry):
        m, acc = carry
        start = pl.multiple_of(j * tk, tk)
        s = _dot_nt(q, kcat_ref[pl.ds(start, tk), :])
        m_new = jnp.maximum(m, jnp.max(s, axis=1, keepdims=True))
        alpha = jnp.exp2(m - m_new)
        p = jnp.exp2(s - m_new).astype(BF16)
        acc = alpha * acc + _dot(p, vcat_ref[pl.ds(start, tk), :])
        return m_new, acc

    m0 = jnp.full((tq, 1), -jnp.inf, F32)
    acc0 = jnp.zeros((tq, V_HEAD_DIM + LANES), F32)
    _, acc = lax.fori_loop(0, n_kv, body, (m0, acc0), unroll=True)
    attn = acc[:, :V_HEAD_DIM] / acc[:, V_HEAD_DIM:]
    o_ref[...] = (attn * sg_ref[...].astype(F32)).astype(o_ref.dtype)


def _mla_attention(q, kv, kpe, gates, tq=2048, tk=512):
    s = q.shape[0]
    return pl.pallas_call(
        functools.partial(_mla_attn_kernel, tk=tk),
        out_shape=jax.ShapeDtypeStruct((s, MLA_WIDTH), BF16),
        grid=(MLA_HEADS, s // tq),
        in_specs=[pl.BlockSpec((tq, QK_PAD_DIM), lambda h, i: (i, h)),
                  pl.BlockSpec((s, QK_NOPE_DIM), lambda h, i: (0, h)),
                  pl.BlockSpec((s, V_HEAD_DIM), lambda h, i: (0, MLA_HEADS + h)),
                  pl.BlockSpec((s, LANES), lambda h, i: (0, 0)),
                  pl.BlockSpec((tq, V_HEAD_DIM), lambda h, i: (i, h))],
        out_specs=pl.BlockSpec((tq, V_HEAD_DIM), lambda h, i: (i, h)),
        scratch_shapes=[pltpu.VMEM((s, QK_PAD_DIM), BF16),
                        pltpu.VMEM((s, V_HEAD_DIM + LANES), BF16)],
        compiler_params=_params(("parallel", "arbitrary")),
        name="mla_attention",
    )(q, kv, kv, kpe, gates)


def _band_bucket_index(dil, tq):
    nb = REL_BUCKETS // 2
    max_exact = nb // 2
    rel = (np.arange(tq + 2 * BAND_HALF)[None, :] - BAND_HALF - np.arange(tq)[:, None]) * dil
    n = np.abs(rel)
    nf = np.maximum(n, 1).astype(np.float64)
    large = max_exact + (np.log(nf / max_exact) / math.log(REL_MAX_DISTANCE / max_exact)
                         * (nb - max_exact)).astype(np.int32)
    large = np.minimum(large, nb - 1)
    return (np.where(rel > 0, nb, 0) + np.where(n < max_exact, n, large)).astype(np.int32)


def _band_bias_kernel(rb_ref, idx_ref, o_ref):
    hd = pl.program_id(0)
    idx = idx_ref[0]
    acc = jnp.zeros(idx.shape, F32)
    for b in range(REL_BUCKETS):
        acc = jnp.where(idx == b, rb_ref[b, hd], acc)
    o_ref[0] = acc


def _band_bias(rel_bias, tq):
    idx = jnp.asarray(np.stack([_band_bucket_index(dil, tq) for dil in DILATIONS]))
    tw = idx.shape[2]
    return pl.pallas_call(
        _band_bias_kernel,
        out_shape=jax.ShapeDtypeStruct((DSWA_HEADS, tq, tw), F32),
        grid=(DSWA_HEADS,),
        in_specs=[pl.BlockSpec(memory_space=pltpu.SMEM),
                  pl.BlockSpec((1, tq, tw), lambda h: (h // DSWA_HEADS_PER_GROUP, 0, 0))],
        out_specs=pl.BlockSpec((1, tq, tw), lambda h: (h, 0, 0)),
        compiler_params=_params(("parallel",)),
        name="band_bias",
    )(rel_bias, idx)


def _band_attn_kernel(q_ref, kp_ref, ko_ref, kn_ref, vp_ref, vo_ref, vn_ref, bias_ref,
                      o_ref, lse_ref):
    tq = q_ref.shape[0]
    tw = tq + 2 * BAND_HALF
    tile = pl.program_id(1)
    prev_ok = tile != 0
    next_ok = tile != pl.num_programs(1) - 1
    row = lax.broadcasted_iota(jnp.int32, (tq, tw), 0)
    col = lax.broadcasted_iota(jnp.int32, (tq, tw), 1)
    rel = col - BAND_HALF - row
    mask = ((jnp.abs(rel) <= BAND_HALF) & ((col >= BAND_HALF) | prev_ok)
            & ((col < tq + BAND_HALF) | next_ok))
    lane = lax.broadcasted_iota(jnp.int32, (tq, LANES), 1)
    scale = DSWA_HEAD_DIM ** -0.5
    heads = range(DSWA_HEADS_PER_GROUP)
    cols = [slice(hd * DSWA_HEAD_DIM, (hd + 1) * DSWA_HEAD_DIM) for hd in heads]
    kc = [jnp.concatenate([kp_ref[:, c], ko_ref[:, c], kn_ref[:, c]], axis=0) for c in cols]
    s = [_dot_nt(q_ref[:, cols[hd]], kc[hd]) * scale + bias_ref[hd] for hd in heads]
    s = [jnp.where(mask, sh, NEG_INF) for sh in s]
    m = [jnp.max(sh, axis=1, keepdims=True) for sh in s]
    p = [jnp.exp(sh - mh).astype(BF16) for sh, mh in zip(s, m)]
    ones = jnp.ones((tw, LANES), BF16)
    vc = [jnp.concatenate(
        [jnp.concatenate([vp_ref[:, c], vo_ref[:, c], vn_ref[:, c]], axis=0), ones], axis=1)
        for c in cols]
    pv = [_dot(p[hd], vc[hd]) for hd in heads]
    lse_tile = jnp.zeros((tq, LANES), F32)
    for hd in heads:
        den = pv[hd][:, DSWA_HEAD_DIM:]
        o_ref[:, cols[hd]] = (pv[hd][:, :DSWA_HEAD_DIM] / den).astype(o_ref.dtype)
        lse_tile = jnp.where(lane == hd, m[hd] + jnp.log(den), lse_tile)
    lse_ref[...] = lse_tile


def _band_attention(qkv, bias, group, dil):
    rows = qkv.shape[0]
    tq = bias.shape[1]
    nt = rows // tq
    per_tile = tq // BAND_HALF
    n_edge = rows // BAND_HALF
    tw = tq + 2 * BAND_HALF

    def own(part):
        return pl.BlockSpec((tq, DSWA_WIDTH), lambda r, n: (n, 3 * r + part))

    def prev(part):
        return pl.BlockSpec((BAND_HALF, DSWA_WIDTH),
                            lambda r, n: (jnp.maximum(n * per_tile - 1, 0), 3 * r + part))

    def nxt(part):
        return pl.BlockSpec((BAND_HALF, DSWA_WIDTH),
                            lambda r, n: (jnp.minimum((n + 1) * per_tile, n_edge - 1), 3 * r + part))

    return pl.pallas_call(
        _band_attn_kernel,
        out_shape=(jax.ShapeDtypeStruct((rows, dil * DSWA_WIDTH), BF16),
                   jax.ShapeDtypeStruct((rows, dil * LANES), F32)),
        grid=(dil, nt),
        in_specs=[own(0),
                  prev(1), own(1), nxt(1),
                  prev(2), own(2), nxt(2),
                  pl.BlockSpec((DSWA_HEADS_PER_GROUP, tq, tw), lambda r, n: (group, 0, 0))],
        out_specs=(pl.BlockSpec((tq, DSWA_WIDTH), lambda r, n: (n, r)),
                   pl.BlockSpec((tq, LANES), lambda r, n: (n, r))),
        compiler_params=_params(("parallel", "parallel")),
        name=f"band_attention_d{dil}",
    )(qkv, qkv, qkv, qkv, qkv, qkv, qkv, bias)


def _mix_kernel(o1_ref, o4_ref, o16_ref, l1_ref, l4_ref, l16_ref, sg_ref, out_ref,
                o4n_ref, l4n_ref, o16n_ref, l16n_ref):
    for dil, o_ref, l_ref, on_ref, ln_ref in ((DILATIONS[1], o4_ref, l4_ref, o4n_ref, l4n_ref),
                                               (DILATIONS[2], o16_ref, l16_ref, o16n_ref, l16n_ref)):
        rows = o_ref.shape[0]
        for r in range(dil):
            for hd in range(DSWA_HEADS_PER_GROUP):
                lo = r * DSWA_WIDTH + hd * DSWA_HEAD_DIM
                on_ref[hd, pl.ds(r, rows, stride=dil), :] = (
                    o_ref[:, lo:lo + DSWA_HEAD_DIM].astype(F32))
            ln_ref[pl.ds(r, rows, stride=dil), :] = l_ref[:, r * LANES:(r + 1) * LANES]
    for hd in range(DSWA_HEADS_PER_GROUP):
        cols = slice(hd * DSWA_HEAD_DIM, (hd + 1) * DSWA_HEAD_DIM)
        l1 = l1_ref[:, hd:hd + 1]
        l2 = l4n_ref[:, hd:hd + 1]
        l3 = l16n_ref[:, hd:hd + 1]
        mx = jnp.maximum(jnp.maximum(l1, l2), l3)
        e1, e2, e3 = jnp.exp(l1 - mx), jnp.exp(l2 - mx), jnp.exp(l3 - mx)
        den = e1 + e2 + e3
        mix = ((e1 / den) * o1_ref[:, cols].astype(F32)
               + (e2 / den) * o4n_ref[hd]
               + (e3 / den) * o16n_ref[hd])
        out_ref[:, cols] = (mix * sg_ref[:, cols].astype(F32)).astype(out_ref.dtype)


def _group_mixture(outs, lses, gates, tm=256):
    o1, o4, o16 = outs
    l1, l4, l16 = lses
    s = o1.shape[0]
    d4, d16 = DILATIONS[1], DILATIONS[2]
    w = DSWA_WIDTH
    return pl.pallas_call(
        _mix_kernel,
        out_shape=jax.ShapeDtypeStruct((s, w), BF16),
        grid=(s // tm,),
        in_specs=[pl.BlockSpec((tm, w), lambda i: (i, 0)),
                  pl.BlockSpec((tm // d4, d4 * w), lambda i: (i, 0)),
                  pl.BlockSpec((tm // d16, d16 * w), lambda i: (i, 0)),
                  pl.BlockSpec((tm, LANES), lambda i: (i, 0)),
                  pl.BlockSpec((tm // d4, d4 * LANES), lambda i: (i, 0)),
                  pl.BlockSpec((tm // d16, d16 * LANES), lambda i: (i, 0)),
                  pl.BlockSpec((tm, w), lambda i: (i, MLA_WIDTH // w))],
        out_specs=pl.BlockSpec((tm, w), lambda i: (i, 0)),
        scratch_shapes=[pltpu.VMEM((DSWA_HEADS_PER_GROUP, tm, DSWA_HEAD_DIM), F32),
                        pltpu.VMEM((tm, LANES), F32),
                        pltpu.VMEM((DSWA_HEADS_PER_GROUP, tm, DSWA_HEAD_DIM), F32),
                        pltpu.VMEM((tm, LANES), F32)],
        compiler_params=_params(("parallel",)),
        name="group_mixture",
    )(o1, o4, o16, l1, l4, l16, gates)


def _merge_kernel(a_ref, b_ref, wm_ref, wd_ref, rm_ref, rd_ref, o_ref):
    y_mla = _dot(a_ref[...], wm_ref[...])
    y_dswa = _dot(b_ref[...], wd_ref[...])
    merged = rm_ref[...].astype(F32) * y_mla + rd_ref[...].astype(F32) * y_dswa
    o_ref[...] = merged.astype(o_ref.dtype)


def _merge_branches(a_gated, b_gated, gates, w_o_mla, w_o_dswa, tm=512, tn=COL_BLOCK):
    s = a_gated.shape[0]
    d = w_o_mla.shape[1]
    r_mla_block = (MLA_WIDTH + DSWA_WIDTH) // tn
    r_dswa_block = (MLA_WIDTH + DSWA_WIDTH + D_MODEL) // tn
    return pl.pallas_call(
        _merge_kernel,
        out_shape=jax.ShapeDtypeStruct((s, d), BF16),
        grid=(d // tn, s // tm),
        in_specs=[pl.BlockSpec((tm, MLA_WIDTH), lambda j, i: (i, 0)),
                  pl.BlockSpec((tm, DSWA_WIDTH), lambda j, i: (i, 0)),
                  pl.BlockSpec((MLA_WIDTH, tn), lambda j, i: (0, j)),
                  pl.BlockSpec((DSWA_WIDTH, tn), lambda j, i: (0, j)),
                  pl.BlockSpec((tm, tn), lambda j, i: (i, r_mla_block + j)),
                  pl.BlockSpec((tm, tn), lambda j, i: (i, r_dswa_block + j))],
        out_specs=pl.BlockSpec((tm, tn), lambda j, i: (i, j)),
        compiler_params=_params(("parallel", "parallel")),
        name="merge_branches",
    )(a_gated, b_gated, w_o_mla, w_o_dswa, gates, gates)


def _final_ln_kernel(x_ref, o_ref, g0_ref, b0_ref, g_ref, b_ref, out_ref):
    h = _layer_norm_rows(x_ref[...], g0_ref[...], b0_ref[...])
    out_ref[...] = _layer_norm_rows(DEEPNORM_ALPHA * h + o_ref[...], g_ref[...], b_ref[...])


def _final_layer_norm(x, out, g0, b0, g, b, tm=256):
    s, d = x.shape
    row = pl.BlockSpec((tm, d), lambda i: (i, 0))
    vec = pl.BlockSpec((1, d), lambda i: (0, 0))
    return pl.pallas_call(
        _final_ln_kernel,
        out_shape=jax.ShapeDtypeStruct((s, d), F32),
        grid=(s // tm,),
        in_specs=[row, row, vec, vec, vec, vec],
        out_specs=row,
        compiler_params=_params(("parallel",)),
        name="final_ln",
    )(x, out, g0.reshape(1, d), b0.reshape(1, d), g.reshape(1, d), b.reshape(1, d))


def _cast_kernel(w_ref, o_ref):
    o_ref[...] = w_ref[...].astype(o_ref.dtype)


def _cast_rows_bf16(w_t, row0, n_rows, name):
    k = w_t.shape[1]
    tr = math.gcd(MLA_A_COLS, REST_COLS)
    assert row0 % tr == 0 and n_rows % tr == 0 and tr % 16 == 0
    return pl.pallas_call(
        _cast_kernel,
        out_shape=jax.ShapeDtypeStruct((n_rows, k), BF16),
        grid=(n_rows // tr,),
        in_specs=[pl.BlockSpec((tr, k), lambda j: (row0 // tr + j, 0))],
        out_specs=pl.BlockSpec((tr, k), lambda j: (j, 0)),
        compiler_params=_params(("parallel",)),
        name=name,
    )(w_t)


def _rope_tables(seq):
    pos = jnp.arange(seq, dtype=F32)
    inv_freq = 1.0 / (ROPE_THETA ** (jnp.arange(0, QK_ROPE_DIM, 2, dtype=F32) / QK_ROPE_DIM))
    ang = pos[:, None] * inv_freq[None, :]
    cos, sin = jnp.cos(ang), jnp.sin(ang)
    z32 = jnp.zeros_like(cos)
    z64 = jnp.zeros((seq, LANES - QK_ROPE_DIM), F32)
    return jnp.concatenate([cos, cos, z64, z32, sin, z64, -sin, z32, z64], axis=1)


def kernel(x, emb_ln_g, emb_ln_b, rel_bias, w_in, q_a_norm_g, w_q_b, kv_a_norm_g, w_kv_b,
           w_o_mla, w_o_dswa, w_out, ln_g, ln_b):
    assert DEPTH == 1 and x.shape == (1, SEQ, D_MODEL)
    x2 = x[0]

    w_in_t = jnp.swapaxes(w_in, 1, 2)[0]
    w_mla_a = _cast_rows_bf16(w_in_t, 0, MLA_A_COLS, "mla_a_weights_bf16")
    w_rest = _cast_rows_bf16(w_in_t, MLA_A_COLS, REST_COLS, "rest_weights_bf16")
    w_q = jnp.pad(w_q_b[0].reshape(Q_LORA_RANK, MLA_HEADS, QK_HEAD_DIM),
                  ((0, 0), (0, 0), (0, QK_PAD_DIM - QK_HEAD_DIM))
                  ).reshape(Q_LORA_RANK, MLA_HEADS * QK_PAD_DIM).astype(BF16)
    w_kv = (w_kv_b[0].reshape(KV_LORA_RANK, MLA_HEADS, 2, QK_NOPE_DIM)
            .transpose(0, 2, 1, 3).reshape(KV_LORA_RANK, 2 * MLA_WIDTH).astype(BF16))
    w_om = w_o_mla[0].astype(BF16)
    w_od = w_o_dswa[0].astype(BF16)
    w_o = w_out[0].astype(BF16)

    q_scale = QK_HEAD_DIM ** -0.5 * math.log2(math.e)
    rope_tab = _rope_tables(SEQ)

    h_slabs = _input_layer_norm(x2, emb_ln_g, emb_ln_b)
    h = h_slabs[0]

    gates = _proj(h, w_rest, n_out=GATE_WIDTH, w_block=lambda j: GATE_COL0 + j, gated=True,
                  n_silu_blocks=(MLA_WIDTH + DSWA_WIDTH) // COL_BLOCK, w_transposed=True,
                  name="gate_proj")

    qn, ckvn, kpe = _mla_a_proj(h, w_mla_a, q_a_norm_g[0], kv_a_norm_g[0], rope_tab)
    q = _mla_q_proj(qn, w_q, rope_tab, q_scale)
    kv = _proj(ckvn, w_kv, n_out=2 * MLA_WIDTH, w_block=lambda j: j, name="mla_kv_proj")
    a_gated = _mla_attention(q, kv, kpe, gates)

    bias = _band_bias(rel_bias, BAND_TILE)
    outs, lses = [], []
    blocks_per_part = DSWA_QKV_WIDTH // COL_BLOCK
    for g, dil in enumerate(DILATIONS):
        qkv = _proj(h_slabs[g], w_rest, n_out=3 * DSWA_WIDTH, dil=dil, w_transposed=True,
                    w_block=lambda j, g=g: j * blocks_per_part + g, name=f"dswa_proj_d{dil}")
        o, lse = _band_attention(qkv, bias, g, dil)
        outs.append(o)
        lses.append(lse)
    b_gated = _group_mixture(outs, lses, gates)

    merged = _merge_branches(a_gated, b_gated, gates, w_om, w_od)
    out = _proj(merged, w_o, n_out=D_MODEL, w_block=lambda j: j, out_dtype=F32, name="out_proj")
    y = _final_layer_norm(x2, out, emb_ln_g, emb_ln_b, ln_g[0], ln_b[0])
    return y[None]
```

```python
import functools
import math

import jax
import jax.numpy as jnp
import numpy as np
from jax import lax
from jax.experimental import pallas as pl
from jax.experimental.pallas import tpu as pltpu

D_MODEL = 4096
SEQ = 8192
DEPTH = 1

MLA_HEADS = 16
Q_LORA_RANK = 1024
KV_LORA_RANK = 512
QK_NOPE_DIM = 128
QK_ROPE_DIM = 64
QK_HEAD_DIM = QK_NOPE_DIM + QK_ROPE_DIM
V_HEAD_DIM = 128
MLA_WIDTH = MLA_HEADS * V_HEAD_DIM
ROPE_THETA = 10000.0

DIL_GROUPS = ((128, 1), (512, 4), (2048, 16))
DILATIONS = tuple(d for _, d in DIL_GROUPS)
DSWA_HEADS_PER_GROUP = 8
DSWA_HEADS = 24
DSWA_HEAD_DIM = 128
DSWA_QKV_WIDTH = DSWA_HEADS * DSWA_HEAD_DIM
DSWA_WIDTH = DSWA_HEADS_PER_GROUP * DSWA_HEAD_DIM
BAND_HALF = 64
BAND_TILE = 256

REL_BUCKETS = 32
REL_MAX_DISTANCE = 1024

DEEPNORM_ALPHA = (2.0 * DEPTH) ** 0.25
LN_EPS = 1e-5
RMS_EPS = 1e-6
NEG_INF = -1e30

IN_SPLITS = (Q_LORA_RANK, KV_LORA_RANK + QK_ROPE_DIM, DSWA_QKV_WIDTH, DSWA_QKV_WIDTH,
             DSWA_QKV_WIDTH, MLA_WIDTH, DSWA_WIDTH, D_MODEL, D_MODEL)
IN_OFFSETS = tuple(int(v) for v in np.cumsum(IN_SPLITS)[:-1])
MLA_A_COLS = IN_OFFSETS[1]
REST_COLS = sum(IN_SPLITS) - MLA_A_COLS

LANES = 128
QK_PAD_DIM = 256
COL_BLOCK = 1024
GATE_COL0 = 3 * DSWA_QKV_WIDTH // COL_BLOCK
GATE_WIDTH = MLA_WIDTH + DSWA_WIDTH + 2 * D_MODEL
VMEM_LIMIT = 56 * 1024 * 1024

BF16 = jnp.bfloat16
F32 = jnp.float32


def _params(sem):
    return pltpu.CompilerParams(dimension_semantics=sem, vmem_limit_bytes=VMEM_LIMIT)


def _dot(a, b):
    return jnp.dot(a, b, preferred_element_type=F32)


def _dot_nt(a, b):
    return lax.dot_general(a, b, (((1,), (1,)), ((), ())), preferred_element_type=F32)


def _sigmoid(x):
    return 0.5 * jnp.tanh(0.5 * x) + 0.5


def _layer_norm_rows(x, g, b):
    mu = jnp.mean(x, axis=-1, keepdims=True)
    xc = x - mu
    var = jnp.mean(xc * xc, axis=-1, keepdims=True)
    return xc * lax.rsqrt(var + LN_EPS) * g + b


def _rms_norm_rows(x, g):
    return x * lax.rsqrt(jnp.mean(x * x, axis=-1, keepdims=True) + RMS_EPS) * g


def _rope_lanes(t, tab):
    c, s1, s2 = tab[:, :LANES], tab[:, LANES:2 * LANES], tab[:, 2 * LANES:]
    return t * c + pltpu.roll(t, 32, 1) * s1 + pltpu.roll(t, LANES - 32, 1) * s2


def _ln_kernel(x_ref, g_ref, b_ref, h1_ref, h4_ref, h16_ref, y_ref):
    d = x_ref.shape[1]
    y = _layer_norm_rows(x_ref[...], g_ref[...], b_ref[...])
    h1_ref[...] = y.astype(BF16)
    for c in range(d // LANES):
        y_ref[c] = y[:, c * LANES:(c + 1) * LANES]
    for dil, slab_ref in ((DILATIONS[1], h4_ref), (DILATIONS[2], h16_ref)):
        rows = slab_ref.shape[0]
        for r in range(dil):
            for c in range(d // LANES):
                lo = r * d + c * LANES
                slab_ref[:, lo:lo + LANES] = y_ref[c, pl.ds(r, rows, stride=dil), :].astype(BF16)


def _input_layer_norm(x, g, b, tm=256):
    s, d = x.shape
    d4, d16 = DILATIONS[1], DILATIONS[2]
    return pl.pallas_call(
        _ln_kernel,
        out_shape=(jax.ShapeDtypeStruct((s, d), BF16),
                   jax.ShapeDtypeStruct((s // d4, d4 * d), BF16),
                   jax.ShapeDtypeStruct((s // d16, d16 * d), BF16)),
        grid=(s // tm,),
        in_specs=[pl.BlockSpec((tm, d), lambda i: (i, 0)),
                  pl.BlockSpec((1, d), lambda i: (0, 0)),
                  pl.BlockSpec((1, d), lambda i: (0, 0))],
        out_specs=(pl.BlockSpec((tm, d), lambda i: (i, 0)),
                   pl.BlockSpec((tm // d4, d4 * d), lambda i: (i, 0)),
                   pl.BlockSpec((tm // d16, d16 * d), lambda i: (i, 0))),
        scratch_shapes=[pltpu.VMEM((d // LANES, tm, LANES), F32)],
        compiler_params=_params(("parallel",)),
        name="input_ln",
    )(x, g.reshape(1, d), b.reshape(1, d))


def _proj_kernel(a_ref, w_ref, o_ref, *, n_silu_blocks, gated, w_transposed):
    acc = (_dot_nt if w_transposed else _dot)(a_ref[...], w_ref[...])
    if gated:
        sig = _sigmoid(acc)
        acc = jnp.where(pl.program_id(0) < n_silu_blocks, acc * sig, sig)
    o_ref[...] = acc.astype(o_ref.dtype)


def _proj(a, w, *, n_out, w_block, dil=1, tm=1024, tn=COL_BLOCK, gated=False, n_silu_blocks=0,
          w_transposed=False, out_dtype=BF16, name):
    rows, kd = a.shape
    k = kd // dil
    nj = n_out // tn
    tm = min(tm, rows)
    kern = functools.partial(_proj_kernel, n_silu_blocks=n_silu_blocks, gated=gated,
                             w_transposed=w_transposed)
    if w_transposed:
        w_spec = pl.BlockSpec((tn, k), lambda j, r, i: (w_block(j), 0))
    else:
        w_spec = pl.BlockSpec((k, tn), lambda j, r, i: (0, w_block(j)))
    return pl.pallas_call(
        kern,
        out_shape=jax.ShapeDtypeStruct((rows, dil * n_out), out_dtype),
        grid=(nj, dil, rows // tm),
        in_specs=[pl.BlockSpec((tm, k), lambda j, r, i: (i, r)), w_spec],
        out_specs=pl.BlockSpec((tm, tn), lambda j, r, i: (i, r * nj + j)),
        compiler_params=_params(("parallel", "parallel", "parallel")),
        name=name,
    )(a, w)


def _mla_a_kernel(h_ref, w_ref, gq_ref, gkv_ref, tab_ref, qn_ref, ckvn_ref, kpe_ref):
    acc = _dot_nt(h_ref[...], w_ref[...])
    qn_ref[...] = _rms_norm_rows(acc[:, :Q_LORA_RANK], gq_ref[...]).astype(BF16)
    ckv = acc[:, Q_LORA_RANK:Q_LORA_RANK + KV_LORA_RANK]
    ckvn_ref[...] = _rms_norm_rows(ckv, gkv_ref[...]).astype(BF16)
    k_pe = acc[:, Q_LORA_RANK + KV_LORA_RANK:]
    k_pe = jnp.concatenate([k_pe, jnp.zeros((k_pe.shape[0], LANES - QK_ROPE_DIM), F32)], axis=1)
    kpe_ref[...] = _rope_lanes(k_pe, tab_ref[...]).astype(BF16)


def _mla_a_proj(h, w, gq, gkv, k_tab, tm=512):
    s, k = h.shape
    return pl.pallas_call(
        _mla_a_kernel,
        out_shape=(jax.ShapeDtypeStruct((s, Q_LORA_RANK), BF16),
                   jax.ShapeDtypeStruct((s, KV_LORA_RANK), BF16),
                   jax.ShapeDtypeStruct((s, LANES), BF16)),
        grid=(s // tm,),
        in_specs=[pl.BlockSpec((tm, k), lambda i: (i, 0)),
                  pl.BlockSpec((MLA_A_COLS, k), lambda i: (0, 0)),
                  pl.BlockSpec((1, Q_LORA_RANK), lambda i: (0, 0)),
                  pl.BlockSpec((1, KV_LORA_RANK), lambda i: (0, 0)),
                  pl.BlockSpec((tm, 3 * LANES), lambda i: (i, 0))],
        out_specs=(pl.BlockSpec((tm, Q_LORA_RANK), lambda i: (i, 0)),
                   pl.BlockSpec((tm, KV_LORA_RANK), lambda i: (i, 0)),
                   pl.BlockSpec((tm, LANES), lambda i: (i, 0))),
        compiler_params=_params(("parallel",)),
        name="mla_a_proj",
    )(h, w, gq.reshape(1, -1), gkv.reshape(1, -1), k_tab)


def _mla_q_kernel(qn_ref, w_ref, tab_ref, q_ref, *, q_scale):
    qn = qn_ref[...]
    tab = tab_ref[...]
    for hd in range(MLA_HEADS):
        lo = hd * QK_PAD_DIM
        acc = _dot(qn, w_ref[:, lo:lo + QK_PAD_DIM]) * q_scale
        q_ref[:, lo:lo + LANES] = acc[:, :LANES].astype(BF16)
        q_ref[:, lo + LANES:lo + QK_PAD_DIM] = _rope_lanes(acc[:, LANES:], tab).astype(BF16)


def _mla_q_proj(qn, w, rope_tab, q_scale, tm=1024):
    s, k = qn.shape
    n = w.shape[1]
    return pl.pallas_call(
        functools.partial(_mla_q_kernel, q_scale=q_scale),
        out_shape=jax.ShapeDtypeStruct((s, n), BF16),
        grid=(s // tm,),
        in_specs=[pl.BlockSpec((tm, k), lambda i: (i, 0)),
                  pl.BlockSpec((k, n), lambda i: (0, 0)),
                  pl.BlockSpec((tm, 3 * LANES), lambda i: (i, 0))],
        out_specs=pl.BlockSpec((tm, n), lambda i: (i, 0)),
        compiler_params=_params(("parallel",)),
        name="mla_q_proj",
    )(qn, w, rope_tab)


def _mla_attn_kernel(q_ref, k_ref, v_ref, kpe_ref, sg_ref, o_ref, kcat_ref, vcat_ref, *, tk, sub):
    @pl.when(pl.program_id(1) == 0)
    def _():
        kcat_ref[:, :LANES] = k_ref[...]
        kcat_ref[:, LANES:] = kpe_ref[...]
        vcat_ref[:, :V_HEAD_DIM] = v_ref[...]
        vcat_ref[:, V_HEAD_DIM:] = jnp.ones((v_ref.shape[0], LANES), BF16)

    n_kv = k_ref.shape[0] // tk
    n_sub = q_ref.shape[0] // sub
    qs = [q_ref[c * sub:(c + 1) * sub, :] for c in range(n_sub)]
    ms = [jnp.full((sub, 1), -jnp.inf, F32) for _ in range(n_sub)]
    accs = [jnp.zeros((sub, V_HEAD_DIM + LANES), F32) for _ in range(n_sub)]
    for j in range(n_kv):
        kblk = kcat_ref[j * tk:(j + 1) * tk, :]
        vblk = vcat_ref[j * tk:(j + 1) * tk, :]
        for c in range(n_sub):
            s = _dot_nt(qs[c], kblk)
            m_new = jnp.maximum(ms[c], jnp.max(s, axis=1, keepdims=True))
            alpha = jnp.exp2(ms[c] - m_new)
            p = jnp.exp2(s - m_new).astype(BF16)
            accs[c] = alpha * accs[c] + _dot(p, vblk)
            ms[c] = m_new
    for c in range(n_sub):
        rows = slice(c * sub, (c + 1) * sub)
        attn = accs[c][:, :V_HEAD_DIM] / accs[c][:, V_HEAD_DIM:]
        o_ref[rows, :] = (attn * sg_ref[rows, :].astype(F32)).astype(o_ref.dtype)


def _mla_attention(q, kv, kpe, gates, tq=2048, tk=1024, sub=512):
    s = q.shape[0]
    return pl.pallas_call(
        functools.partial(_mla_attn_kernel, tk=tk, sub=sub),
        out_shape=jax.ShapeDtypeStruct((s, MLA_WIDTH), BF16),
        grid=(MLA_HEADS, s // tq),
        in_specs=[pl.BlockSpec((tq, QK_PAD_DIM), lambda h, i: (i, h)),
                  pl.BlockSpec((s, QK_NOPE_DIM), lambda h, i: (0, h)),
                  pl.BlockSpec((s, V_HEAD_DIM), lambda h, i: (0, MLA_HEADS + h)),
                  pl.BlockSpec((s, LANES), lambda h, i: (0, 0)),
                  pl.BlockSpec((tq, V_HEAD_DIM), lambda h, i: (i, h))],
        out_specs=pl.BlockSpec((tq, V_HEAD_DIM), lambda h, i: (i, h)),
        scratch_shapes=[pltpu.VMEM((s, QK_PAD_DIM), BF16),
                        pltpu.VMEM((s, V_HEAD_DIM + LANES), BF16)],
        compiler_params=_params(("parallel", "arbitrary")),
        name="mla_attention",
    )(q, kv, kv, kpe, gates)


def _band_bucket_index(dil, tq):
    nb = REL_BUCKETS // 2
    max_exact = nb // 2
    rel = (np.arange(tq + 2 * BAND_HALF)[None, :] - BAND_HALF - np.arange(tq)[:, None]) * dil
    n = np.abs(rel)
    nf = np.maximum(n, 1).astype(np.float64)
    large = max_exact + (np.log(nf / max_exact) / math.log(REL_MAX_DISTANCE / max_exact)
                         * (nb - max_exact)).astype(np.int32)
    large = np.minimum(large, nb - 1)
    return (np.where(rel > 0, nb, 0) + np.where(n < max_exact, n, large)).astype(np.int32)


def _band_bias_kernel(rb_ref, idx_ref, o_ref):
    hd = pl.program_id(0)
    idx = idx_ref[0]
    acc = jnp.zeros(idx.shape, F32)
    for b in range(REL_BUCKETS):
        acc = jnp.where(idx == b, rb_ref[b, hd], acc)
    o_ref[0] = acc


def _band_bias(rel_bias, tq):
    idx = jnp.asarray(np.stack([_band_bucket_index(dil, tq) for dil in DILATIONS]))
    tw = idx.shape[2]
    return pl.pallas_call(
        _band_bias_kernel,
        out_shape=jax.ShapeDtypeStruct((DSWA_HEADS, tq, tw), F32),
        grid=(DSWA_HEADS,),
        in_specs=[pl.BlockSpec(memory_space=pltpu.SMEM),
                  pl.BlockSpec((1, tq, tw), lambda h: (h // DSWA_HEADS_PER_GROUP, 0, 0))],
        out_specs=pl.BlockSpec((1, tq, tw), lambda h: (h, 0, 0)),
        compiler_params=_params(("parallel",)),
        name="band_bias",
    )(rel_bias, idx)


def _band_attn_kernel(q_ref, kp_ref, ko_ref, kn_ref, vp_ref, vo_ref, vn_ref, bias_ref,
                      o_ref, lse_ref):
    tq = q_ref.shape[0]
    tw = tq + 2 * BAND_HALF
    tile = pl.program_id(1)
    prev_ok = tile != 0
    next_ok = tile != pl.num_programs(1) - 1
    row = lax.broadcasted_iota(jnp.int32, (tq, tw), 0)
    col = lax.broadcasted_iota(jnp.int32, (tq, tw), 1)
    rel = col - BAND_HALF - row
    mask = ((jnp.abs(rel) <= BAND_HALF) & ((col >= BAND_HALF) | prev_ok)
            & ((col < tq + BAND_HALF) | next_ok))
    lane = lax.broadcasted_iota(jnp.int32, (tq, LANES), 1)
    scale = DSWA_HEAD_DIM ** -0.5
    heads = range(DSWA_HEADS_PER_GROUP)
    cols = [slice(hd * DSWA_HEAD_DIM, (hd + 1) * DSWA_HEAD_DIM) for hd in heads]
    kc = [jnp.concatenate([kp_ref[:, c], ko_ref[:, c], kn_ref[:, c]], axis=0) for c in cols]
    s = [_dot_nt(q_ref[:, cols[hd]], kc[hd]) * scale + bias_ref[hd] for hd in heads]
    s = [jnp.where(mask, sh, NEG_INF) for sh in s]
    m = [jnp.max(sh, axis=1, keepdims=True) for sh in s]
    p = [jnp.exp(sh - mh).astype(BF16) for sh, mh in zip(s, m)]
    ones = jnp.ones((tw, LANES), BF16)
    vc = [jnp.concatenate(
        [jnp.concatenate([vp_ref[:, c], vo_ref[:, c], vn_ref[:, c]], axis=0), ones], axis=1)
        for c in cols]
    pv = [_dot(p[hd], vc[hd]) for hd in heads]
    lse_tile = jnp.zeros((tq, LANES), F32)
    for hd in heads:
        den = pv[hd][:, DSWA_HEAD_DIM:]
        o_ref[:, cols[hd]] = (pv[hd][:, :DSWA_HEAD_DIM] / den).astype(o_ref.dtype)
        lse_tile = jnp.where(lane == hd, m[hd] + jnp.log(den), lse_tile)
    lse_ref[...] = lse_tile


def _band_attention(qkv, bias, group, dil):
    rows = qkv.shape[0]
    tq = bias.shape[1]
    nt = rows // tq
    per_tile = tq // BAND_HALF
    n_edge = rows // BAND_HALF
    tw = tq + 2 * BAND_HALF

    def own(part):
        return pl.BlockSpec((tq, DSWA_WIDTH), lambda r, n: (n, 3 * r + part))

    def prev(part):
        return pl.BlockSpec((BAND_HALF, DSWA_WIDTH),
                            lambda r, n: (jnp.maximum(n * per_tile - 1, 0), 3 * r + part))

    def nxt(part):
        return pl.BlockSpec((BAND_HALF, DSWA_WIDTH),
                            lambda r, n: (jnp.minimum((n + 1) * per_tile, n_edge - 1), 3 * r + part))

    return pl.pallas_call(
        _band_attn_kernel,
        out_shape=(jax.ShapeDtypeStruct((rows, dil * DSWA_WIDTH), BF16),
                   jax.ShapeDtypeStruct((rows, dil * LANES), F32)),
        grid=(dil, nt),
        in_specs=[own(0),
                  prev(1), own(1), nxt(1),
                  prev(2), own(2), nxt(2),
                  pl.BlockSpec((DSWA_HEADS_PER_GROUP, tq, tw), lambda r, n: (group, 0, 0))],
        out_specs=(pl.BlockSpec((tq, DSWA_WIDTH), lambda r, n: (n, r)),
                   pl.BlockSpec((tq, LANES), lambda r, n: (n, r))),
        compiler_params=_params(("parallel", "parallel")),
        name=f"band_attention_d{dil}",
    )(qkv, qkv, qkv, qkv, qkv, qkv, qkv, bias)


def _mix_kernel(o1_ref, o4_ref, o16_ref, l1_ref, l4_ref, l16_ref, sg_ref, out_ref,
                o4n_ref, l4n_ref, o16n_ref, l16n_ref):
    for dil, o_ref, l_ref, on_ref, ln_ref in ((DILATIONS[1], o4_ref, l4_ref, o4n_ref, l4n_ref),
                                               (DILATIONS[2], o16_ref, l16_ref, o16n_ref, l16n_ref)):
        rows = o_ref.shape[0]
        for r in range(dil):
            for hd in range(DSWA_HEADS_PER_GROUP):
                lo = r * DSWA_WIDTH + hd * DSWA_HEAD_DIM
                on_ref[hd, pl.ds(r, rows, stride=dil), :] = (
                    o_ref[:, lo:lo + DSWA_HEAD_DIM].astype(F32))
            ln_ref[pl.ds(r, rows, stride=dil), :] = l_ref[:, r * LANES:(r + 1) * LANES]
    for hd in range(DSWA_HEADS_PER_GROUP):
        cols = slice(hd * DSWA_HEAD_DIM, (hd + 1) * DSWA_HEAD_DIM)
        l1 = l1_ref[:, hd:hd + 1]
        l2 = l4n_ref[:, hd:hd + 1]
        l3 = l16n_ref[:, hd:hd + 1]
        mx = jnp.maximum(jnp.maximum(l1, l2), l3)
        e1, e2, e3 = jnp.exp(l1 - mx), jnp.exp(l2 - mx), jnp.exp(l3 - mx)
        den = e1 + e2 + e3
        mix = ((e1 / den) * o1_ref[:, cols].astype(F32)
               + (e2 / den) * o4n_ref[hd]
               + (e3 / den) * o16n_ref[hd])
        out_ref[:, cols] = (mix * sg_ref[:, cols].astype(F32)).astype(out_ref.dtype)


def _group_mixture(outs, lses, gates, tm=256):
    o1, o4, o16 = outs
    l1, l4, l16 = lses
    s = o1.shape[0]
    d4, d16 = DILATIONS[1], DILATIONS[2]
    w = DSWA_WIDTH
    return pl.pallas_call(
        _mix_kernel,
        out_shape=jax.ShapeDtypeStruct((s, w), BF16),
        grid=(s // tm,),
        in_specs=[pl.BlockSpec((tm, w), lambda i: (i, 0)),
                  pl.BlockSpec((tm // d4, d4 * w), lambda i: (i, 0)),
                  pl.BlockSpec((tm // d16, d16 * w), lambda i: (i, 0)),
                  pl.BlockSpec((tm, LANES), lambda i: (i, 0)),
                  pl.BlockSpec((tm // d4, d4 * LANES), lambda i: (i, 0)),
                  pl.BlockSpec((tm // d16, d16 * LANES), lambda i: (i, 0)),
                  pl.BlockSpec((tm, w), lambda i: (i, MLA_WIDTH // w))],
        out_specs=pl.BlockSpec((tm, w), lambda i: (i, 0)),
        scratch_shapes=[pltpu.VMEM((DSWA_HEADS_PER_GROUP, tm, DSWA_HEAD_DIM), F32),
                        pltpu.VMEM((tm, LANES), F32),
                        pltpu.VMEM((DSWA_HEADS_PER_GROUP, tm, DSWA_HEAD_DIM), F32),
                        pltpu.VMEM((tm, LANES), F32)],
        compiler_params=_params(("parallel",)),
        name="group_mixture",
    )(o1, o4, o16, l1, l4, l16, gates)


def _merge_kernel(a_ref, b_ref, wm_ref, wd_ref, rm_ref, rd_ref, o_ref):
    y_mla = _dot(a_ref[...], wm_ref[...])
    y_dswa = _dot(b_ref[...], wd_ref[...])
    merged = rm_ref[...].astype(F32) * y_mla + rd_ref[...].astype(F32) * y_dswa
    o_ref[...] = merged.astype(o_ref.dtype)


def _merge_branches(a_gated, b_gated, gates, w_o_mla, w_o_dswa, tm=1024, tn=COL_BLOCK):
    s = a_gated.shape[0]
    d = w_o_mla.shape[1]
    r_mla_block = (MLA_WIDTH + DSWA_WIDTH) // tn
    r_dswa_block = (MLA_WIDTH + DSWA_WIDTH + D_MODEL) // tn
    return pl.pallas_call(
        _merge_kernel,
        out_shape=jax.ShapeDtypeStruct((s, d), BF16),
        grid=(d // tn, s // tm),
        in_specs=[pl.BlockSpec((tm, MLA_WIDTH), lambda j, i: (i, 0)),
                  pl.BlockSpec((tm, DSWA_WIDTH), lambda j, i: (i, 0)),
                  pl.BlockSpec((MLA_WIDTH, tn), lambda j, i: (0, j)),
                  pl.BlockSpec((DSWA_WIDTH, tn), lambda j, i: (0, j)),
                  pl.BlockSpec((tm, tn), lambda j, i: (i, r_mla_block + j)),
                  pl.BlockSpec((tm, tn), lambda j, i: (i, r_dswa_block + j))],
        out_specs=pl.BlockSpec((tm, tn), lambda j, i: (i, j)),
        compiler_params=_params(("parallel", "parallel")),
        name="merge_branches",
    )(a_gated, b_gated, w_o_mla, w_o_dswa, gates, gates)


def _final_ln_kernel(x_ref, o_ref, g0_ref, b0_ref, g_ref, b_ref, out_ref):
    h = _layer_norm_rows(x_ref[...], g0_ref[...], b0_ref[...])
    out_ref[...] = _layer_norm_rows(DEEPNORM_ALPHA * h + o_ref[...], g_ref[...], b_ref[...])


def _final_layer_norm(x, out, g0, b0, g, b, tm=256):
    s, d = x.shape
    row = pl.BlockSpec((tm, d), lambda i: (i, 0))
    vec = pl.BlockSpec((1, d), lambda i: (0, 0))
    return pl.pallas_call(
        _final_ln_kernel,
        out_shape=jax.ShapeDtypeStruct((s, d), F32),
        grid=(s // tm,),
        in_specs=[row, row, vec, vec, vec, vec],
        out_specs=row,
        compiler_params=_params(("parallel",)),
        name="final_ln",
    )(x, out, g0.reshape(1, d), b0.reshape(1, d), g.reshape(1, d), b.reshape(1, d))


def _cast_kernel(w_ref, o_ref):
    o_ref[...] = w_ref[...].astype(o_ref.dtype)


def _cast_rows_bf16(w_t, row0, n_rows, name):
    k = w_t.shape[1]
    tr = math.gcd(MLA_A_COLS, REST_COLS)
    assert row0 % tr == 0 and n_rows % tr == 0 and tr % 16 == 0
    return pl.pallas_call(
        _cast_kernel,
        out_shape=jax.ShapeDtypeStruct((n_rows, k), BF16),
        grid=(n_rows // tr,),
        in_specs=[pl.BlockSpec((tr, k), lambda j: (row0 // tr + j, 0))],
        out_specs=pl.BlockSpec((tr, k), lambda j: (j, 0)),
        compiler_params=_params(("parallel",)),
        name=name,
    )(w_t)


def _rope_tables(seq):
    pos = jnp.arange(seq, dtype=F32)
    inv_freq = 1.0 / (ROPE_THETA ** (jnp.arange(0, QK_ROPE_DIM, 2, dtype=F32) / QK_ROPE_DIM))
    ang = pos[:, None] * inv_freq[None, :]
    cos, sin = jnp.cos(ang), jnp.sin(ang)
    z32 = jnp.zeros_like(cos)
    z64 = jnp.zeros((seq, LANES - QK_ROPE_DIM), F32)
    return jnp.concatenate([cos, cos, z64, z32, sin, z64, -sin, z32, z64], axis=1)


def kernel(x, emb_ln_g, emb_ln_b, rel_bias, w_in, q_a_norm_g, w_q_b, kv_a_norm_g, w_kv_b,
           w_o_mla, w_o_dswa, w_out, ln_g, ln_b):
    assert DEPTH == 1 and x.shape == (1, SEQ, D_MODEL)
    x2 = x[0]

    w_in_t = jnp.swapaxes(w_in, 1, 2)[0]
    w_mla_a = _cast_rows_bf16(w_in_t, 0, MLA_A_COLS, "mla_a_weights_bf16")
    w_rest = _cast_rows_bf16(w_in_t, MLA_A_COLS, REST_COLS, "rest_weights_bf16")
    w_q = jnp.pad(w_q_b[0].reshape(Q_LORA_RANK, MLA_HEADS, QK_HEAD_DIM),
                  ((0, 0), (0, 0), (0, QK_PAD_DIM - QK_HEAD_DIM))
                  ).reshape(Q_LORA_RANK, MLA_HEADS * QK_PAD_DIM).astype(BF16)
    w_kv = (w_kv_b[0].reshape(KV_LORA_RANK, MLA_HEADS, 2, QK_NOPE_DIM)
            .transpose(0, 2, 1, 3).reshape(KV_LORA_RANK, 2 * MLA_WIDTH).astype(BF16))
    w_om = w_o_mla[0].astype(BF16)
    w_od = w_o_dswa[0].astype(BF16)
    w_o = w_out[0].astype(BF16)

    q_scale = QK_HEAD_DIM ** -0.5 * math.log2(math.e)
    rope_tab = _rope_tables(SEQ)

    h_slabs = _input_layer_norm(x2, emb_ln_g, emb_ln_b)
    h = h_slabs[0]

    gates = _proj(h, w_rest, n_out=GATE_WIDTH, w_block=lambda j: GATE_COL0 + j, gated=True,
                  n_silu_blocks=(MLA_WIDTH + DSWA_WIDTH) // COL_BLOCK, w_transposed=True,
                  name="gate_proj")

    qn, ckvn, kpe = _mla_a_proj(h, w_mla_a, q_a_norm_g[0], kv_a_norm_g[0], rope_tab)
    q = _mla_q_proj(qn, w_q, rope_tab, q_scale)
    kv = _proj(ckvn, w_kv, n_out=2 * MLA_WIDTH, w_block=lambda j: j, name="mla_kv_proj")
    a_gated = _mla_attention(q, kv, kpe, gates)

    bias = _band_bias(rel_bias, BAND_TILE)
    outs, lses = [], []
    blocks_per_part = DSWA_QKV_WIDTH // COL_BLOCK
    for g, dil in enumerate(DILATIONS):
        qkv = _proj(h_slabs[g], w_rest, n_out=3 * DSWA_WIDTH, dil=dil, w_transposed=True,
                    w_block=lambda j, g=g: j * blocks_per_part + g, name=f"dswa_proj_d{dil}")
        o, lse = _band_attention(qkv, bias, g, dil)
        outs.append(o)
        lses.append(lse)
    b_gated = _group_mixture(outs, lses, gates)

    merged = _merge_branches(a_gated, b_gated, gates, w_om, w_od)
    out = _proj(merged, w_o, n_out=D_MODEL, w_block=lambda j: j, out_dtype=F32, name="out_proj")
    y = _final_layer_norm(x2, out, emb_ln_g, emb_ln_b, ln_g[0], ln_b[0])
    return y[None]
```

```python
import functools
import math

import jax
import jax.numpy as jnp
import numpy as np
from jax import lax
from jax.experimental import pallas as pl
from jax.experimental.pallas import tpu as pltpu

D_MODEL = 4096
SEQ = 8192
DEPTH = 1

MLA_HEADS = 16
Q_LORA_RANK = 1024
KV_LORA_RANK = 512
QK_NOPE_DIM = 128
QK_ROPE_DIM = 64
QK_HEAD_DIM = QK_NOPE_DIM + QK_ROPE_DIM
V_HEAD_DIM = 128
MLA_WIDTH = MLA_HEADS * V_HEAD_DIM
ROPE_THETA = 10000.0

DIL_GROUPS = ((128, 1), (512, 4), (2048, 16))
DILATIONS = tuple(d for _, d in DIL_GROUPS)
DSWA_HEADS_PER_GROUP = 8
DSWA_HEADS = 24
DSWA_HEAD_DIM = 128
DSWA_QKV_WIDTH = DSWA_HEADS * DSWA_HEAD_DIM
DSWA_WIDTH = DSWA_HEADS_PER_GROUP * DSWA_HEAD_DIM
BAND_HALF = 64
BAND_TILE = 256

REL_BUCKETS = 32
REL_MAX_DISTANCE = 1024

DEEPNORM_ALPHA = (2.0 * DEPTH) ** 0.25
LN_EPS = 1e-5
RMS_EPS = 1e-6
NEG_INF = -1e30

IN_SPLITS = (Q_LORA_RANK, KV_LORA_RANK + QK_ROPE_DIM, DSWA_QKV_WIDTH, DSWA_QKV_WIDTH,
             DSWA_QKV_WIDTH, MLA_WIDTH, DSWA_WIDTH, D_MODEL, D_MODEL)
IN_OFFSETS = tuple(int(v) for v in np.cumsum(IN_SPLITS)[:-1])
MLA_A_COLS = IN_OFFSETS[1]
REST_COLS = sum(IN_SPLITS) - MLA_A_COLS

LANES = 128
QK_PAD_DIM = 256
COL_BLOCK = 1024
GATE_COL0 = 3 * DSWA_QKV_WIDTH // COL_BLOCK
GATE_WIDTH = MLA_WIDTH + DSWA_WIDTH + 2 * D_MODEL
VMEM_LIMIT = 56 * 1024 * 1024

BF16 = jnp.bfloat16
F32 = jnp.float32


def _params(sem):
    return pltpu.CompilerParams(dimension_semantics=sem, vmem_limit_bytes=VMEM_LIMIT)


def _dot(a, b):
    return jnp.dot(a, b, preferred_element_type=F32)


def _dot_nt(a, b):
    return lax.dot_general(a, b, (((1,), (1,)), ((), ())), preferred_element_type=F32)


def _sigmoid(x):
    return 0.5 * jnp.tanh(0.5 * x) + 0.5


def _layer_norm_rows(x, g, b):
    mu = jnp.mean(x, axis=-1, keepdims=True)
    xc = x - mu
    var = jnp.mean(xc * xc, axis=-1, keepdims=True)
    return xc * lax.rsqrt(var + LN_EPS) * g + b


def _rms_norm_rows(x, g):
    return x * lax.rsqrt(jnp.mean(x * x, axis=-1, keepdims=True) + RMS_EPS) * g


def _rope_lanes(t, tab):
    c, s1, s2 = tab[:, :LANES], tab[:, LANES:2 * LANES], tab[:, 2 * LANES:]
    return t * c + pltpu.roll(t, 32, 1) * s1 + pltpu.roll(t, LANES - 32, 1) * s2


def _class_gather_matrix(tm, dil):
    per_class = tm // dil
    i = np.arange(tm)
    p = np.zeros((tm, tm), np.float32)
    p[i, (i % per_class) * dil + i // per_class] = 1.0
    return jnp.asarray(p, BF16)


def _ln_kernel(x_ref, g_ref, b_ref, p4_ref, p16_ref, h1_ref, h4_ref, h16_ref):
    d = x_ref.shape[1]
    y = _layer_norm_rows(x_ref[...], g_ref[...], b_ref[...]).astype(BF16)
    h1_ref[...] = y
    for dil, p_ref, slab_ref in ((DILATIONS[1], p4_ref, h4_ref), (DILATIONS[2], p16_ref, h16_ref)):
        rows = slab_ref.shape[0]
        grouped = _dot(p_ref[...], y).astype(BF16)
        for r in range(dil):
            slab_ref[:, r * d:(r + 1) * d] = grouped[r * rows:(r + 1) * rows, :]


def _input_layer_norm(x, g, b, tm=256):
    s, d = x.shape
    d4, d16 = DILATIONS[1], DILATIONS[2]
    return pl.pallas_call(
        _ln_kernel,
        out_shape=(jax.ShapeDtypeStruct((s, d), BF16),
                   jax.ShapeDtypeStruct((s // d4, d4 * d), BF16),
                   jax.ShapeDtypeStruct((s // d16, d16 * d), BF16)),
        grid=(s // tm,),
        in_specs=[pl.BlockSpec((tm, d), lambda i: (i, 0)),
                  pl.BlockSpec((1, d), lambda i: (0, 0)),
                  pl.BlockSpec((1, d), lambda i: (0, 0)),
                  pl.BlockSpec((tm, tm), lambda i: (0, 0)),
                  pl.BlockSpec((tm, tm), lambda i: (0, 0))],
        out_specs=(pl.BlockSpec((tm, d), lambda i: (i, 0)),
                   pl.BlockSpec((tm // d4, d4 * d), lambda i: (i, 0)),
                   pl.BlockSpec((tm // d16, d16 * d), lambda i: (i, 0))),
        compiler_params=_params(("parallel",)),
        name="input_ln",
    )(x, g.reshape(1, d), b.reshape(1, d),
      _class_gather_matrix(tm, d4), _class_gather_matrix(tm, d16))


def _proj_kernel(a_ref, w_ref, o_ref, *, n_silu_blocks, gated, w_transposed):
    acc = (_dot_nt if w_transposed else _dot)(a_ref[...], w_ref[...])
    if gated:
        sig = _sigmoid(acc)
        acc = jnp.where(pl.program_id(0) < n_silu_blocks, acc * sig, sig)
    o_ref[...] = acc.astype(o_ref.dtype)


def _proj(a, w, *, n_out, w_block, dil=1, tm=1024, tn=COL_BLOCK, gated=False, n_silu_blocks=0,
          w_transposed=False, out_dtype=BF16, name):
    rows, kd = a.shape
    k = kd // dil
    nj = n_out // tn
    tm = min(tm, rows)
    kern = functools.partial(_proj_kernel, n_silu_blocks=n_silu_blocks, gated=gated,
                             w_transposed=w_transposed)
    if w_transposed:
        w_spec = pl.BlockSpec((tn, k), lambda j, r, i: (w_block(j), 0))
    else:
        w_spec = pl.BlockSpec((k, tn), lambda j, r, i: (0, w_block(j)))
    return pl.pallas_call(
        kern,
        out_shape=jax.ShapeDtypeStruct((rows, dil * n_out), out_dtype),
        grid=(nj, dil, rows // tm),
        in_specs=[pl.BlockSpec((tm, k), lambda j, r, i: (i, r)), w_spec],
        out_specs=pl.BlockSpec((tm, tn), lambda j, r, i: (i, r * nj + j)),
        compiler_params=_params(("parallel", "parallel", "parallel")),
        name=name,
    )(a, w)


def _mla_a_kernel(h_ref, w_ref, gq_ref, gkv_ref, tab_ref, qn_ref, ckvn_ref, kpe_ref):
    acc = _dot_nt(h_ref[...], w_ref[...])
    qn_ref[...] = _rms_norm_rows(acc[:, :Q_LORA_RANK], gq_ref[...]).astype(BF16)
    ckv = acc[:, Q_LORA_RANK:Q_LORA_RANK + KV_LORA_RANK]
    ckvn_ref[...] = _rms_norm_rows(ckv, gkv_ref[...]).astype(BF16)
    k_pe = acc[:, Q_LORA_RANK + KV_LORA_RANK:]
    k_pe = jnp.concatenate([k_pe, jnp.zeros((k_pe.shape[0], LANES - QK_ROPE_DIM), F32)], axis=1)
    kpe_ref[...] = _rope_lanes(k_pe, tab_ref[...]).astype(BF16)


def _mla_a_proj(h, w, gq, gkv, k_tab, tm=512):
    s, k = h.shape
    return pl.pallas_call(
        _mla_a_kernel,
        out_shape=(jax.ShapeDtypeStruct((s, Q_LORA_RANK), BF16),
                   jax.ShapeDtypeStruct((s, KV_LORA_RANK), BF16),
                   jax.ShapeDtypeStruct((s, LANES), BF16)),
        grid=(s // tm,),
        in_specs=[pl.BlockSpec((tm, k), lambda i: (i, 0)),
                  pl.BlockSpec((MLA_A_COLS, k), lambda i: (0, 0)),
                  pl.BlockSpec((1, Q_LORA_RANK), lambda i: (0, 0)),
                  pl.BlockSpec((1, KV_LORA_RANK), lambda i: (0, 0)),
                  pl.BlockSpec((tm, 3 * LANES), lambda i: (i, 0))],
        out_specs=(pl.BlockSpec((tm, Q_LORA_RANK), lambda i: (i, 0)),
                   pl.BlockSpec((tm, KV_LORA_RANK), lambda i: (i, 0)),
                   pl.BlockSpec((tm, LANES), lambda i: (i, 0))),
        compiler_params=_params(("parallel",)),
        name="mla_a_proj",
    )(h, w, gq.reshape(1, -1), gkv.reshape(1, -1), k_tab)


def _mla_q_kernel(qn_ref, w_ref, tab_ref, q_ref, *, q_scale):
    qn = qn_ref[...]
    tab = tab_ref[...]
    for hd in range(MLA_HEADS):
        lo = hd * QK_PAD_DIM
        acc = _dot(qn, w_ref[:, lo:lo + QK_PAD_DIM]) * q_scale
        q_ref[:, lo:lo + LANES] = acc[:, :LANES].astype(BF16)
        q_ref[:, lo + LANES:lo + QK_PAD_DIM] = _rope_lanes(acc[:, LANES:], tab).astype(BF16)


def _mla_q_proj(qn, w, rope_tab, q_scale, tm=1024):
    s, k = qn.shape
    n = w.shape[1]
    return pl.pallas_call(
        functools.partial(_mla_q_kernel, q_scale=q_scale),
        out_shape=jax.ShapeDtypeStruct((s, n), BF16),
        grid=(s // tm,),
        in_specs=[pl.BlockSpec((tm, k), lambda i: (i, 0)),
                  pl.BlockSpec((k, n), lambda i: (0, 0)),
                  pl.BlockSpec((tm, 3 * LANES), lambda i: (i, 0))],
        out_specs=pl.BlockSpec((tm, n), lambda i: (i, 0)),
        compiler_params=_params(("parallel",)),
        name="mla_q_proj",
    )(qn, w, rope_tab)


def _mla_attn_kernel(q_ref, k_ref, v_ref, kpe_ref, sg_ref, o_ref, kcat_ref, vcat_ref, *, tk, sub):
    @pl.when(pl.program_id(1) == 0)
    def _():
        kcat_ref[:, :LANES] = k_ref[...]
        kcat_ref[:, LANES:] = kpe_ref[...]
        vcat_ref[:, :V_HEAD_DIM] = v_ref[...]
        vcat_ref[:, V_HEAD_DIM:] = jnp.ones((v_ref.shape[0], LANES), BF16)

    n_kv = k_ref.shape[0] // tk
    n_sub = q_ref.shape[0] // sub
    qs = [q_ref[c * sub:(c + 1) * sub, :] for c in range(n_sub)]
    ms = [jnp.full((sub, 1), -jnp.inf, F32) for _ in range(n_sub)]
    accs = [jnp.zeros((sub, V_HEAD_DIM + LANES), F32) for _ in range(n_sub)]
    for j in range(n_kv):
        kblk = kcat_ref[j * tk:(j + 1) * tk, :]
        vblk = vcat_ref[j * tk:(j + 1) * tk, :]
        for c in range(n_sub):
            s = _dot_nt(qs[c], kblk)
            m_new = jnp.maximum(ms[c], jnp.max(s, axis=1, keepdims=True))
            alpha = jnp.exp2(ms[c] - m_new)
            p = jnp.exp2(s - m_new).astype(BF16)
            accs[c] = alpha * accs[c] + _dot(p, vblk)
            ms[c] = m_new
    for c in range(n_sub):
        rows = slice(c * sub, (c + 1) * sub)
        attn = accs[c][:, :V_HEAD_DIM] / accs[c][:, V_HEAD_DIM:]
        o_ref[rows, :] = (attn * sg_ref[rows, :].astype(F32)).astype(o_ref.dtype)


def _mla_attention(q, kv, kpe, gates, tq=2048, tk=1024, sub=512):
    s = q.shape[0]
    return pl.pallas_call(
        functools.partial(_mla_attn_kernel, tk=tk, sub=sub),
        out_shape=jax.ShapeDtypeStruct((s, MLA_WIDTH), BF16),
        grid=(MLA_HEADS, s // tq),
        in_specs=[pl.BlockSpec((tq, QK_PAD_DIM), lambda h, i: (i, h)),
                  pl.BlockSpec((s, QK_NOPE_DIM), lambda h, i: (0, h)),
                  pl.BlockSpec((s, V_HEAD_DIM), lambda h, i: (0, MLA_HEADS + h)),
                  pl.BlockSpec((s, LANES), lambda h, i: (0, 0)),
                  pl.BlockSpec((tq, V_HEAD_DIM), lambda h, i: (i, h))],
        out_specs=pl.BlockSpec((tq, V_HEAD_DIM), lambda h, i: (i, h)),
        scratch_shapes=[pltpu.VMEM((s, QK_PAD_DIM), BF16),
                        pltpu.VMEM((s, V_HEAD_DIM + LANES), BF16)],
        compiler_params=_params(("parallel", "arbitrary")),
        name="mla_attention",
    )(q, kv, kv, kpe, gates)


def _band_bucket_index(dil, tq):
    nb = REL_BUCKETS // 2
    max_exact = nb // 2
    rel = (np.arange(tq + 2 * BAND_HALF)[None, :] - BAND_HALF - np.arange(tq)[:, None]) * dil
    n = np.abs(rel)
    nf = np.maximum(n, 1).astype(np.float64)
    large = max_exact + (np.log(nf / max_exact) / math.log(REL_MAX_DISTANCE / max_exact)
                         * (nb - max_exact)).astype(np.int32)
    large = np.minimum(large, nb - 1)
    return (np.where(rel > 0, nb, 0) + np.where(n < max_exact, n, large)).astype(np.int32)


def _band_bias_kernel(rb_ref, idx_ref, o_ref):
    hd = pl.program_id(0)
    idx = idx_ref[0]
    acc = jnp.zeros(idx.shape, F32)
    for b in range(REL_BUCKETS):
        acc = jnp.where(idx == b, rb_ref[b, hd], acc)
    o_ref[0] = acc


def _band_bias(rel_bias, tq):
    idx = jnp.asarray(np.stack([_band_bucket_index(dil, tq) for dil in DILATIONS]))
    tw = idx.shape[2]
    return pl.pallas_call(
        _band_bias_kernel,
        out_shape=jax.ShapeDtypeStruct((DSWA_HEADS, tq, tw), F32),
        grid=(DSWA_HEADS,),
        in_specs=[pl.BlockSpec(memory_space=pltpu.SMEM),
                  pl.BlockSpec((1, tq, tw), lambda h: (h // DSWA_HEADS_PER_GROUP, 0, 0))],
        out_specs=pl.BlockSpec((1, tq, tw), lambda h: (h, 0, 0)),
        compiler_params=_params(("parallel",)),
        name="band_bias",
    )(rel_bias, idx)


def _band_attn_kernel(q_ref, kp_ref, ko_ref, kn_ref, vp_ref, vo_ref, vn_ref, bias_ref,
                      o_ref, lse_ref):
    tq = q_ref.shape[0]
    tw = tq + 2 * BAND_HALF
    tile = pl.program_id(1)
    prev_ok = tile != 0
    next_ok = tile != pl.num_programs(1) - 1
    row = lax.broadcasted_iota(jnp.int32, (tq, tw), 0)
    col = lax.broadcasted_iota(jnp.int32, (tq, tw), 1)
    rel = col - BAND_HALF - row
    mask = ((jnp.abs(rel) <= BAND_HALF) & ((col >= BAND_HALF) | prev_ok)
            & ((col < tq + BAND_HALF) | next_ok))
    lane = lax.broadcasted_iota(jnp.int32, (tq, LANES), 1)
    scale = DSWA_HEAD_DIM ** -0.5
    heads = range(DSWA_HEADS_PER_GROUP)
    cols = [slice(hd * DSWA_HEAD_DIM, (hd + 1) * DSWA_HEAD_DIM) for hd in heads]
    kc = [jnp.concatenate([kp_ref[:, c], ko_ref[:, c], kn_ref[:, c]], axis=0) for c in cols]
    s = [_dot_nt(q_ref[:, cols[hd]], kc[hd]) * scale + bias_ref[hd] for hd in heads]
    s = [jnp.where(mask, sh, NEG_INF) for sh in s]
    m = [jnp.max(sh, axis=1, keepdims=True) for sh in s]
    p = [jnp.exp(sh - mh).astype(BF16) for sh, mh in zip(s, m)]
    ones = jnp.ones((tw, LANES), BF16)
    vc = [jnp.concatenate(
        [jnp.concatenate([vp_ref[:, c], vo_ref[:, c], vn_ref[:, c]], axis=0), ones], axis=1)
        for c in cols]
    pv = [_dot(p[hd], vc[hd]) for hd in heads]
    lse_tile = jnp.zeros((tq, LANES), F32)
    for hd in heads:
        den = pv[hd][:, DSWA_HEAD_DIM:]
        o_ref[:, cols[hd]] = (pv[hd][:, :DSWA_HEAD_DIM] / den).astype(o_ref.dtype)
        lse_tile = jnp.where(lane == hd, m[hd] + jnp.log(den), lse_tile)
    lse_ref[...] = lse_tile


def _band_attention(qkv, bias, group, dil):
    rows = qkv.shape[0]
    tq = bias.shape[1]
    nt = rows // tq
    per_tile = tq // BAND_HALF
    n_edge = rows // BAND_HALF
    tw = tq + 2 * BAND_HALF

    def own(part):
        return pl.BlockSpec((tq, DSWA_WIDTH), lambda r, n: (n, 3 * r + part))

    def prev(part):
        return pl.BlockSpec((BAND_HALF, DSWA_WIDTH),
                            lambda r, n: (jnp.maximum(n * per_tile - 1, 0), 3 * r + part))

    def nxt(part):
        return pl.BlockSpec((BAND_HALF, DSWA_WIDTH),
                            lambda r, n: (jnp.minimum((n + 1) * per_tile, n_edge - 1), 3 * r + part))

    return pl.pallas_call(
        _band_attn_kernel,
        out_shape=(jax.ShapeDtypeStruct((rows, dil * DSWA_WIDTH), BF16),
                   jax.ShapeDtypeStruct((rows, dil * LANES), F32)),
        grid=(dil, nt),
        in_specs=[own(0),
                  prev(1), own(1), nxt(1),
                  prev(2), own(2), nxt(2),
                  pl.BlockSpec((DSWA_HEADS_PER_GROUP, tq, tw), lambda r, n: (group, 0, 0))],
        out_specs=(pl.BlockSpec((tq, DSWA_WIDTH), lambda r, n: (n, r)),
                   pl.BlockSpec((tq, LANES), lambda r, n: (n, r))),
        compiler_params=_params(("parallel", "parallel")),
        name=f"band_attention_d{dil}",
    )(qkv, qkv, qkv, qkv, qkv, qkv, qkv, bias)


def _mix_kernel(o1_ref, o4_ref, o16_ref, l1_ref, l4_ref, l16_ref, sg_ref, out_ref,
                o4n_ref, l4n_ref, o16n_ref, l16n_ref):
    for dil, o_ref, l_ref, on_ref, ln_ref in ((DILATIONS[1], o4_ref, l4_ref, o4n_ref, l4n_ref),
                                               (DILATIONS[2], o16_ref, l16_ref, o16n_ref, l16n_ref)):
        rows = o_ref.shape[0]
        for r in range(dil):
            for hd in range(DSWA_HEADS_PER_GROUP):
                lo = r * DSWA_WIDTH + hd * DSWA_HEAD_DIM
                on_ref[hd, pl.ds(r, rows, stride=dil), :] = (
                    o_ref[:, lo:lo + DSWA_HEAD_DIM].astype(F32))
            ln_ref[pl.ds(r, rows, stride=dil), :] = l_ref[:, r * LANES:(r + 1) * LANES]
    for hd in range(DSWA_HEADS_PER_GROUP):
        cols = slice(hd * DSWA_HEAD_DIM, (hd + 1) * DSWA_HEAD_DIM)
        l1 = l1_ref[:, hd:hd + 1]
        l2 = l4n_ref[:, hd:hd + 1]
        l3 = l16n_ref[:, hd:hd + 1]
        mx = jnp.maximum(jnp.maximum(l1, l2), l3)
        e1, e2, e3 = jnp.exp(l1 - mx), jnp.exp(l2 - mx), jnp.exp(l3 - mx)
        den = e1 + e2 + e3
        mix = ((e1 / den) * o1_ref[:, cols].astype(F32)
               + (e2 / den) * o4n_ref[hd]
               + (e3 / den) * o16n_ref[hd])
        out_ref[:, cols] = (mix * sg_ref[:, cols].astype(F32)).astype(out_ref.dtype)


def _group_mixture(outs, lses, gates, tm=256):
    o1, o4, o16 = outs
    l1, l4, l16 = lses
    s = o1.shape[0]
    d4, d16 = DILATIONS[1], DILATIONS[2]
    w = DSWA_WIDTH
    return pl.pallas_call(
        _mix_kernel,
        out_shape=jax.ShapeDtypeStruct((s, w), BF16),
        grid=(s // tm,),
        in_specs=[pl.BlockSpec((tm, w), lambda i: (i, 0)),
                  pl.BlockSpec((tm // d4, d4 * w), lambda i: (i, 0)),
                  pl.BlockSpec((tm // d16, d16 * w), lambda i: (i, 0)),
                  pl.BlockSpec((tm, LANES), lambda i: (i, 0)),
                  pl.BlockSpec((tm // d4, d4 * LANES), lambda i: (i, 0)),
                  pl.BlockSpec((tm // d16, d16 * LANES), lambda i: (i, 0)),
                  pl.BlockSpec((tm, w), lambda i: (i, MLA_WIDTH // w))],
        out_specs=pl.BlockSpec((tm, w), lambda i: (i, 0)),
        scratch_shapes=[pltpu.VMEM((DSWA_HEADS_PER_GROUP, tm, DSWA_HEAD_DIM), F32),
                        pltpu.VMEM((tm, LANES), F32),
                        pltpu.VMEM((DSWA_HEADS_PER_GROUP, tm, DSWA_HEAD_DIM), F32),
                        pltpu.VMEM((tm, LANES), F32)],
        compiler_params=_params(("parallel",)),
        name="group_mixture",
    )(o1, o4, o16, l1, l4, l16, gates)


def _merge_kernel(a_ref, b_ref, wm_ref, wd_ref, rm_ref, rd_ref, o_ref):
    y_mla = _dot(a_ref[...], wm_ref[...])
    y_dswa = _dot(b_ref[...], wd_ref[...])
    merged = rm_ref[...].astype(F32) * y_mla + rd_ref[...].astype(F32) * y_dswa
    o_ref[...] = merged.astype(o_ref.dtype)


def _merge_branches(a_gated, b_gated, gates, w_o_mla, w_o_dswa, tm=1024, tn=COL_BLOCK):
    s = a_gated.shape[0]
    d = w_o_mla.shape[1]
    r_mla_block = (MLA_WIDTH + DSWA_WIDTH) // tn
    r_dswa_block = (MLA_WIDTH + DSWA_WIDTH + D_MODEL) // tn
    return pl.pallas_call(
        _merge_kernel,
        out_shape=jax.ShapeDtypeStruct((s, d), BF16),
        grid=(d // tn, s // tm),
        in_specs=[pl.BlockSpec((tm, MLA_WIDTH), lambda j, i: (i, 0)),
                  pl.BlockSpec((tm, DSWA_WIDTH), lambda j, i: (i, 0)),
                  pl.BlockSpec((MLA_WIDTH, tn), lambda j, i: (0, j)),
                  pl.BlockSpec((DSWA_WIDTH, tn), lambda j, i: (0, j)),
                  pl.BlockSpec((tm, tn), lambda j, i: (i, r_mla_block + j)),
                  pl.BlockSpec((tm, tn), lambda j, i: (i, r_dswa_block + j))],
        out_specs=pl.BlockSpec((tm, tn), lambda j, i: (i, j)),
        compiler_params=_params(("parallel", "parallel")),
        name="merge_branches",
    )(a_gated, b_gated, w_o_mla, w_o_dswa, gates, gates)


def _final_ln_kernel(x_ref, o_ref, g0_ref, b0_ref, g_ref, b_ref, out_ref):
    h = _layer_norm_rows(x_ref[...], g0_ref[...], b0_ref[...])
    out_ref[...] = _layer_norm_rows(DEEPNORM_ALPHA * h + o_ref[...], g_ref[...], b_ref[...])


def _final_layer_norm(x, out, g0, b0, g, b, tm=256):
    s, d = x.shape
    row = pl.BlockSpec((tm, d), lambda i: (i, 0))
    vec = pl.BlockSpec((1, d), lambda i: (0, 0))
    return pl.pallas_call(
        _final_ln_kernel,
        out_shape=jax.ShapeDtypeStruct((s, d), F32),
        grid=(s // tm,),
        in_specs=[row, row, vec, vec, vec, vec],
        out_specs=row,
        compiler_params=_params(("parallel",)),
        name="final_ln",
    )(x, out, g0.reshape(1, d), b0.reshape(1, d), g.reshape(1, d), b.reshape(1, d))


def _cast_kernel(w_ref, o_ref):
    o_ref[...] = w_ref[...].astype(o_ref.dtype)


def _cast_rows_bf16(w_t, row0, n_rows, name):
    k = w_t.shape[1]
    tr = math.gcd(MLA_A_COLS, REST_COLS)
    assert row0 % tr == 0 and n_rows % tr == 0 and tr % 16 == 0
    return pl.pallas_call(
        _cast_kernel,
        out_shape=jax.ShapeDtypeStruct((n_rows, k), BF16),
        grid=(n_rows // tr,),
        in_specs=[pl.BlockSpec((tr, k), lambda j: (row0 // tr + j, 0))],
        out_specs=pl.BlockSpec((tr, k), lambda j: (j, 0)),
        compiler_params=_params(("parallel",)),
        name=name,
    )(w_t)


def _rope_tables(seq):
    half = QK_ROPE_DIM // 2
    pos = jnp.arange(seq, dtype=F32)
    inv_freq = 1.0 / (ROPE_THETA ** (jnp.arange(0, QK_ROPE_DIM, 2, dtype=F32) / QK_ROPE_DIM))
    lane_freq = jnp.concatenate([inv_freq, inv_freq, jnp.zeros((LANES - QK_ROPE_DIM,), F32)])
    ang = pos[:, None] * lane_freq[None, :]
    lane = np.arange(LANES)
    on_t1 = jnp.asarray((lane < half).astype(np.float32))
    on_t2 = jnp.asarray(((lane >= half) & (lane < QK_ROPE_DIM)).astype(np.float32))
    cos, sin = jnp.cos(ang), jnp.sin(ang)
    return jnp.concatenate([cos * (on_t1 + on_t2), sin * on_t2, -sin * on_t1], axis=1)


def kernel(x, emb_ln_g, emb_ln_b, rel_bias, w_in, q_a_norm_g, w_q_b, kv_a_norm_g, w_kv_b,
           w_o_mla, w_o_dswa, w_out, ln_g, ln_b):
    assert DEPTH == 1 and x.shape == (1, SEQ, D_MODEL)
    x2 = x[0]

    w_in_t = jnp.swapaxes(w_in, 1, 2)[0]
    w_mla_a = _cast_rows_bf16(w_in_t, 0, MLA_A_COLS, "mla_a_weights_bf16")
    w_rest = _cast_rows_bf16(w_in_t, MLA_A_COLS, REST_COLS, "rest_weights_bf16")
    w_q = jnp.pad(w_q_b[0].reshape(Q_LORA_RANK, MLA_HEADS, QK_HEAD_DIM),
                  ((0, 0), (0, 0), (0, QK_PAD_DIM - QK_HEAD_DIM))
                  ).reshape(Q_LORA_RANK, MLA_HEADS * QK_PAD_DIM).astype(BF16)
    w_kv = (w_kv_b[0].reshape(KV_LORA_RANK, MLA_HEADS, 2, QK_NOPE_DIM)
            .transpose(0, 2, 1, 3).reshape(KV_LORA_RANK, 2 * MLA_WIDTH).astype(BF16))
    w_om = w_o_mla[0].astype(BF16)
    w_od = w_o_dswa[0].astype(BF16)
    w_o = w_out[0].astype(BF16)

    q_scale = QK_HEAD_DIM ** -0.5 * math.log2(math.e)
    rope_tab = _rope_tables(SEQ)

    h_slabs = _input_layer_norm(x2, emb_ln_g, emb_ln_b)
    h = h_slabs[0]

    gates = _proj(h, w_rest, n_out=GATE_WIDTH, w_block=lambda j: GATE_COL0 + j, gated=True,
                  n_silu_blocks=(MLA_WIDTH + DSWA_WIDTH) // COL_BLOCK, w_transposed=True,
                  name="gate_proj")

    qn, ckvn, kpe = _mla_a_proj(h, w_mla_a, q_a_norm_g[0], kv_a_norm_g[0], rope_tab)
    q = _mla_q_proj(qn, w_q, rope_tab, q_scale)
    kv = _proj(ckvn, w_kv, n_out=2 * MLA_WIDTH, w_block=lambda j: j, name="mla_kv_proj")
    a_gated = _mla_attention(q, kv, kpe, gates)

    bias = _band_bias(rel_bias, BAND_TILE)
    outs, lses = [], []
    blocks_per_part = DSWA_QKV_WIDTH // COL_BLOCK
    for g, dil in enumerate(DILATIONS):
        qkv = _proj(h_slabs[g], w_rest, n_out=3 * DSWA_WIDTH, dil=dil, w_transposed=True,
                    w_block=lambda j, g=g: j * blocks_per_part + g, name=f"dswa_proj_d{dil}")
        o, lse = _band_attention(qkv, bias, g, dil)
        outs.append(o)
        lses.append(lse)
    b_gated = _group_mixture(outs, lses, gates)

    merged = _merge_branches(a_gated, b_gated, gates, w_om, w_od)
    out = _proj(merged, w_o, n_out=D_MODEL, w_block=lambda j: j, out_dtype=F32, name="out_proj")
    y = _final_layer_norm(x2, out, emb_ln_g, emb_ln_b, ln_g[0], ln_b[0])
    return y[None]
```

```python
import functools
import math

import jax
import jax.numpy as jnp
import numpy as np
from jax import lax
from jax.experimental import pallas as pl
from jax.experimental.pallas import tpu as pltpu

D_MODEL = 4096
SEQ = 8192
DEPTH = 1

MLA_HEADS = 16
Q_LORA_RANK = 1024
KV_LORA_RANK = 512
QK_NOPE_DIM = 128
QK_ROPE_DIM = 64
QK_HEAD_DIM = QK_NOPE_DIM + QK_ROPE_DIM
V_HEAD_DIM = 128
MLA_WIDTH = MLA_HEADS * V_HEAD_DIM
ROPE_THETA = 10000.0

DIL_GROUPS = ((128, 1), (512, 4), (2048, 16))
DILATIONS = tuple(d for _, d in DIL_GROUPS)
DSWA_HEADS_PER_GROUP = 8
DSWA_HEADS = 24
DSWA_HEAD_DIM = 128
DSWA_QKV_WIDTH = DSWA_HEADS * DSWA_HEAD_DIM
DSWA_WIDTH = DSWA_HEADS_PER_GROUP * DSWA_HEAD_DIM
BAND_HALF = 64
BAND_TILE = 256

REL_BUCKETS = 32
REL_MAX_DISTANCE = 1024

DEEPNORM_ALPHA = (2.0 * DEPTH) ** 0.25
LN_EPS = 1e-5
RMS_EPS = 1e-6
NEG_INF = -1e30

IN_SPLITS = (Q_LORA_RANK, KV_LORA_RANK + QK_ROPE_DIM, DSWA_QKV_WIDTH, DSWA_QKV_WIDTH,
             DSWA_QKV_WIDTH, MLA_WIDTH, DSWA_WIDTH, D_MODEL, D_MODEL)
IN_OFFSETS = tuple(int(v) for v in np.cumsum(IN_SPLITS)[:-1])
MLA_A_COLS = IN_OFFSETS[1]
REST_COLS = sum(IN_SPLITS) - MLA_A_COLS

LANES = 128
QK_PAD_DIM = 256
COL_BLOCK = 1024
GATE_COL0 = 3 * DSWA_QKV_WIDTH // COL_BLOCK
GATE_WIDTH = MLA_WIDTH + DSWA_WIDTH + 2 * D_MODEL
VMEM_LIMIT = 56 * 1024 * 1024

BF16 = jnp.bfloat16
F32 = jnp.float32


def _params(sem):
    return pltpu.CompilerParams(dimension_semantics=sem, vmem_limit_bytes=VMEM_LIMIT)


def _dot(a, b):
    return jnp.dot(a, b, preferred_element_type=F32)


def _dot_nt(a, b):
    return lax.dot_general(a, b, (((1,), (1,)), ((), ())), preferred_element_type=F32)


def _sigmoid(x):
    return 0.5 * jnp.tanh(0.5 * x) + 0.5


def _layer_norm_rows(x, g, b):
    mu = jnp.mean(x, axis=-1, keepdims=True)
    xc = x - mu
    var = jnp.mean(xc * xc, axis=-1, keepdims=True)
    return xc * lax.rsqrt(var + LN_EPS) * g + b


def _rms_norm_rows(x, g):
    return x * lax.rsqrt(jnp.mean(x * x, axis=-1, keepdims=True) + RMS_EPS) * g


def _rope_lanes(t, tab):
    c, s1, s2 = tab[:, :LANES], tab[:, LANES:2 * LANES], tab[:, 2 * LANES:]
    return t * c + pltpu.roll(t, 32, 1) * s1 + pltpu.roll(t, LANES - 32, 1) * s2


def _class_gather_matrix(tm, dil):
    per_class = tm // dil
    i = np.arange(tm)
    p = np.zeros((tm, tm), np.float32)
    p[i, (i % per_class) * dil + i // per_class] = 1.0
    return jnp.asarray(p, BF16)


def _ln_kernel(x_ref, g_ref, b_ref, p4_ref, p16_ref, h1_ref, h4_ref, h16_ref):
    d = x_ref.shape[1]
    y = _layer_norm_rows(x_ref[...], g_ref[...], b_ref[...]).astype(BF16)
    h1_ref[...] = y
    for dil, p_ref, slab_ref in ((DILATIONS[1], p4_ref, h4_ref), (DILATIONS[2], p16_ref, h16_ref)):
        rows = slab_ref.shape[0]
        grouped = _dot(p_ref[...], y).astype(BF16)
        for r in range(dil):
            slab_ref[:, r * d:(r + 1) * d] = grouped[r * rows:(r + 1) * rows, :]


def _input_layer_norm(x, g, b, tm=256):
    s, d = x.shape
    d4, d16 = DILATIONS[1], DILATIONS[2]
    return pl.pallas_call(
        _ln_kernel,
        out_shape=(jax.ShapeDtypeStruct((s, d), BF16),
                   jax.ShapeDtypeStruct((s // d4, d4 * d), BF16),
                   jax.ShapeDtypeStruct((s // d16, d16 * d), BF16)),
        grid=(s // tm,),
        in_specs=[pl.BlockSpec((tm, d), lambda i: (i, 0)),
                  pl.BlockSpec((1, d), lambda i: (0, 0)),
                  pl.BlockSpec((1, d), lambda i: (0, 0)),
                  pl.BlockSpec((tm, tm), lambda i: (0, 0)),
                  pl.BlockSpec((tm, tm), lambda i: (0, 0))],
        out_specs=(pl.BlockSpec((tm, d), lambda i: (i, 0)),
                   pl.BlockSpec((tm // d4, d4 * d), lambda i: (i, 0)),
                   pl.BlockSpec((tm // d16, d16 * d), lambda i: (i, 0))),
        compiler_params=_params(("parallel",)),
        name="input_ln",
    )(x, g.reshape(1, d), b.reshape(1, d),
      _class_gather_matrix(tm, d4), _class_gather_matrix(tm, d16))


def _proj_kernel(a_ref, w_ref, o_ref, *, n_silu_blocks, gated, w_transposed):
    acc = (_dot_nt if w_transposed else _dot)(a_ref[...], w_ref[...])
    if gated:
        sig = _sigmoid(acc)
        acc = jnp.where(pl.program_id(0) < n_silu_blocks, acc * sig, sig)
    o_ref[...] = acc.astype(o_ref.dtype)


def _proj(a, w, *, n_out, w_block, dil=1, tm=1024, tn=COL_BLOCK, gated=False, n_silu_blocks=0,
          w_transposed=False, out_dtype=BF16, name):
    rows, kd = a.shape
    k = kd // dil
    nj = n_out // tn
    tm = min(tm, rows)
    kern = functools.partial(_proj_kernel, n_silu_blocks=n_silu_blocks, gated=gated,
                             w_transposed=w_transposed)
    if w_transposed:
        w_spec = pl.BlockSpec((tn, k), lambda j, r, i: (w_block(j), 0))
    else:
        w_spec = pl.BlockSpec((k, tn), lambda j, r, i: (0, w_block(j)))
    return pl.pallas_call(
        kern,
        out_shape=jax.ShapeDtypeStruct((rows, dil * n_out), out_dtype),
        grid=(nj, dil, rows // tm),
        in_specs=[pl.BlockSpec((tm, k), lambda j, r, i: (i, r)), w_spec],
        out_specs=pl.BlockSpec((tm, tn), lambda j, r, i: (i, r * nj + j)),
        compiler_params=_params(("parallel", "parallel", "parallel")),
        name=name,
    )(a, w)


def _mla_a_kernel(h_ref, w_ref, gq_ref, gkv_ref, tab_ref, qn_ref, ckvn_ref, kpe_ref):
    acc = _dot_nt(h_ref[...], w_ref[...])
    qn_ref[...] = _rms_norm_rows(acc[:, :Q_LORA_RANK], gq_ref[...]).astype(BF16)
    ckv = acc[:, Q_LORA_RANK:Q_LORA_RANK + KV_LORA_RANK]
    ckvn_ref[...] = _rms_norm_rows(ckv, gkv_ref[...]).astype(BF16)
    k_pe = acc[:, Q_LORA_RANK + KV_LORA_RANK:]
    k_pe = jnp.concatenate([k_pe, jnp.zeros((k_pe.shape[0], LANES - QK_ROPE_DIM), F32)], axis=1)
    kpe_ref[...] = _rope_lanes(k_pe, tab_ref[...]).astype(BF16)


def _mla_a_proj(h, w, gq, gkv, k_tab, tm=512):
    s, k = h.shape
    return pl.pallas_call(
        _mla_a_kernel,
        out_shape=(jax.ShapeDtypeStruct((s, Q_LORA_RANK), BF16),
                   jax.ShapeDtypeStruct((s, KV_LORA_RANK), BF16),
                   jax.ShapeDtypeStruct((s, LANES), BF16)),
        grid=(s // tm,),
        in_specs=[pl.BlockSpec((tm, k), lambda i: (i, 0)),
                  pl.BlockSpec((MLA_A_COLS, k), lambda i: (0, 0)),
                  pl.BlockSpec((1, Q_LORA_RANK), lambda i: (0, 0)),
                  pl.BlockSpec((1, KV_LORA_RANK), lambda i: (0, 0)),
                  pl.BlockSpec((tm, 3 * LANES), lambda i: (i, 0))],
        out_specs=(pl.BlockSpec((tm, Q_LORA_RANK), lambda i: (i, 0)),
                   pl.BlockSpec((tm, KV_LORA_RANK), lambda i: (i, 0)),
                   pl.BlockSpec((tm, LANES), lambda i: (i, 0))),
        compiler_params=_params(("parallel",)),
        name="mla_a_proj",
    )(h, w, gq.reshape(1, -1), gkv.reshape(1, -1), k_tab)


def _mla_q_kernel(qn_ref, w_ref, tab_ref, q_ref, *, q_scale):
    qn = qn_ref[...]
    tab = tab_ref[...]
    for hd in range(MLA_HEADS):
        lo = hd * QK_PAD_DIM
        acc = _dot(qn, w_ref[:, lo:lo + QK_PAD_DIM]) * q_scale
        q_ref[:, lo:lo + LANES] = acc[:, :LANES].astype(BF16)
        q_ref[:, lo + LANES:lo + QK_PAD_DIM] = _rope_lanes(acc[:, LANES:], tab).astype(BF16)


def _mla_q_proj(qn, w, rope_tab, q_scale, tm=1024):
    s, k = qn.shape
    n = w.shape[1]
    return pl.pallas_call(
        functools.partial(_mla_q_kernel, q_scale=q_scale),
        out_shape=jax.ShapeDtypeStruct((s, n), BF16),
        grid=(s // tm,),
        in_specs=[pl.BlockSpec((tm, k), lambda i: (i, 0)),
                  pl.BlockSpec((k, n), lambda i: (0, 0)),
                  pl.BlockSpec((tm, 3 * LANES), lambda i: (i, 0))],
        out_specs=pl.BlockSpec((tm, n), lambda i: (i, 0)),
        compiler_params=_params(("parallel",)),
        name="mla_q_proj",
    )(qn, w, rope_tab)


def _mla_attn_kernel(q_ref, k_ref, v_ref, kpe_ref, wsrc_ref, o_ref, wdst_ref, kcat_ref, vcat_ref,
                     *, tk, sub):
    wdst_ref[...] = wsrc_ref[...].astype(wdst_ref.dtype)

    @pl.when(pl.program_id(1) == 0)
    def _():
        kcat_ref[:, :LANES] = k_ref[...]
        kcat_ref[:, LANES:] = kpe_ref[...]
        vcat_ref[:, :V_HEAD_DIM] = v_ref[...]
        vcat_ref[:, V_HEAD_DIM:] = jnp.ones((v_ref.shape[0], LANES), BF16)

    n_kv = k_ref.shape[0] // tk
    n_sub = q_ref.shape[0] // sub
    qs = [q_ref[c * sub:(c + 1) * sub, :] for c in range(n_sub)]
    ms = [jnp.full((sub, 1), -jnp.inf, F32) for _ in range(n_sub)]
    accs = [jnp.zeros((sub, V_HEAD_DIM + LANES), F32) for _ in range(n_sub)]
    for j in range(n_kv):
        kblk = kcat_ref[j * tk:(j + 1) * tk, :]
        vblk = vcat_ref[j * tk:(j + 1) * tk, :]
        for c in range(n_sub):
            s = _dot_nt(qs[c], kblk)
            m_new = jnp.maximum(ms[c], jnp.max(s, axis=1, keepdims=True))
            alpha = jnp.exp2(ms[c] - m_new)
            p = jnp.exp2(s - m_new).astype(BF16)
            accs[c] = alpha * accs[c] + _dot(p, vblk)
            ms[c] = m_new
    for c in range(n_sub):
        rows = slice(c * sub, (c + 1) * sub)
        attn = accs[c][:, :V_HEAD_DIM] / accs[c][:, V_HEAD_DIM:]
        o_ref[rows, :] = attn.astype(o_ref.dtype)


def _mla_attention(q, kv, kpe, w_t, w_row0, w_rows, tq=2048, tk=1024, sub=512):
    s = q.shape[0]
    k_w = w_t.shape[1]
    n_q = s // tq
    tr = w_rows // (MLA_HEADS * n_q)
    assert tr * MLA_HEADS * n_q == w_rows and w_row0 % tr == 0 and tr % 16 == 0
    return pl.pallas_call(
        functools.partial(_mla_attn_kernel, tk=tk, sub=sub),
        out_shape=(jax.ShapeDtypeStruct((s, MLA_WIDTH), BF16),
                   jax.ShapeDtypeStruct((w_rows, k_w), BF16)),
        grid=(MLA_HEADS, n_q),
        in_specs=[pl.BlockSpec((tq, QK_PAD_DIM), lambda h, i: (i, h)),
                  pl.BlockSpec((s, QK_NOPE_DIM), lambda h, i: (0, h)),
                  pl.BlockSpec((s, V_HEAD_DIM), lambda h, i: (0, MLA_HEADS + h)),
                  pl.BlockSpec((s, LANES), lambda h, i: (0, 0)),
                  pl.BlockSpec((tr, k_w), lambda h, i: (w_row0 // tr + h * n_q + i, 0))],
        out_specs=(pl.BlockSpec((tq, V_HEAD_DIM), lambda h, i: (i, h)),
                   pl.BlockSpec((tr, k_w), lambda h, i: (h * n_q + i, 0))),
        scratch_shapes=[pltpu.VMEM((s, QK_PAD_DIM), BF16),
                        pltpu.VMEM((s, V_HEAD_DIM + LANES), BF16)],
        compiler_params=_params(("parallel", "arbitrary")),
        name="mla_attention",
    )(q, kv, kv, kpe, w_t)


def _band_bucket_index(dil, tq):
    nb = REL_BUCKETS // 2
    max_exact = nb // 2
    rel = (np.arange(tq + 2 * BAND_HALF)[None, :] - BAND_HALF - np.arange(tq)[:, None]) * dil
    n = np.abs(rel)
    nf = np.maximum(n, 1).astype(np.float64)
    large = max_exact + (np.log(nf / max_exact) / math.log(REL_MAX_DISTANCE / max_exact)
                         * (nb - max_exact)).astype(np.int32)
    large = np.minimum(large, nb - 1)
    return (np.where(rel > 0, nb, 0) + np.where(n < max_exact, n, large)).astype(np.int32)


def _band_bias_kernel(rb_ref, idx_ref, o_ref):
    hd = pl.program_id(0)
    idx = idx_ref[0]
    acc = jnp.zeros(idx.shape, F32)
    for b in range(REL_BUCKETS):
        acc = jnp.where(idx == b, rb_ref[b, hd], acc)
    o_ref[0] = acc


def _band_bias(rel_bias, tq):
    idx = jnp.asarray(np.stack([_band_bucket_index(dil, tq) for dil in DILATIONS]))
    tw = idx.shape[2]
    return pl.pallas_call(
        _band_bias_kernel,
        out_shape=jax.ShapeDtypeStruct((DSWA_HEADS, tq, tw), F32),
        grid=(DSWA_HEADS,),
        in_specs=[pl.BlockSpec(memory_space=pltpu.SMEM),
                  pl.BlockSpec((1, tq, tw), lambda h: (h // DSWA_HEADS_PER_GROUP, 0, 0))],
        out_specs=pl.BlockSpec((1, tq, tw), lambda h: (h, 0, 0)),
        compiler_params=_params(("parallel",)),
        name="band_bias",
    )(rel_bias, idx)


def _band_attn_kernel(q_ref, kp_ref, ko_ref, kn_ref, vp_ref, vo_ref, vn_ref, bias_ref,
                      o_ref, lse_ref):
    tq = q_ref.shape[0]
    tw = tq + 2 * BAND_HALF
    tile = pl.program_id(1)
    prev_ok = tile != 0
    next_ok = tile != pl.num_programs(1) - 1
    row = lax.broadcasted_iota(jnp.int32, (tq, tw), 0)
    col = lax.broadcasted_iota(jnp.int32, (tq, tw), 1)
    rel = col - BAND_HALF - row
    mask = ((jnp.abs(rel) <= BAND_HALF) & ((col >= BAND_HALF) | prev_ok)
            & ((col < tq + BAND_HALF) | next_ok))
    lane = lax.broadcasted_iota(jnp.int32, (tq, LANES), 1)
    scale = DSWA_HEAD_DIM ** -0.5
    heads = range(DSWA_HEADS_PER_GROUP)
    cols = [slice(hd * DSWA_HEAD_DIM, (hd + 1) * DSWA_HEAD_DIM) for hd in heads]
    kc = [jnp.concatenate([kp_ref[:, c], ko_ref[:, c], kn_ref[:, c]], axis=0) for c in cols]
    s = [_dot_nt(q_ref[:, cols[hd]], kc[hd]) * scale + bias_ref[hd] for hd in heads]
    s = [jnp.where(mask, sh, NEG_INF) for sh in s]
    m = [jnp.max(sh, axis=1, keepdims=True) for sh in s]
    p = [jnp.exp(sh - mh).astype(BF16) for sh, mh in zip(s, m)]
    ones = jnp.ones((tw, LANES), BF16)
    vc = [jnp.concatenate(
        [jnp.concatenate([vp_ref[:, c], vo_ref[:, c], vn_ref[:, c]], axis=0), ones], axis=1)
        for c in cols]
    pv = [_dot(p[hd], vc[hd]) for hd in heads]
    lse_tile = jnp.zeros((tq, LANES), F32)
    for hd in heads:
        den = pv[hd][:, DSWA_HEAD_DIM:]
        o_ref[:, cols[hd]] = (pv[hd][:, :DSWA_HEAD_DIM] / den).astype(o_ref.dtype)
        lse_tile = jnp.where(lane == hd, m[hd] + jnp.log(den), lse_tile)
    lse_ref[...] = lse_tile


def _band_attention(qkv, bias, group, dil):
    rows = qkv.shape[0]
    tq = bias.shape[1]
    nt = rows // tq
    per_tile = tq // BAND_HALF
    n_edge = rows // BAND_HALF
    tw = tq + 2 * BAND_HALF

    def own(part):
        return pl.BlockSpec((tq, DSWA_WIDTH), lambda r, n: (n, 3 * r + part))

    def prev(part):
        return pl.BlockSpec((BAND_HALF, DSWA_WIDTH),
                            lambda r, n: (jnp.maximum(n * per_tile - 1, 0), 3 * r + part))

    def nxt(part):
        return pl.BlockSpec((BAND_HALF, DSWA_WIDTH),
                            lambda r, n: (jnp.minimum((n + 1) * per_tile, n_edge - 1), 3 * r + part))

    return pl.pallas_call(
        _band_attn_kernel,
        out_shape=(jax.ShapeDtypeStruct((rows, dil * DSWA_WIDTH), BF16),
                   jax.ShapeDtypeStruct((rows, dil * LANES), F32)),
        grid=(dil, nt),
        in_specs=[own(0),
                  prev(1), own(1), nxt(1),
                  prev(2), own(2), nxt(2),
                  pl.BlockSpec((DSWA_HEADS_PER_GROUP, tq, tw), lambda r, n: (group, 0, 0))],
        out_specs=(pl.BlockSpec((tq, DSWA_WIDTH), lambda r, n: (n, r)),
                   pl.BlockSpec((tq, LANES), lambda r, n: (n, r))),
        compiler_params=_params(("parallel", "parallel")),
        name=f"band_attention_d{dil}",
    )(qkv, qkv, qkv, qkv, qkv, qkv, qkv, bias)


def _mix_kernel(o1_ref, o4_ref, o16_ref, l1_ref, l4_ref, l16_ref, sg_ref, out_ref,
                o4n_ref, l4n_ref, o16n_ref, l16n_ref):
    for dil, o_ref, l_ref, on_ref, ln_ref in ((DILATIONS[1], o4_ref, l4_ref, o4n_ref, l4n_ref),
                                               (DILATIONS[2], o16_ref, l16_ref, o16n_ref, l16n_ref)):
        rows = o_ref.shape[0]
        for r in range(dil):
            for hd in range(DSWA_HEADS_PER_GROUP):
                lo = r * DSWA_WIDTH + hd * DSWA_HEAD_DIM
                on_ref[hd, pl.ds(r, rows, stride=dil), :] = (
                    o_ref[:, lo:lo + DSWA_HEAD_DIM].astype(F32))
            ln_ref[pl.ds(r, rows, stride=dil), :] = l_ref[:, r * LANES:(r + 1) * LANES]
    for hd in range(DSWA_HEADS_PER_GROUP):
        cols = slice(hd * DSWA_HEAD_DIM, (hd + 1) * DSWA_HEAD_DIM)
        l1 = l1_ref[:, hd:hd + 1]
        l2 = l4n_ref[:, hd:hd + 1]
        l3 = l16n_ref[:, hd:hd + 1]
        mx = jnp.maximum(jnp.maximum(l1, l2), l3)
        e1, e2, e3 = jnp.exp(l1 - mx), jnp.exp(l2 - mx), jnp.exp(l3 - mx)
        den = e1 + e2 + e3
        mix = ((e1 / den) * o1_ref[:, cols].astype(F32)
               + (e2 / den) * o4n_ref[hd]
               + (e3 / den) * o16n_ref[hd])
        out_ref[:, cols] = (mix * sg_ref[:, cols].astype(F32)).astype(out_ref.dtype)


def _group_mixture(outs, lses, gates, tm=256):
    o1, o4, o16 = outs
    l1, l4, l16 = lses
    s = o1.shape[0]
    d4, d16 = DILATIONS[1], DILATIONS[2]
    w = DSWA_WIDTH
    return pl.pallas_call(
        _mix_kernel,
        out_shape=jax.ShapeDtypeStruct((s, w), BF16),
        grid=(s // tm,),
        in_specs=[pl.BlockSpec((tm, w), lambda i: (i, 0)),
                  pl.BlockSpec((tm // d4, d4 * w), lambda i: (i, 0)),
                  pl.BlockSpec((tm // d16, d16 * w), lambda i: (i, 0)),
                  pl.BlockSpec((tm, LANES), lambda i: (i, 0)),
                  pl.BlockSpec((tm // d4, d4 * LANES), lambda i: (i, 0)),
                  pl.BlockSpec((tm // d16, d16 * LANES), lambda i: (i, 0)),
                  pl.BlockSpec((tm, w), lambda i: (i, MLA_WIDTH // w))],
        out_specs=pl.BlockSpec((tm, w), lambda i: (i, 0)),
        scratch_shapes=[pltpu.VMEM((DSWA_HEADS_PER_GROUP, tm, DSWA_HEAD_DIM), F32),
                        pltpu.VMEM((tm, LANES), F32),
                        pltpu.VMEM((DSWA_HEADS_PER_GROUP, tm, DSWA_HEAD_DIM), F32),
                        pltpu.VMEM((tm, LANES), F32)],
        compiler_params=_params(("parallel",)),
        name="group_mixture",
    )(o1, o4, o16, l1, l4, l16, gates)


def _merge_kernel(a_ref, sg_ref, b_ref, wm_ref, wd_ref, rm_ref, rd_ref, o_ref):
    a_gated = (a_ref[...].astype(F32) * sg_ref[...].astype(F32)).astype(BF16)
    y_mla = _dot(a_gated, wm_ref[...])
    y_dswa = _dot(b_ref[...], wd_ref[...])
    merged = rm_ref[...].astype(F32) * y_mla + rd_ref[...].astype(F32) * y_dswa
    o_ref[...] = merged.astype(o_ref.dtype)


def _merge_branches(a, b_gated, gates, w_o_mla, w_o_dswa, tm=512, tn=COL_BLOCK):
    s = a.shape[0]
    d = w_o_mla.shape[1]
    r_mla_block = (MLA_WIDTH + DSWA_WIDTH) // tn
    r_dswa_block = (MLA_WIDTH + DSWA_WIDTH + D_MODEL) // tn
    return pl.pallas_call(
        _merge_kernel,
        out_shape=jax.ShapeDtypeStruct((s, d), BF16),
        grid=(d // tn, s // tm),
        in_specs=[pl.BlockSpec((tm, MLA_WIDTH), lambda j, i: (i, 0)),
                  pl.BlockSpec((tm, MLA_WIDTH), lambda j, i: (i, 0)),
                  pl.BlockSpec((tm, DSWA_WIDTH), lambda j, i: (i, 0)),
                  pl.BlockSpec((MLA_WIDTH, tn), lambda j, i: (0, j)),
                  pl.BlockSpec((DSWA_WIDTH, tn), lambda j, i: (0, j)),
                  pl.BlockSpec((tm, tn), lambda j, i: (i, r_mla_block + j)),
                  pl.BlockSpec((tm, tn), lambda j, i: (i, r_dswa_block + j))],
        out_specs=pl.BlockSpec((tm, tn), lambda j, i: (i, j)),
        compiler_params=_params(("parallel", "parallel")),
        name="merge_branches",
    )(a, gates, b_gated, w_o_mla, w_o_dswa, gates, gates)


def _final_ln_kernel(x_ref, o_ref, g0_ref, b0_ref, g_ref, b_ref, out_ref):
    h = _layer_norm_rows(x_ref[...], g0_ref[...], b0_ref[...])
    out_ref[...] = _layer_norm_rows(DEEPNORM_ALPHA * h + o_ref[...], g_ref[...], b_ref[...])


def _final_layer_norm(x, out, g0, b0, g, b, tm=256):
    s, d = x.shape
    row = pl.BlockSpec((tm, d), lambda i: (i, 0))
    vec = pl.BlockSpec((1, d), lambda i: (0, 0))
    return pl.pallas_call(
        _final_ln_kernel,
        out_shape=jax.ShapeDtypeStruct((s, d), F32),
        grid=(s // tm,),
        in_specs=[row, row, vec, vec, vec, vec],
        out_specs=row,
        compiler_params=_params(("parallel",)),
        name="final_ln",
    )(x, out, g0.reshape(1, d), b0.reshape(1, d), g.reshape(1, d), b.reshape(1, d))


def _cast_kernel(w_ref, o_ref):
    o_ref[...] = w_ref[...].astype(o_ref.dtype)


def _cast_rows_bf16(w_t, row0, n_rows, name):
    k = w_t.shape[1]
    tr = math.gcd(MLA_A_COLS, REST_COLS)
    assert row0 % tr == 0 and n_rows % tr == 0 and tr % 16 == 0
    return pl.pallas_call(
        _cast_kernel,
        out_shape=jax.ShapeDtypeStruct((n_rows, k), BF16),
        grid=(n_rows // tr,),
        in_specs=[pl.BlockSpec((tr, k), lambda j: (row0 // tr + j, 0))],
        out_specs=pl.BlockSpec((tr, k), lambda j: (j, 0)),
        compiler_params=_params(("parallel",)),
        name=name,
    )(w_t)


def _rope_tables(seq):
    half = QK_ROPE_DIM // 2
    pos = jnp.arange(seq, dtype=F32)
    inv_freq = 1.0 / (ROPE_THETA ** (jnp.arange(0, QK_ROPE_DIM, 2, dtype=F32) / QK_ROPE_DIM))
    lane_freq = jnp.concatenate([inv_freq, inv_freq, jnp.zeros((LANES - QK_ROPE_DIM,), F32)])
    ang = pos[:, None] * lane_freq[None, :]
    lane = np.arange(LANES)
    on_t1 = jnp.asarray((lane < half).astype(np.float32))
    on_t2 = jnp.asarray(((lane >= half) & (lane < QK_ROPE_DIM)).astype(np.float32))
    cos, sin = jnp.cos(ang), jnp.sin(ang)
    return jnp.concatenate([cos * (on_t1 + on_t2), sin * on_t2, -sin * on_t1], axis=1)


def kernel(x, emb_ln_g, emb_ln_b, rel_bias, w_in, q_a_norm_g, w_q_b, kv_a_norm_g, w_kv_b,
           w_o_mla, w_o_dswa, w_out, ln_g, ln_b):
    assert DEPTH == 1 and x.shape == (1, SEQ, D_MODEL)
    x2 = x[0]

    w_in_t = jnp.swapaxes(w_in, 1, 2)[0]
    w_mla_a = _cast_rows_bf16(w_in_t, 0, MLA_A_COLS, "mla_a_weights_bf16")
    w_q = jnp.pad(w_q_b[0].reshape(Q_LORA_RANK, MLA_HEADS, QK_HEAD_DIM),
                  ((0, 0), (0, 0), (0, QK_PAD_DIM - QK_HEAD_DIM))
                  ).reshape(Q_LORA_RANK, MLA_HEADS * QK_PAD_DIM).astype(BF16)
    w_kv = (w_kv_b[0].reshape(KV_LORA_RANK, MLA_HEADS, 2, QK_NOPE_DIM)
            .transpose(0, 2, 1, 3).reshape(KV_LORA_RANK, 2 * MLA_WIDTH).astype(BF16))
    w_om = w_o_mla[0].astype(BF16)
    w_od = w_o_dswa[0].astype(BF16)
    w_o = w_out[0].astype(BF16)

    q_scale = QK_HEAD_DIM ** -0.5 * math.log2(math.e)
    rope_tab = _rope_tables(SEQ)

    h_slabs = _input_layer_norm(x2, emb_ln_g, emb_ln_b)
    h = h_slabs[0]

    qn, ckvn, kpe = _mla_a_proj(h, w_mla_a, q_a_norm_g[0], kv_a_norm_g[0], rope_tab)
    q = _mla_q_proj(qn, w_q, rope_tab, q_scale)
    kv = _proj(ckvn, w_kv, n_out=2 * MLA_WIDTH, w_block=lambda j: j, name="mla_kv_proj")
    a, w_rest = _mla_attention(q, kv, kpe, w_in_t, MLA_A_COLS, REST_COLS)

    gates = _proj(h, w_rest, n_out=GATE_WIDTH, w_block=lambda j: GATE_COL0 + j, gated=True,
                  n_silu_blocks=(MLA_WIDTH + DSWA_WIDTH) // COL_BLOCK, w_transposed=True,
                  name="gate_proj")

    bias = _band_bias(rel_bias, BAND_TILE)
    outs, lses = [], []
    blocks_per_part = DSWA_QKV_WIDTH // COL_BLOCK
    for g, dil in enumerate(DILATIONS):
        qkv = _proj(h_slabs[g], w_rest, n_out=3 * DSWA_WIDTH, dil=dil, w_transposed=True,
                    w_block=lambda j, g=g: j * blocks_per_part + g, name=f"dswa_proj_d{dil}")
        o, lse = _band_attention(qkv, bias, g, dil)
        outs.append(o)
        lses.append(lse)
    b_gated = _group_mixture(outs, lses, gates)

    merged = _merge_branches(a, b_gated, gates, w_om, w_od)
    out = _proj(merged, w_o, n_out=D_MODEL, w_block=lambda j: j, out_dtype=F32, name="out_proj")
    y = _final_layer_norm(x2, out, emb_ln_g, emb_ln_b, ln_g[0], ln_b[0])
    return y[None]
```

```python
import functools
import math

import jax
import jax.numpy as jnp
import numpy as np
from jax import lax
from jax.experimental import pallas as pl
from jax.experimental.pallas import tpu as pltpu

D_MODEL = 4096
SEQ = 8192
DEPTH = 1

MLA_HEADS = 16
Q_LORA_RANK = 1024
KV_LORA_RANK = 512
QK_NOPE_DIM = 128
QK_ROPE_DIM = 64
QK_HEAD_DIM = QK_NOPE_DIM + QK_ROPE_DIM
V_HEAD_DIM = 128
MLA_WIDTH = MLA_HEADS * V_HEAD_DIM
ROPE_THETA = 10000.0

DIL_GROUPS = ((128, 1), (512, 4), (2048, 16))
DILATIONS = tuple(d for _, d in DIL_GROUPS)
DSWA_HEADS_PER_GROUP = 8
DSWA_HEADS = 24
DSWA_HEAD_DIM = 128
DSWA_QKV_WIDTH = DSWA_HEADS * DSWA_HEAD_DIM
DSWA_WIDTH = DSWA_HEADS_PER_GROUP * DSWA_HEAD_DIM
BAND_HALF = 64
BAND_TILE = 256

REL_BUCKETS = 32
REL_MAX_DISTANCE = 1024

DEEPNORM_ALPHA = (2.0 * DEPTH) ** 0.25
LN_EPS = 1e-5
RMS_EPS = 1e-6
NEG_INF = -1e30

IN_SPLITS = (Q_LORA_RANK, KV_LORA_RANK + QK_ROPE_DIM, DSWA_QKV_WIDTH, DSWA_QKV_WIDTH,
             DSWA_QKV_WIDTH, MLA_WIDTH, DSWA_WIDTH, D_MODEL, D_MODEL)
IN_OFFSETS = tuple(int(v) for v in np.cumsum(IN_SPLITS)[:-1])
MLA_A_COLS = IN_OFFSETS[1]
REST_COLS = sum(IN_SPLITS) - MLA_A_COLS

LANES = 128
QK_PAD_DIM = 256
COL_BLOCK = 1024
GATE_COL0 = 3 * DSWA_QKV_WIDTH // COL_BLOCK
GATE_WIDTH = MLA_WIDTH + DSWA_WIDTH + 2 * D_MODEL
VMEM_LIMIT = 56 * 1024 * 1024

BF16 = jnp.bfloat16
F32 = jnp.float32


def _params(sem):
    return pltpu.CompilerParams(dimension_semantics=sem, vmem_limit_bytes=VMEM_LIMIT)


def _dot(a, b):
    return jnp.dot(a, b, preferred_element_type=F32)


def _dot_nt(a, b):
    return lax.dot_general(a, b, (((1,), (1,)), ((), ())), preferred_element_type=F32)


def _sigmoid(x):
    return 0.5 * jnp.tanh(0.5 * x) + 0.5


def _layer_norm_rows(x, g, b):
    mu = jnp.mean(x, axis=-1, keepdims=True)
    xc = x - mu
    var = jnp.mean(xc * xc, axis=-1, keepdims=True)
    return xc * lax.rsqrt(var + LN_EPS) * g + b


def _rms_norm_rows(x, g):
    return x * lax.rsqrt(jnp.mean(x * x, axis=-1, keepdims=True) + RMS_EPS) * g


def _rope_lanes(t, tab):
    c, s1, s2 = tab[:, :LANES], tab[:, LANES:2 * LANES], tab[:, 2 * LANES:]
    return t * c + pltpu.roll(t, 32, 1) * s1 + pltpu.roll(t, LANES - 32, 1) * s2


def _class_gather_matrix(tm, dil):
    per_class = tm // dil
    i = np.arange(tm)
    p = np.zeros((tm, tm), np.float32)
    p[i, (i % per_class) * dil + i // per_class] = 1.0
    return jnp.asarray(p, BF16)


def _ln_kernel(x_ref, g_ref, b_ref, p4_ref, p16_ref, h1_ref, h4_ref, h16_ref):
    d = x_ref.shape[1]
    y = _layer_norm_rows(x_ref[...], g_ref[...], b_ref[...]).astype(BF16)
    h1_ref[...] = y
    for dil, p_ref, slab_ref in ((DILATIONS[1], p4_ref, h4_ref), (DILATIONS[2], p16_ref, h16_ref)):
        rows = slab_ref.shape[0]
        grouped = _dot(p_ref[...], y).astype(BF16)
        for r in range(dil):
            slab_ref[:, r * d:(r + 1) * d] = grouped[r * rows:(r + 1) * rows, :]


def _input_layer_norm(x, g, b, tm=256):
    s, d = x.shape
    d4, d16 = DILATIONS[1], DILATIONS[2]
    return pl.pallas_call(
        _ln_kernel,
        out_shape=(jax.ShapeDtypeStruct((s, d), BF16),
                   jax.ShapeDtypeStruct((s // d4, d4 * d), BF16),
                   jax.ShapeDtypeStruct((s // d16, d16 * d), BF16)),
        grid=(s // tm,),
        in_specs=[pl.BlockSpec((tm, d), lambda i: (i, 0)),
                  pl.BlockSpec((1, d), lambda i: (0, 0)),
                  pl.BlockSpec((1, d), lambda i: (0, 0)),
                  pl.BlockSpec((tm, tm), lambda i: (0, 0)),
                  pl.BlockSpec((tm, tm), lambda i: (0, 0))],
        out_specs=(pl.BlockSpec((tm, d), lambda i: (i, 0)),
                   pl.BlockSpec((tm // d4, d4 * d), lambda i: (i, 0)),
                   pl.BlockSpec((tm // d16, d16 * d), lambda i: (i, 0))),
        compiler_params=_params(("parallel",)),
        name="input_ln",
    )(x, g.reshape(1, d), b.reshape(1, d),
      _class_gather_matrix(tm, d4), _class_gather_matrix(tm, d16))


def _proj_kernel(a_ref, w_ref, o_ref, *, n_silu_blocks, gated, w_transposed):
    acc = (_dot_nt if w_transposed else _dot)(a_ref[...], w_ref[...])
    if gated:
        sig = _sigmoid(acc)
        acc = jnp.where(pl.program_id(0) < n_silu_blocks, acc * sig, sig)
    o_ref[...] = acc.astype(o_ref.dtype)


def _proj(a, w, *, n_out, w_block, dil=1, tm=1024, tn=COL_BLOCK, gated=False, n_silu_blocks=0,
          w_transposed=False, out_dtype=BF16, name):
    rows, kd = a.shape
    k = kd // dil
    nj = n_out // tn
    tm = min(tm, rows)
    kern = functools.partial(_proj_kernel, n_silu_blocks=n_silu_blocks, gated=gated,
                             w_transposed=w_transposed)
    if w_transposed:
        w_spec = pl.BlockSpec((tn, k), lambda j, r, i: (w_block(j), 0))
    else:
        w_spec = pl.BlockSpec((k, tn), lambda j, r, i: (0, w_block(j)))
    return pl.pallas_call(
        kern,
        out_shape=jax.ShapeDtypeStruct((rows, dil * n_out), out_dtype),
        grid=(nj, dil, rows // tm),
        in_specs=[pl.BlockSpec((tm, k), lambda j, r, i: (i, r)), w_spec],
        out_specs=pl.BlockSpec((tm, tn), lambda j, r, i: (i, r * nj + j)),
        compiler_params=_params(("parallel", "parallel", "parallel")),
        name=name,
    )(a, w)


def _mla_a_kernel(h_ref, w_ref, gq_ref, gkv_ref, tab_ref, qn_ref, ckvn_ref, kpe_ref):
    acc = _dot_nt(h_ref[...], w_ref[...])
    qn_ref[...] = _rms_norm_rows(acc[:, :Q_LORA_RANK], gq_ref[...]).astype(BF16)
    ckv = acc[:, Q_LORA_RANK:Q_LORA_RANK + KV_LORA_RANK]
    ckvn_ref[...] = _rms_norm_rows(ckv, gkv_ref[...]).astype(BF16)
    k_pe = acc[:, Q_LORA_RANK + KV_LORA_RANK:]
    k_pe = jnp.concatenate([k_pe, jnp.zeros((k_pe.shape[0], LANES - QK_ROPE_DIM), F32)], axis=1)
    kpe_ref[...] = _rope_lanes(k_pe, tab_ref[...]).astype(BF16)


def _mla_a_proj(h, w, gq, gkv, k_tab, tm=512):
    s, k = h.shape
    return pl.pallas_call(
        _mla_a_kernel,
        out_shape=(jax.ShapeDtypeStruct((s, Q_LORA_RANK), BF16),
                   jax.ShapeDtypeStruct((s, KV_LORA_RANK), BF16),
                   jax.ShapeDtypeStruct((s, LANES), BF16)),
        grid=(s // tm,),
        in_specs=[pl.BlockSpec((tm, k), lambda i: (i, 0)),
                  pl.BlockSpec((MLA_A_COLS, k), lambda i: (0, 0)),
                  pl.BlockSpec((1, Q_LORA_RANK), lambda i: (0, 0)),
                  pl.BlockSpec((1, KV_LORA_RANK), lambda i: (0, 0)),
                  pl.BlockSpec((tm, 3 * LANES), lambda i: (i, 0))],
        out_specs=(pl.BlockSpec((tm, Q_LORA_RANK), lambda i: (i, 0)),
                   pl.BlockSpec((tm, KV_LORA_RANK), lambda i: (i, 0)),
                   pl.BlockSpec((tm, LANES), lambda i: (i, 0))),
        compiler_params=_params(("parallel",)),
        name="mla_a_proj",
    )(h, w, gq.reshape(1, -1), gkv.reshape(1, -1), k_tab)


def _mla_q_kernel(qn_ref, w_ref, tab_ref, q_ref, *, q_scale):
    qn = qn_ref[...]
    tab = tab_ref[...]
    for hd in range(MLA_HEADS):
        lo = hd * QK_PAD_DIM
        acc = _dot(qn, w_ref[:, lo:lo + QK_PAD_DIM]) * q_scale
        q_ref[:, lo:lo + LANES] = acc[:, :LANES].astype(BF16)
        q_ref[:, lo + LANES:lo + QK_PAD_DIM] = _rope_lanes(acc[:, LANES:], tab).astype(BF16)


def _mla_q_proj(qn, w, rope_tab, q_scale, tm=1024):
    s, k = qn.shape
    n = w.shape[1]
    return pl.pallas_call(
        functools.partial(_mla_q_kernel, q_scale=q_scale),
        out_shape=jax.ShapeDtypeStruct((s, n), BF16),
        grid=(s // tm,),
        in_specs=[pl.BlockSpec((tm, k), lambda i: (i, 0)),
                  pl.BlockSpec((k, n), lambda i: (0, 0)),
                  pl.BlockSpec((tm, 3 * LANES), lambda i: (i, 0))],
        out_specs=pl.BlockSpec((tm, n), lambda i: (i, 0)),
        compiler_params=_params(("parallel",)),
        name="mla_q_proj",
    )(qn, w, rope_tab)


def _mla_attn_kernel(q_ref, k_ref, v_ref, kpe_ref, *rest, tk, sub, n_cast):
    o_ref = rest[n_cast]
    kcat_ref, vcat_ref = rest[2 * n_cast + 1:]
    for src_ref, dst_ref in zip(rest[:n_cast], rest[n_cast + 1:2 * n_cast + 1]):
        dst_ref[...] = src_ref[...].astype(dst_ref.dtype)

    @pl.when(pl.program_id(1) == 0)
    def _():
        kcat_ref[:, :LANES] = k_ref[...]
        kcat_ref[:, LANES:] = kpe_ref[...]
        vcat_ref[:, :V_HEAD_DIM] = v_ref[...]
        vcat_ref[:, V_HEAD_DIM:] = jnp.ones((v_ref.shape[0], LANES), BF16)

    n_kv = k_ref.shape[0] // tk
    n_sub = q_ref.shape[0] // sub
    qs = [q_ref[c * sub:(c + 1) * sub, :] for c in range(n_sub)]
    ms = [jnp.full((sub, 1), -jnp.inf, F32) for _ in range(n_sub)]
    accs = [jnp.zeros((sub, V_HEAD_DIM + LANES), F32) for _ in range(n_sub)]
    for j in range(n_kv):
        kblk = kcat_ref[j * tk:(j + 1) * tk, :]
        vblk = vcat_ref[j * tk:(j + 1) * tk, :]
        for c in range(n_sub):
            s = _dot_nt(qs[c], kblk)
            m_new = jnp.maximum(ms[c], jnp.max(s, axis=1, keepdims=True))
            alpha = jnp.exp2(ms[c] - m_new)
            p = jnp.exp2(s - m_new).astype(BF16)
            accs[c] = alpha * accs[c] + _dot(p, vblk)
            ms[c] = m_new
    for c in range(n_sub):
        rows = slice(c * sub, (c + 1) * sub)
        attn = accs[c][:, :V_HEAD_DIM] / accs[c][:, V_HEAD_DIM:]
        o_ref[rows, :] = attn.astype(o_ref.dtype)


def _mla_attention(q, kv, kpe, casts, tq=2048, tk=1024, sub=512):
    s = q.shape[0]
    n_q = s // tq
    steps = MLA_HEADS * n_q
    cast_in, cast_out, cast_shapes = [], [], []
    for w, row0, rows in casts:
        tr = rows // steps
        assert tr * steps == rows and row0 % tr == 0 and tr % 16 == 0
        cast_in.append(pl.BlockSpec(
            (tr, w.shape[1]), lambda h, i, row0=row0, tr=tr: (row0 // tr + h * n_q + i, 0)))
        cast_out.append(pl.BlockSpec((tr, w.shape[1]), lambda h, i: (h * n_q + i, 0)))
        cast_shapes.append(jax.ShapeDtypeStruct((rows, w.shape[1]), BF16))
    return pl.pallas_call(
        functools.partial(_mla_attn_kernel, tk=tk, sub=sub, n_cast=len(casts)),
        out_shape=(jax.ShapeDtypeStruct((s, MLA_WIDTH), BF16), *cast_shapes),
        grid=(MLA_HEADS, n_q),
        in_specs=[pl.BlockSpec((tq, QK_PAD_DIM), lambda h, i: (i, h)),
                  pl.BlockSpec((s, QK_NOPE_DIM), lambda h, i: (0, h)),
                  pl.BlockSpec((s, V_HEAD_DIM), lambda h, i: (0, MLA_HEADS + h)),
                  pl.BlockSpec((s, LANES), lambda h, i: (0, 0)),
                  *cast_in],
        out_specs=(pl.BlockSpec((tq, V_HEAD_DIM), lambda h, i: (i, h)), *cast_out),
        scratch_shapes=[pltpu.VMEM((s, QK_PAD_DIM), BF16),
                        pltpu.VMEM((s, V_HEAD_DIM + LANES), BF16)],
        compiler_params=_params(("parallel", "arbitrary")),
        name="mla_attention",
    )(q, kv, kv, kpe, *[w for w, _, _ in casts])


def _band_bucket_index(dil, tq):
    nb = REL_BUCKETS // 2
    max_exact = nb // 2
    rel = (np.arange(tq + 2 * BAND_HALF)[None, :] - BAND_HALF - np.arange(tq)[:, None]) * dil
    n = np.abs(rel)
    nf = np.maximum(n, 1).astype(np.float64)
    large = max_exact + (np.log(nf / max_exact) / math.log(REL_MAX_DISTANCE / max_exact)
                         * (nb - max_exact)).astype(np.int32)
    large = np.minimum(large, nb - 1)
    return (np.where(rel > 0, nb, 0) + np.where(n < max_exact, n, large)).astype(np.int32)


def _band_bias_kernel(rb_ref, idx_ref, o_ref):
    hd = pl.program_id(0)
    idx = idx_ref[0]
    acc = jnp.zeros(idx.shape, F32)
    for b in range(REL_BUCKETS):
        acc = jnp.where(idx == b, rb_ref[b, hd], acc)
    o_ref[0] = acc


def _band_bias(rel_bias, tq):
    idx = jnp.asarray(np.stack([_band_bucket_index(dil, tq) for dil in DILATIONS]))
    tw = idx.shape[2]
    return pl.pallas_call(
        _band_bias_kernel,
        out_shape=jax.ShapeDtypeStruct((DSWA_HEADS, tq, tw), F32),
        grid=(DSWA_HEADS,),
        in_specs=[pl.BlockSpec(memory_space=pltpu.SMEM),
                  pl.BlockSpec((1, tq, tw), lambda h: (h // DSWA_HEADS_PER_GROUP, 0, 0))],
        out_specs=pl.BlockSpec((1, tq, tw), lambda h: (h, 0, 0)),
        compiler_params=_params(("parallel",)),
        name="band_bias",
    )(rel_bias, idx)


def _band_attn_kernel(q_ref, kp_ref, ko_ref, kn_ref, vp_ref, vo_ref, vn_ref, bias_ref,
                      o_ref, lse_ref):
    tq = q_ref.shape[0]
    tw = tq + 2 * BAND_HALF
    tile = pl.program_id(1)
    prev_ok = tile != 0
    next_ok = tile != pl.num_programs(1) - 1
    row = lax.broadcasted_iota(jnp.int32, (tq, tw), 0)
    col = lax.broadcasted_iota(jnp.int32, (tq, tw), 1)
    rel = col - BAND_HALF - row
    mask = ((jnp.abs(rel) <= BAND_HALF) & ((col >= BAND_HALF) | prev_ok)
            & ((col < tq + BAND_HALF) | next_ok))
    lane = lax.broadcasted_iota(jnp.int32, (tq, LANES), 1)
    scale = DSWA_HEAD_DIM ** -0.5
    heads = range(DSWA_HEADS_PER_GROUP)
    cols = [slice(hd * DSWA_HEAD_DIM, (hd + 1) * DSWA_HEAD_DIM) for hd in heads]
    kc = [jnp.concatenate([kp_ref[:, c], ko_ref[:, c], kn_ref[:, c]], axis=0) for c in cols]
    s = [_dot_nt(q_ref[:, cols[hd]], kc[hd]) * scale + bias_ref[hd] for hd in heads]
    s = [jnp.where(mask, sh, NEG_INF) for sh in s]
    m = [jnp.max(sh, axis=1, keepdims=True) for sh in s]
    p = [jnp.exp(sh - mh).astype(BF16) for sh, mh in zip(s, m)]
    ones = jnp.ones((tw, LANES), BF16)
    vc = [jnp.concatenate(
        [jnp.concatenate([vp_ref[:, c], vo_ref[:, c], vn_ref[:, c]], axis=0), ones], axis=1)
        for c in cols]
    pv = [_dot(p[hd], vc[hd]) for hd in heads]
    lse_tile = jnp.zeros((tq, LANES), F32)
    for hd in heads:
        den = pv[hd][:, DSWA_HEAD_DIM:]
        o_ref[:, cols[hd]] = (pv[hd][:, :DSWA_HEAD_DIM] / den).astype(o_ref.dtype)
        lse_tile = jnp.where(lane == hd, m[hd] + jnp.log(den), lse_tile)
    lse_ref[...] = lse_tile


def _band_attention(qkv, bias, group, dil):
    rows = qkv.shape[0]
    tq = bias.shape[1]
    nt = rows // tq
    per_tile = tq // BAND_HALF
    n_edge = rows // BAND_HALF
    tw = tq + 2 * BAND_HALF

    def own(part):
        return pl.BlockSpec((tq, DSWA_WIDTH), lambda r, n: (n, 3 * r + part))

    def prev(part):
        return pl.BlockSpec((BAND_HALF, DSWA_WIDTH),
                            lambda r, n: (jnp.maximum(n * per_tile - 1, 0), 3 * r + part))

    def nxt(part):
        return pl.BlockSpec((BAND_HALF, DSWA_WIDTH),
                            lambda r, n: (jnp.minimum((n + 1) * per_tile, n_edge - 1), 3 * r + part))

    return pl.pallas_call(
        _band_attn_kernel,
        out_shape=(jax.ShapeDtypeStruct((rows, dil * DSWA_WIDTH), BF16),
                   jax.ShapeDtypeStruct((rows, dil * LANES), F32)),
        grid=(dil, nt),
        in_specs=[own(0),
                  prev(1), own(1), nxt(1),
                  prev(2), own(2), nxt(2),
                  pl.BlockSpec((DSWA_HEADS_PER_GROUP, tq, tw), lambda r, n: (group, 0, 0))],
        out_specs=(pl.BlockSpec((tq, DSWA_WIDTH), lambda r, n: (n, r)),
                   pl.BlockSpec((tq, LANES), lambda r, n: (n, r))),
        compiler_params=_params(("parallel", "parallel")),
        name=f"band_attention_d{dil}",
    )(qkv, qkv, qkv, qkv, qkv, qkv, qkv, bias)


def _mix_kernel(o1_ref, o4_ref, o16_ref, l1_ref, l4_ref, l16_ref, sg_ref, out_ref,
                o4n_ref, l4n_ref, o16n_ref, l16n_ref):
    for dil, o_ref, l_ref, on_ref, ln_ref in ((DILATIONS[1], o4_ref, l4_ref, o4n_ref, l4n_ref),
                                               (DILATIONS[2], o16_ref, l16_ref, o16n_ref, l16n_ref)):
        rows = o_ref.shape[0]
        for r in range(dil):
            for hd in range(DSWA_HEADS_PER_GROUP):
                lo = r * DSWA_WIDTH + hd * DSWA_HEAD_DIM
                on_ref[hd, pl.ds(r, rows, stride=dil), :] = (
                    o_ref[:, lo:lo + DSWA_HEAD_DIM].astype(F32))
            ln_ref[pl.ds(r, rows, stride=dil), :] = l_ref[:, r * LANES:(r + 1) * LANES]
    for hd in range(DSWA_HEADS_PER_GROUP):
        cols = slice(hd * DSWA_HEAD_DIM, (hd + 1) * DSWA_HEAD_DIM)
        l1 = l1_ref[:, hd:hd + 1]
        l2 = l4n_ref[:, hd:hd + 1]
        l3 = l16n_ref[:, hd:hd + 1]
        mx = jnp.maximum(jnp.maximum(l1, l2), l3)
        e1, e2, e3 = jnp.exp(l1 - mx), jnp.exp(l2 - mx), jnp.exp(l3 - mx)
        den = e1 + e2 + e3
        mix = ((e1 / den) * o1_ref[:, cols].astype(F32)
               + (e2 / den) * o4n_ref[hd]
               + (e3 / den) * o16n_ref[hd])
        out_ref[:, cols] = (mix * sg_ref[:, cols].astype(F32)).astype(out_ref.dtype)


def _group_mixture(outs, lses, gates, tm=256):
    o1, o4, o16 = outs
    l1, l4, l16 = lses
    s = o1.shape[0]
    d4, d16 = DILATIONS[1], DILATIONS[2]
    w = DSWA_WIDTH
    return pl.pallas_call(
        _mix_kernel,
        out_shape=jax.ShapeDtypeStruct((s, w), BF16),
        grid=(s // tm,),
        in_specs=[pl.BlockSpec((tm, w), lambda i: (i, 0)),
                  pl.BlockSpec((tm // d4, d4 * w), lambda i: (i, 0)),
                  pl.BlockSpec((tm // d16, d16 * w), lambda i: (i, 0)),
                  pl.BlockSpec((tm, LANES), lambda i: (i, 0)),
                  pl.BlockSpec((tm // d4, d4 * LANES), lambda i: (i, 0)),
                  pl.BlockSpec((tm // d16, d16 * LANES), lambda i: (i, 0)),
                  pl.BlockSpec((tm, w), lambda i: (i, MLA_WIDTH // w))],
        out_specs=pl.BlockSpec((tm, w), lambda i: (i, 0)),
        scratch_shapes=[pltpu.VMEM((DSWA_HEADS_PER_GROUP, tm, DSWA_HEAD_DIM), F32),
                        pltpu.VMEM((tm, LANES), F32),
                        pltpu.VMEM((DSWA_HEADS_PER_GROUP, tm, DSWA_HEAD_DIM), F32),
                        pltpu.VMEM((tm, LANES), F32)],
        compiler_params=_params(("parallel",)),
        name="group_mixture",
    )(o1, o4, o16, l1, l4, l16, gates)


def _merge_kernel(a_ref, sg_ref, b_ref, wm_ref, wd_ref, rm_ref, rd_ref, o_ref):
    a_gated = (a_ref[...].astype(F32) * sg_ref[...].astype(F32)).astype(BF16)
    y_mla = _dot(a_gated, wm_ref[...])
    y_dswa = _dot(b_ref[...], wd_ref[...])
    merged = rm_ref[...].astype(F32) * y_mla + rd_ref[...].astype(F32) * y_dswa
    o_ref[...] = merged.astype(o_ref.dtype)


def _merge_branches(a, b_gated, gates, w_o_mla, w_o_dswa, tm=512, tn=COL_BLOCK):
    s = a.shape[0]
    d = w_o_mla.shape[1]
    r_mla_block = (MLA_WIDTH + DSWA_WIDTH) // tn
    r_dswa_block = (MLA_WIDTH + DSWA_WIDTH + D_MODEL) // tn
    return pl.pallas_call(
        _merge_kernel,
        out_shape=jax.ShapeDtypeStruct((s, d), BF16),
        grid=(d // tn, s // tm),
        in_specs=[pl.BlockSpec((tm, MLA_WIDTH), lambda j, i: (i, 0)),
                  pl.BlockSpec((tm, MLA_WIDTH), lambda j, i: (i, 0)),
                  pl.BlockSpec((tm, DSWA_WIDTH), lambda j, i: (i, 0)),
                  pl.BlockSpec((MLA_WIDTH, tn), lambda j, i: (0, j)),
                  pl.BlockSpec((DSWA_WIDTH, tn), lambda j, i: (0, j)),
                  pl.BlockSpec((tm, tn), lambda j, i: (i, r_mla_block + j)),
                  pl.BlockSpec((tm, tn), lambda j, i: (i, r_dswa_block + j))],
        out_specs=pl.BlockSpec((tm, tn), lambda j, i: (i, j)),
        compiler_params=_params(("parallel", "parallel")),
        name="merge_branches",
    )(a, gates, b_gated, w_o_mla, w_o_dswa, gates, gates)


def _final_ln_kernel(x_ref, o_ref, g0_ref, b0_ref, g_ref, b_ref, out_ref):
    h = _layer_norm_rows(x_ref[...], g0_ref[...], b0_ref[...])
    out_ref[...] = _layer_norm_rows(DEEPNORM_ALPHA * h + o_ref[...], g_ref[...], b_ref[...])


def _final_layer_norm(x, out, g0, b0, g, b, tm=256):
    s, d = x.shape
    row = pl.BlockSpec((tm, d), lambda i: (i, 0))
    vec = pl.BlockSpec((1, d), lambda i: (0, 0))
    return pl.pallas_call(
        _final_ln_kernel,
        out_shape=jax.ShapeDtypeStruct((s, d), F32),
        grid=(s // tm,),
        in_specs=[row, row, vec, vec, vec, vec],
        out_specs=row,
        compiler_params=_params(("parallel",)),
        name="final_ln",
    )(x, out, g0.reshape(1, d), b0.reshape(1, d), g.reshape(1, d), b.reshape(1, d))


def _cast_kernel(w_ref, o_ref):
    o_ref[...] = w_ref[...].astype(o_ref.dtype)


def _cast_rows_bf16(w_t, row0, n_rows, name):
    k = w_t.shape[1]
    tr = math.gcd(MLA_A_COLS, REST_COLS)
    assert row0 % tr == 0 and n_rows % tr == 0 and tr % 16 == 0
    return pl.pallas_call(
        _cast_kernel,
        out_shape=jax.ShapeDtypeStruct((n_rows, k), BF16),
        grid=(n_rows // tr,),
        in_specs=[pl.BlockSpec((tr, k), lambda j: (row0 // tr + j, 0))],
        out_specs=pl.BlockSpec((tr, k), lambda j: (j, 0)),
        compiler_params=_params(("parallel",)),
        name=name,
    )(w_t)


def _rope_tables(seq):
    half = QK_ROPE_DIM // 2
    pos = jnp.arange(seq, dtype=F32)
    inv_freq = 1.0 / (ROPE_THETA ** (jnp.arange(0, QK_ROPE_DIM, 2, dtype=F32) / QK_ROPE_DIM))
    lane_freq = jnp.concatenate([inv_freq, inv_freq, jnp.zeros((LANES - QK_ROPE_DIM,), F32)])
    ang = pos[:, None] * lane_freq[None, :]
    lane = np.arange(LANES)
    on_t1 = jnp.asarray((lane < half).astype(np.float32))
    on_t2 = jnp.asarray(((lane >= half) & (lane < QK_ROPE_DIM)).astype(np.float32))
    cos, sin = jnp.cos(ang), jnp.sin(ang)
    return jnp.concatenate([cos * (on_t1 + on_t2), sin * on_t2, -sin * on_t1], axis=1)


def kernel(x, emb_ln_g, emb_ln_b, rel_bias, w_in, q_a_norm_g, w_q_b, kv_a_norm_g, w_kv_b,
           w_o_mla, w_o_dswa, w_out, ln_g, ln_b):
    assert DEPTH == 1 and x.shape == (1, SEQ, D_MODEL)
    x2 = x[0]

    w_in_t = jnp.swapaxes(w_in, 1, 2)[0]
    w_mla_a = _cast_rows_bf16(w_in_t, 0, MLA_A_COLS, "mla_a_weights_bf16")
    w_q = jnp.pad(w_q_b[0].reshape(Q_LORA_RANK, MLA_HEADS, QK_HEAD_DIM),
                  ((0, 0), (0, 0), (0, QK_PAD_DIM - QK_HEAD_DIM))
                  ).reshape(Q_LORA_RANK, MLA_HEADS * QK_PAD_DIM).astype(BF16)
    w_kv = (w_kv_b[0].reshape(KV_LORA_RANK, MLA_HEADS, 2, QK_NOPE_DIM)
            .transpose(0, 2, 1, 3).reshape(KV_LORA_RANK, 2 * MLA_WIDTH).astype(BF16))

    q_scale = QK_HEAD_DIM ** -0.5 * math.log2(math.e)
    rope_tab = _rope_tables(SEQ)

    h_slabs = _input_layer_norm(x2, emb_ln_g, emb_ln_b)
    h = h_slabs[0]

    qn, ckvn, kpe = _mla_a_proj(h, w_mla_a, q_a_norm_g[0], kv_a_norm_g[0], rope_tab)
    q = _mla_q_proj(qn, w_q, rope_tab, q_scale)
    kv = _proj(ckvn, w_kv, n_out=2 * MLA_WIDTH, w_block=lambda j: j, tm=2048, tn=2 * COL_BLOCK,
               name="mla_kv_proj")
    a, w_rest, w_om, w_od, w_o = _mla_attention(
        q, kv, kpe, [(w_in_t, MLA_A_COLS, REST_COLS), (w_o_mla[0], 0, MLA_WIDTH),
                     (w_o_dswa[0], 0, DSWA_WIDTH), (w_out[0], 0, D_MODEL)])

    gates = _proj(h, w_rest, n_out=GATE_WIDTH, w_block=lambda j: GATE_COL0 + j, gated=True,
                  n_silu_blocks=(MLA_WIDTH + DSWA_WIDTH) // COL_BLOCK, w_transposed=True,
                  name="gate_proj")

    bias = _band_bias(rel_bias, BAND_TILE)
    outs, lses = [], []
    blocks_per_part = DSWA_QKV_WIDTH // COL_BLOCK
    for g, dil in enumerate(DILATIONS):
        qkv = _proj(h_slabs[g], w_rest, n_out=3 * DSWA_WIDTH, dil=dil, w_transposed=True,
                    w_block=lambda j, g=g: j * blocks_per_part + g, name=f"dswa_proj_d{dil}")
        o, lse = _band_attention(qkv, bias, g, dil)
        outs.append(o)
        lses.append(lse)
    b_gated = _group_mixture(outs, lses, gates)

    merged = _merge_branches(a, b_gated, gates, w_om, w_od)
    out = _proj(merged, w_o, n_out=D_MODEL, w_block=lambda j: j, out_dtype=F32, name="out_proj")
    y = _final_layer_norm(x2, out, emb_ln_g, emb_ln_b, ln_g[0], ln_b[0])
    return y[None]
```

```python
import functools
import math

import jax
import jax.numpy as jnp
import numpy as np
from jax import lax
from jax.experimental import pallas as pl
from jax.experimental.pallas import tpu as pltpu

D_MODEL = 4096
SEQ = 8192
DEPTH = 1

MLA_HEADS = 16
Q_LORA_RANK = 1024
KV_LORA_RANK = 512
QK_NOPE_DIM = 128
QK_ROPE_DIM = 64
QK_HEAD_DIM = QK_NOPE_DIM + QK_ROPE_DIM
V_HEAD_DIM = 128
MLA_WIDTH = MLA_HEADS * V_HEAD_DIM
ROPE_THETA = 10000.0

DIL_GROUPS = ((128, 1), (512, 4), (2048, 16))
DILATIONS = tuple(d for _, d in DIL_GROUPS)
DSWA_HEADS_PER_GROUP = 8
DSWA_HEADS = 24
DSWA_HEAD_DIM = 128
DSWA_QKV_WIDTH = DSWA_HEADS * DSWA_HEAD_DIM
DSWA_WIDTH = DSWA_HEADS_PER_GROUP * DSWA_HEAD_DIM
BAND_HALF = 64
BAND_TILE = 256

REL_BUCKETS = 32
REL_MAX_DISTANCE = 1024

DEEPNORM_ALPHA = (2.0 * DEPTH) ** 0.25
LN_EPS = 1e-5
RMS_EPS = 1e-6
NEG_INF = -1e30

IN_SPLITS = (Q_LORA_RANK, KV_LORA_RANK + QK_ROPE_DIM, DSWA_QKV_WIDTH, DSWA_QKV_WIDTH,
             DSWA_QKV_WIDTH, MLA_WIDTH, DSWA_WIDTH, D_MODEL, D_MODEL)
IN_OFFSETS = tuple(int(v) for v in np.cumsum(IN_SPLITS)[:-1])
MLA_A_COLS = IN_OFFSETS[1]
REST_COLS = sum(IN_SPLITS) - MLA_A_COLS

LANES = 128
QK_PAD_DIM = 256
COL_BLOCK = 1024
GATE_COL0 = 3 * DSWA_QKV_WIDTH // COL_BLOCK
GATE_WIDTH = MLA_WIDTH + DSWA_WIDTH + 2 * D_MODEL
VMEM_LIMIT = 56 * 1024 * 1024

BF16 = jnp.bfloat16
F32 = jnp.float32


def _params(sem):
    return pltpu.CompilerParams(dimension_semantics=sem, vmem_limit_bytes=VMEM_LIMIT)


def _dot(a, b):
    return jnp.dot(a, b, preferred_element_type=F32)


def _dot_nt(a, b):
    return lax.dot_general(a, b, (((1,), (1,)), ((), ())), preferred_element_type=F32)


def _sigmoid(x):
    return 0.5 * jnp.tanh(0.5 * x) + 0.5


def _layer_norm_rows(x, g, b):
    mu = jnp.mean(x, axis=-1, keepdims=True)
    xc = x - mu
    var = jnp.mean(xc * xc, axis=-1, keepdims=True)
    return xc * lax.rsqrt(var + LN_EPS) * g + b


def _rms_norm_rows(x, g):
    return x * lax.rsqrt(jnp.mean(x * x, axis=-1, keepdims=True) + RMS_EPS) * g


def _rope_lanes(t, tab):
    c, s1, s2 = tab[:, :LANES], tab[:, LANES:2 * LANES], tab[:, 2 * LANES:]
    return t * c + pltpu.roll(t, 32, 1) * s1 + pltpu.roll(t, LANES - 32, 1) * s2


def _class_gather_matrix(tm, dil):
    per_class = tm // dil
    i = np.arange(tm)
    p = np.zeros((tm, tm), np.float32)
    p[i, (i % per_class) * dil + i // per_class] = 1.0
    return jnp.asarray(p, BF16)


def _ln_kernel(x_ref, g_ref, b_ref, p4_ref, p16_ref, h1_ref, h4_ref, h16_ref):
    d = x_ref.shape[1]
    y = _layer_norm_rows(x_ref[...], g_ref[...], b_ref[...]).astype(BF16)
    h1_ref[...] = y
    for dil, p_ref, slab_ref in ((DILATIONS[1], p4_ref, h4_ref), (DILATIONS[2], p16_ref, h16_ref)):
        rows = slab_ref.shape[0]
        grouped = _dot(p_ref[...], y).astype(BF16)
        for r in range(dil):
            slab_ref[:, r * d:(r + 1) * d] = grouped[r * rows:(r + 1) * rows, :]


def _input_layer_norm(x, g, b, tm=256):
    s, d = x.shape
    d4, d16 = DILATIONS[1], DILATIONS[2]
    return pl.pallas_call(
        _ln_kernel,
        out_shape=(jax.ShapeDtypeStruct((s, d), BF16),
                   jax.ShapeDtypeStruct((s // d4, d4 * d), BF16),
                   jax.ShapeDtypeStruct((s // d16, d16 * d), BF16)),
        grid=(s // tm,),
        in_specs=[pl.BlockSpec((tm, d), lambda i: (i, 0)),
                  pl.BlockSpec((1, d), lambda i: (0, 0)),
                  pl.BlockSpec((1, d), lambda i: (0, 0)),
                  pl.BlockSpec((tm, tm), lambda i: (0, 0)),
                  pl.BlockSpec((tm, tm), lambda i: (0, 0))],
        out_specs=(pl.BlockSpec((tm, d), lambda i: (i, 0)),
                   pl.BlockSpec((tm // d4, d4 * d), lambda i: (i, 0)),
                   pl.BlockSpec((tm // d16, d16 * d), lambda i: (i, 0))),
        compiler_params=_params(("parallel",)),
        name="input_ln",
    )(x, g.reshape(1, d), b.reshape(1, d),
      _class_gather_matrix(tm, d4), _class_gather_matrix(tm, d16))


def _proj_kernel(a_ref, w_ref, o_ref, *, n_silu_blocks, gated, w_transposed):
    acc = (_dot_nt if w_transposed else _dot)(a_ref[...], w_ref[...])
    if gated:
        sig = _sigmoid(acc)
        acc = jnp.where(pl.program_id(0) < n_silu_blocks, acc * sig, sig)
    o_ref[...] = acc.astype(o_ref.dtype)


def _proj(a, w, *, n_out, w_block, dil=1, tm=1024, tn=COL_BLOCK, gated=False, n_silu_blocks=0,
          w_transposed=False, out_dtype=BF16, name):
    rows, kd = a.shape
    k = kd // dil
    nj = n_out // tn
    tm = min(tm, rows)
    kern = functools.partial(_proj_kernel, n_silu_blocks=n_silu_blocks, gated=gated,
                             w_transposed=w_transposed)
    if w_transposed:
        w_spec = pl.BlockSpec((tn, k), lambda j, r, i: (w_block(j), 0))
    else:
        w_spec = pl.BlockSpec((k, tn), lambda j, r, i: (0, w_block(j)))
    return pl.pallas_call(
        kern,
        out_shape=jax.ShapeDtypeStruct((rows, dil * n_out), out_dtype),
        grid=(nj, dil, rows // tm),
        in_specs=[pl.BlockSpec((tm, k), lambda j, r, i: (i, r)), w_spec],
        out_specs=pl.BlockSpec((tm, tn), lambda j, r, i: (i, r * nj + j)),
        compiler_params=_params(("parallel", "parallel", "parallel")),
        name=name,
    )(a, w)


def _mla_a_kernel(h_ref, w_ref, gq_ref, gkv_ref, tab_ref, qn_ref, ckvn_ref, kpe_ref):
    acc = _dot_nt(h_ref[...], w_ref[...])
    qn_ref[...] = _rms_norm_rows(acc[:, :Q_LORA_RANK], gq_ref[...]).astype(BF16)
    ckv = acc[:, Q_LORA_RANK:Q_LORA_RANK + KV_LORA_RANK]
    ckvn_ref[...] = _rms_norm_rows(ckv, gkv_ref[...]).astype(BF16)
    k_pe = acc[:, Q_LORA_RANK + KV_LORA_RANK:]
    k_pe = jnp.concatenate([k_pe, jnp.zeros((k_pe.shape[0], LANES - QK_ROPE_DIM), F32)], axis=1)
    kpe_ref[...] = _rope_lanes(k_pe, tab_ref[...]).astype(BF16)


def _mla_a_proj(h, w, gq, gkv, k_tab, tm=512):
    s, k = h.shape
    return pl.pallas_call(
        _mla_a_kernel,
        out_shape=(jax.ShapeDtypeStruct((s, Q_LORA_RANK), BF16),
                   jax.ShapeDtypeStruct((s, KV_LORA_RANK), BF16),
                   jax.ShapeDtypeStruct((s, LANES), BF16)),
        grid=(s // tm,),
        in_specs=[pl.BlockSpec((tm, k), lambda i: (i, 0)),
                  pl.BlockSpec((MLA_A_COLS, k), lambda i: (0, 0)),
                  pl.BlockSpec((1, Q_LORA_RANK), lambda i: (0, 0)),
                  pl.BlockSpec((1, KV_LORA_RANK), lambda i: (0, 0)),
                  pl.BlockSpec((tm, 3 * LANES), lambda i: (i, 0))],
        out_specs=(pl.BlockSpec((tm, Q_LORA_RANK), lambda i: (i, 0)),
                   pl.BlockSpec((tm, KV_LORA_RANK), lambda i: (i, 0)),
                   pl.BlockSpec((tm, LANES), lambda i: (i, 0))),
        compiler_params=_params(("parallel",)),
        name="mla_a_proj",
    )(h, w, gq.reshape(1, -1), gkv.reshape(1, -1), k_tab)


def _mla_q_kernel(qn_ref, w_ref, tab_ref, q_ref, *, q_scale):
    qn = qn_ref[...]
    tab = tab_ref[...]
    for hd in range(MLA_HEADS):
        lo = hd * QK_PAD_DIM
        acc = _dot(qn, w_ref[:, lo:lo + QK_PAD_DIM]) * q_scale
        q_ref[:, lo:lo + LANES] = acc[:, :LANES].astype(BF16)
        q_ref[:, lo + LANES:lo + QK_PAD_DIM] = _rope_lanes(acc[:, LANES:], tab).astype(BF16)


def _mla_q_proj(qn, w, rope_tab, q_scale, tm=1024):
    s, k = qn.shape
    n = w.shape[1]
    return pl.pallas_call(
        functools.partial(_mla_q_kernel, q_scale=q_scale),
        out_shape=jax.ShapeDtypeStruct((s, n), BF16),
        grid=(s // tm,),
        in_specs=[pl.BlockSpec((tm, k), lambda i: (i, 0)),
                  pl.BlockSpec((k, n), lambda i: (0, 0)),
                  pl.BlockSpec((tm, 3 * LANES), lambda i: (i, 0))],
        out_specs=pl.BlockSpec((tm, n), lambda i: (i, 0)),
        compiler_params=_params(("parallel",)),
        name="mla_q_proj",
    )(qn, w, rope_tab)


ONES_ROWS = 16


def _mla_attn_kernel(q_ref, k_ref, v_ref, kpe_ref, *rest, tk, sub, n_cast):
    o_ref = rest[n_cast]
    kcat_ref, vcat_ref = rest[2 * n_cast + 1:]
    for src_ref, dst_ref in zip(rest[:n_cast], rest[n_cast + 1:2 * n_cast + 1]):
        dst_ref[...] = src_ref[...].astype(dst_ref.dtype)

    @pl.when(pl.program_id(1) == 0)
    def _():
        kcat_ref[:, :LANES] = k_ref[...]
        kcat_ref[:, LANES:] = kpe_ref[...]
        vcat_ref[:V_HEAD_DIM, :] = v_ref[...]
        vcat_ref[V_HEAD_DIM:, :] = jnp.ones((ONES_ROWS, v_ref.shape[1]), BF16)

    n_kv = k_ref.shape[0] // tk
    n_sub = q_ref.shape[0] // sub
    chunks = range(n_sub)
    qs = [q_ref[c * sub:(c + 1) * sub, :] for c in chunks]
    ms = [jnp.full((1, sub), -jnp.inf, F32) for _ in chunks]
    accs = [jnp.zeros((V_HEAD_DIM + ONES_ROWS, sub), F32) for _ in chunks]
    for j in range(n_kv):
        kblk = kcat_ref[j * tk:(j + 1) * tk, :]
        vblk = vcat_ref[:, j * tk:(j + 1) * tk]
        ss = [_dot_nt(kblk, qs[c]) for c in chunks]
        ps = []
        for c in chunks:
            m_new = jnp.maximum(ms[c], jnp.max(ss[c], axis=0, keepdims=True))
            accs[c] = jnp.exp2(ms[c] - m_new) * accs[c]
            ps.append(jnp.exp2(ss[c] - m_new).astype(BF16))
            ms[c] = m_new
        for c in chunks:
            accs[c] = accs[c] + _dot(vblk, ps[c])
    for c in chunks:
        attn_t = accs[c][:V_HEAD_DIM, :] / accs[c][V_HEAD_DIM:V_HEAD_DIM + 1, :]
        o_ref[c * sub:(c + 1) * sub, :] = attn_t.T.astype(o_ref.dtype)


def _mla_attention(q, k_nope, v_t, kpe, casts, tq=2048, tk=1024, sub=512):
    s = q.shape[0]
    n_q = s // tq
    steps = MLA_HEADS * n_q
    cast_in, cast_out, cast_shapes = [], [], []
    for w, row0, rows in casts:
        tr = rows // steps
        assert tr * steps == rows and row0 % tr == 0 and tr % 16 == 0
        cast_in.append(pl.BlockSpec(
            (tr, w.shape[1]), lambda h, i, row0=row0, tr=tr: (row0 // tr + h * n_q + i, 0)))
        cast_out.append(pl.BlockSpec((tr, w.shape[1]), lambda h, i: (h * n_q + i, 0)))
        cast_shapes.append(jax.ShapeDtypeStruct((rows, w.shape[1]), BF16))
    return pl.pallas_call(
        functools.partial(_mla_attn_kernel, tk=tk, sub=sub, n_cast=len(casts)),
        out_shape=(jax.ShapeDtypeStruct((s, MLA_WIDTH), BF16), *cast_shapes),
        grid=(MLA_HEADS, n_q),
        in_specs=[pl.BlockSpec((tq, QK_PAD_DIM), lambda h, i: (i, h)),
                  pl.BlockSpec((s, QK_NOPE_DIM), lambda h, i: (0, h)),
                  pl.BlockSpec((V_HEAD_DIM, s), lambda h, i: (h, 0)),
                  pl.BlockSpec((s, LANES), lambda h, i: (0, 0)),
                  *cast_in],
        out_specs=(pl.BlockSpec((tq, V_HEAD_DIM), lambda h, i: (i, h)), *cast_out),
        scratch_shapes=[pltpu.VMEM((s, QK_PAD_DIM), BF16),
                        pltpu.VMEM((V_HEAD_DIM + ONES_ROWS, s), BF16)],
        compiler_params=_params(("parallel", "arbitrary")),
        name="mla_attention",
    )(q, k_nope, v_t, kpe, *[w for w, _, _ in casts])


def _vt_kernel(w_ref, c_ref, o_ref):
    o_ref[...] = _dot_nt(w_ref[...], c_ref[...]).astype(o_ref.dtype)


def _mla_vt_proj(ckvn, w_v_t, tm=1024):
    s, k = ckvn.shape
    n = w_v_t.shape[0]
    return pl.pallas_call(
        _vt_kernel,
        out_shape=jax.ShapeDtypeStruct((n, s), BF16),
        grid=(s // tm,),
        in_specs=[pl.BlockSpec((n, k), lambda i: (0, 0)),
                  pl.BlockSpec((tm, k), lambda i: (i, 0))],
        out_specs=pl.BlockSpec((n, tm), lambda i: (0, i)),
        compiler_params=_params(("parallel",)),
        name="mla_vt_proj",
    )(w_v_t, ckvn)


def _band_bucket_index(dil, tq):
    nb = REL_BUCKETS // 2
    max_exact = nb // 2
    rel = (np.arange(tq + 2 * BAND_HALF)[None, :] - BAND_HALF - np.arange(tq)[:, None]) * dil
    n = np.abs(rel)
    nf = np.maximum(n, 1).astype(np.float64)
    large = max_exact + (np.log(nf / max_exact) / math.log(REL_MAX_DISTANCE / max_exact)
                         * (nb - max_exact)).astype(np.int32)
    large = np.minimum(large, nb - 1)
    return (np.where(rel > 0, nb, 0) + np.where(n < max_exact, n, large)).astype(np.int32)


def _band_bias_kernel(rb_ref, idx_ref, o_ref):
    hd = pl.program_id(0)
    idx = idx_ref[0]
    acc = jnp.zeros(idx.shape, F32)
    for b in range(REL_BUCKETS):
        acc = jnp.where(idx == b, rb_ref[b, hd], acc)
    o_ref[0] = acc


def _band_bias(rel_bias, tq):
    idx = jnp.asarray(np.stack([_band_bucket_index(dil, tq) for dil in DILATIONS]))
    tw = idx.shape[2]
    return pl.pallas_call(
        _band_bias_kernel,
        out_shape=jax.ShapeDtypeStruct((DSWA_HEADS, tq, tw), F32),
        grid=(DSWA_HEADS,),
        in_specs=[pl.BlockSpec(memory_space=pltpu.SMEM),
                  pl.BlockSpec((1, tq, tw), lambda h: (h // DSWA_HEADS_PER_GROUP, 0, 0))],
        out_specs=pl.BlockSpec((1, tq, tw), lambda h: (h, 0, 0)),
        compiler_params=_params(("parallel",)),
        name="band_bias",
    )(rel_bias, idx)


def _band_attn_kernel(q_ref, kp_ref, ko_ref, kn_ref, vp_ref, vo_ref, vn_ref, bias_ref,
                      o_ref, lse_ref):
    tq = q_ref.shape[0]
    tw = tq + 2 * BAND_HALF
    tile = pl.program_id(1)
    prev_ok = tile != 0
    next_ok = tile != pl.num_programs(1) - 1
    row = lax.broadcasted_iota(jnp.int32, (tq, tw), 0)
    col = lax.broadcasted_iota(jnp.int32, (tq, tw), 1)
    rel = col - BAND_HALF - row
    mask = ((jnp.abs(rel) <= BAND_HALF) & ((col >= BAND_HALF) | prev_ok)
            & ((col < tq + BAND_HALF) | next_ok))
    lane = lax.broadcasted_iota(jnp.int32, (tq, LANES), 1)
    scale = DSWA_HEAD_DIM ** -0.5
    heads = range(DSWA_HEADS_PER_GROUP)
    cols = [slice(hd * DSWA_HEAD_DIM, (hd + 1) * DSWA_HEAD_DIM) for hd in heads]
    kc = [jnp.concatenate([kp_ref[:, c], ko_ref[:, c], kn_ref[:, c]], axis=0) for c in cols]
    s = [_dot_nt(q_ref[:, cols[hd]], kc[hd]) * scale + bias_ref[hd] for hd in heads]
    s = [jnp.where(mask, sh, NEG_INF) for sh in s]
    m = [jnp.max(sh, axis=1, keepdims=True) for sh in s]
    p = [jnp.exp(sh - mh).astype(BF16) for sh, mh in zip(s, m)]
    ones = jnp.ones((tw, LANES), BF16)
    vc = [jnp.concatenate(
        [jnp.concatenate([vp_ref[:, c], vo_ref[:, c], vn_ref[:, c]], axis=0), ones], axis=1)
        for c in cols]
    pv = [_dot(p[hd], vc[hd]) for hd in heads]
    lse_tile = jnp.zeros((tq, LANES), F32)
    for hd in heads:
        den = pv[hd][:, DSWA_HEAD_DIM:]
        o_ref[:, cols[hd]] = (pv[hd][:, :DSWA_HEAD_DIM] / den).astype(o_ref.dtype)
        lse_tile = jnp.where(lane == hd, m[hd] + jnp.log(den), lse_tile)
    lse_ref[...] = lse_tile


def _band_attention(qkv, bias, group, dil):
    rows = qkv.shape[0]
    tq = bias.shape[1]
    nt = rows // tq
    per_tile = tq // BAND_HALF
    n_edge = rows // BAND_HALF
    tw = tq + 2 * BAND_HALF

    def own(part):
        return pl.BlockSpec((tq, DSWA_WIDTH), lambda r, n: (n, 3 * r + part))

    def prev(part):
        return pl.BlockSpec((BAND_HALF, DSWA_WIDTH),
                            lambda r, n: (jnp.maximum(n * per_tile - 1, 0), 3 * r + part))

    def nxt(part):
        return pl.BlockSpec((BAND_HALF, DSWA_WIDTH),
                            lambda r, n: (jnp.minimum((n + 1) * per_tile, n_edge - 1), 3 * r + part))

    return pl.pallas_call(
        _band_attn_kernel,
        out_shape=(jax.ShapeDtypeStruct((rows, dil * DSWA_WIDTH), BF16),
                   jax.ShapeDtypeStruct((rows, dil * LANES), F32)),
        grid=(dil, nt),
        in_specs=[own(0),
                  prev(1), own(1), nxt(1),
                  prev(2), own(2), nxt(2),
                  pl.BlockSpec((DSWA_HEADS_PER_GROUP, tq, tw), lambda r, n: (group, 0, 0))],
        out_specs=(pl.BlockSpec((tq, DSWA_WIDTH), lambda r, n: (n, r)),
                   pl.BlockSpec((tq, LANES), lambda r, n: (n, r))),
        compiler_params=_params(("parallel", "parallel")),
        name=f"band_attention_d{dil}",
    )(qkv, qkv, qkv, qkv, qkv, qkv, qkv, bias)


def _mix_kernel(o1_ref, o4_ref, o16_ref, l1_ref, l4_ref, l16_ref, sg_ref, out_ref,
                o4n_ref, l4n_ref, o16n_ref, l16n_ref):
    for dil, o_ref, l_ref, on_ref, ln_ref in ((DILATIONS[1], o4_ref, l4_ref, o4n_ref, l4n_ref),
                                               (DILATIONS[2], o16_ref, l16_ref, o16n_ref, l16n_ref)):
        rows = o_ref.shape[0]
        for r in range(dil):
            for hd in range(DSWA_HEADS_PER_GROUP):
                lo = r * DSWA_WIDTH + hd * DSWA_HEAD_DIM
                on_ref[hd, pl.ds(r, rows, stride=dil), :] = (
                    o_ref[:, lo:lo + DSWA_HEAD_DIM].astype(F32))
            ln_ref[pl.ds(r, rows, stride=dil), :] = l_ref[:, r * LANES:(r + 1) * LANES]
    for hd in range(DSWA_HEADS_PER_GROUP):
        cols = slice(hd * DSWA_HEAD_DIM, (hd + 1) * DSWA_HEAD_DIM)
        l1 = l1_ref[:, hd:hd + 1]
        l2 = l4n_ref[:, hd:hd + 1]
        l3 = l16n_ref[:, hd:hd + 1]
        mx = jnp.maximum(jnp.maximum(l1, l2), l3)
        e1, e2, e3 = jnp.exp(l1 - mx), jnp.exp(l2 - mx), jnp.exp(l3 - mx)
        den = e1 + e2 + e3
        mix = ((e1 / den) * o1_ref[:, cols].astype(F32)
               + (e2 / den) * o4n_ref[hd]
               + (e3 / den) * o16n_ref[hd])
        out_ref[:, cols] = (mix * sg_ref[:, cols].astype(F32)).astype(out_ref.dtype)


def _group_mixture(outs, lses, gates, tm=256):
    o1, o4, o16 = outs
    l1, l4, l16 = lses
    s = o1.shape[0]
    d4, d16 = DILATIONS[1], DILATIONS[2]
    w = DSWA_WIDTH
    return pl.pallas_call(
        _mix_kernel,
        out_shape=jax.ShapeDtypeStruct((s, w), BF16),
        grid=(s // tm,),
        in_specs=[pl.BlockSpec((tm, w), lambda i: (i, 0)),
                  pl.BlockSpec((tm // d4, d4 * w), lambda i: (i, 0)),
                  pl.BlockSpec((tm // d16, d16 * w), lambda i: (i, 0)),
                  pl.BlockSpec((tm, LANES), lambda i: (i, 0)),
                  pl.BlockSpec((tm // d4, d4 * LANES), lambda i: (i, 0)),
                  pl.BlockSpec((tm // d16, d16 * LANES), lambda i: (i, 0)),
                  pl.BlockSpec((tm, w), lambda i: (i, MLA_WIDTH // w))],
        out_specs=pl.BlockSpec((tm, w), lambda i: (i, 0)),
        scratch_shapes=[pltpu.VMEM((DSWA_HEADS_PER_GROUP, tm, DSWA_HEAD_DIM), F32),
                        pltpu.VMEM((tm, LANES), F32),
                        pltpu.VMEM((DSWA_HEADS_PER_GROUP, tm, DSWA_HEAD_DIM), F32),
                        pltpu.VMEM((tm, LANES), F32)],
        compiler_params=_params(("parallel",)),
        name="group_mixture",
    )(o1, o4, o16, l1, l4, l16, gates)


def _merge_kernel(a_ref, sg_ref, b_ref, wm_ref, wd_ref, rm_ref, rd_ref, o_ref):
    a_gated = (a_ref[...].astype(F32) * sg_ref[...].astype(F32)).astype(BF16)
    y_mla = _dot(a_gated, wm_ref[...])
    y_dswa = _dot(b_ref[...], wd_ref[...])
    merged = rm_ref[...].astype(F32) * y_mla + rd_ref[...].astype(F32) * y_dswa
    o_ref[...] = merged.astype(o_ref.dtype)


def _merge_branches(a, b_gated, gates, w_o_mla, w_o_dswa, tm=512, tn=COL_BLOCK):
    s = a.shape[0]
    d = w_o_mla.shape[1]
    r_mla_block = (MLA_WIDTH + DSWA_WIDTH) // tn
    r_dswa_block = (MLA_WIDTH + DSWA_WIDTH + D_MODEL) // tn
    return pl.pallas_call(
        _merge_kernel,
        out_shape=jax.ShapeDtypeStruct((s, d), BF16),
        grid=(d // tn, s // tm),
        in_specs=[pl.BlockSpec((tm, MLA_WIDTH), lambda j, i: (i, 0)),
                  pl.BlockSpec((tm, MLA_WIDTH), lambda j, i: (i, 0)),
                  pl.BlockSpec((tm, DSWA_WIDTH), lambda j, i: (i, 0)),
                  pl.BlockSpec((MLA_WIDTH, tn), lambda j, i: (0, j)),
                  pl.BlockSpec((DSWA_WIDTH, tn), lambda j, i: (0, j)),
                  pl.BlockSpec((tm, tn), lambda j, i: (i, r_mla_block + j)),
                  pl.BlockSpec((tm, tn), lambda j, i: (i, r_dswa_block + j))],
        out_specs=pl.BlockSpec((tm, tn), lambda j, i: (i, j)),
        compiler_params=_params(("parallel", "parallel")),
        name="merge_branches",
    )(a, gates, b_gated, w_o_mla, w_o_dswa, gates, gates)


def _final_ln_kernel(x_ref, o_ref, g0_ref, b0_ref, g_ref, b_ref, out_ref):
    h = _layer_norm_rows(x_ref[...], g0_ref[...], b0_ref[...])
    out_ref[...] = _layer_norm_rows(DEEPNORM_ALPHA * h + o_ref[...], g_ref[...], b_ref[...])


def _final_layer_norm(x, out, g0, b0, g, b, tm=256):
    s, d = x.shape
    row = pl.BlockSpec((tm, d), lambda i: (i, 0))
    vec = pl.BlockSpec((1, d), lambda i: (0, 0))
    return pl.pallas_call(
        _final_ln_kernel,
        out_shape=jax.ShapeDtypeStruct((s, d), F32),
        grid=(s // tm,),
        in_specs=[row, row, vec, vec, vec, vec],
        out_specs=row,
        compiler_params=_params(("parallel",)),
        name="final_ln",
    )(x, out, g0.reshape(1, d), b0.reshape(1, d), g.reshape(1, d), b.reshape(1, d))


def _cast_kernel(w_ref, o_ref):
    o_ref[...] = w_ref[...].astype(o_ref.dtype)


def _cast_rows_bf16(w_t, row0, n_rows, name):
    k = w_t.shape[1]
    tr = math.gcd(MLA_A_COLS, REST_COLS)
    assert row0 % tr == 0 and n_rows % tr == 0 and tr % 16 == 0
    return pl.pallas_call(
        _cast_kernel,
        out_shape=jax.ShapeDtypeStruct((n_rows, k), BF16),
        grid=(n_rows // tr,),
        in_specs=[pl.BlockSpec((tr, k), lambda j: (row0 // tr + j, 0))],
        out_specs=pl.BlockSpec((tr, k), lambda j: (j, 0)),
        compiler_params=_params(("parallel",)),
        name=name,
    )(w_t)


def _rope_tables(seq):
    half = QK_ROPE_DIM // 2
    pos = jnp.arange(seq, dtype=F32)
    inv_freq = 1.0 / (ROPE_THETA ** (jnp.arange(0, QK_ROPE_DIM, 2, dtype=F32) / QK_ROPE_DIM))
    lane_freq = jnp.concatenate([inv_freq, inv_freq, jnp.zeros((LANES - QK_ROPE_DIM,), F32)])
    ang = pos[:, None] * lane_freq[None, :]
    lane = np.arange(LANES)
    on_t1 = jnp.asarray((lane < half).astype(np.float32))
    on_t2 = jnp.asarray(((lane >= half) & (lane < QK_ROPE_DIM)).astype(np.float32))
    cos, sin = jnp.cos(ang), jnp.sin(ang)
    return jnp.concatenate([cos * (on_t1 + on_t2), sin * on_t2, -sin * on_t1], axis=1)


def kernel(x, emb_ln_g, emb_ln_b, rel_bias, w_in, q_a_norm_g, w_q_b, kv_a_norm_g, w_kv_b,
           w_o_mla, w_o_dswa, w_out, ln_g, ln_b):
    assert DEPTH == 1 and x.shape == (1, SEQ, D_MODEL)
    x2 = x[0]

    w_in_t = jnp.swapaxes(w_in, 1, 2)[0]
    w_mla_a = _cast_rows_bf16(w_in_t, 0, MLA_A_COLS, "mla_a_weights_bf16")
    w_q = jnp.pad(w_q_b[0].reshape(Q_LORA_RANK, MLA_HEADS, QK_HEAD_DIM),
                  ((0, 0), (0, 0), (0, QK_PAD_DIM - QK_HEAD_DIM))
                  ).reshape(Q_LORA_RANK, MLA_HEADS * QK_PAD_DIM).astype(BF16)
    w_kv_heads = w_kv_b[0].reshape(KV_LORA_RANK, MLA_HEADS, 2, QK_NOPE_DIM)
    w_k = w_kv_heads[:, :, 0].reshape(KV_LORA_RANK, MLA_WIDTH).astype(BF16)
    w_v_t = w_kv_heads[:, :, 1].reshape(KV_LORA_RANK, MLA_WIDTH).T.astype(BF16)

    q_scale = QK_HEAD_DIM ** -0.5 * math.log2(math.e)
    rope_tab = _rope_tables(SEQ)

    h_slabs = _input_layer_norm(x2, emb_ln_g, emb_ln_b)
    h = h_slabs[0]

    qn, ckvn, kpe = _mla_a_proj(h, w_mla_a, q_a_norm_g[0], kv_a_norm_g[0], rope_tab)
    q = _mla_q_proj(qn, w_q, rope_tab, q_scale)
    k_nope = _proj(ckvn, w_k, n_out=MLA_WIDTH, w_block=lambda j: j, tm=2048, tn=2 * COL_BLOCK,
                   name="mla_k_proj")
    v_t = _mla_vt_proj(ckvn, w_v_t)
    a, w_rest, w_om, w_od, w_o = _mla_attention(
        q, k_nope, v_t, kpe, [(w_in_t, MLA_A_COLS, REST_COLS), (w_o_mla[0], 0, MLA_WIDTH),
                     (w_o_dswa[0], 0, DSWA_WIDTH), (w_out[0], 0, D_MODEL)])

    gates = _proj(h, w_rest, n_out=GATE_WIDTH, w_block=lambda j: GATE_COL0 + j, gated=True,
                  n_silu_blocks=(MLA_WIDTH + DSWA_WIDTH) // COL_BLOCK, w_transposed=True,
                  name="gate_proj")

    bias = _band_bias(rel_bias, BAND_TILE)
    outs, lses = [], []
    blocks_per_part = DSWA_QKV_WIDTH // COL_BLOCK
    for g, dil in enumerate(DILATIONS):
        qkv = _proj(h_slabs[g], w_rest, n_out=3 * DSWA_WIDTH, dil=dil, w_transposed=True,
                    w_block=lambda j, g=g: j * blocks_per_part + g, name=f"dswa_proj_d{dil}")
        o, lse = _band_attention(qkv, bias, g, dil)
        outs.append(o)
        lses.append(lse)
    b_gated = _group_mixture(outs, lses, gates)

    merged = _merge_branches(a, b_gated, gates, w_om, w_od)
    out = _proj(merged, w_o, n_out=D_MODEL, w_block=lambda j: j, out_dtype=F32, name="out_proj")
    y = _final_layer_norm(x2, out, emb_ln_g, emb_ln_b, ln_g[0], ln_b[0])
    return y[None]
```

```python
import functools
import math

import jax
import jax.numpy as jnp
import numpy as np
from jax import lax
from jax.experimental import pallas as pl
from jax.experimental.pallas import tpu as pltpu

D_MODEL = 4096
SEQ = 8192
DEPTH = 1

MLA_HEADS = 16
Q_LORA_RANK = 1024
KV_LORA_RANK = 512
QK_NOPE_DIM = 128
QK_ROPE_DIM = 64
QK_HEAD_DIM = QK_NOPE_DIM + QK_ROPE_DIM
V_HEAD_DIM = 128
MLA_WIDTH = MLA_HEADS * V_HEAD_DIM
ROPE_THETA = 10000.0

DIL_GROUPS = ((128, 1), (512, 4), (2048, 16))
DILATIONS = tuple(d for _, d in DIL_GROUPS)
DSWA_HEADS_PER_GROUP = 8
DSWA_HEADS = 24
DSWA_HEAD_DIM = 128
DSWA_QKV_WIDTH = DSWA_HEADS * DSWA_HEAD_DIM
DSWA_WIDTH = DSWA_HEADS_PER_GROUP * DSWA_HEAD_DIM
BAND_HALF = 64
BAND_TILE = 256

REL_BUCKETS = 32
REL_MAX_DISTANCE = 1024

DEEPNORM_ALPHA = (2.0 * DEPTH) ** 0.25
LN_EPS = 1e-5
RMS_EPS = 1e-6
NEG_INF = -1e30

IN_SPLITS = (Q_LORA_RANK, KV_LORA_RANK + QK_ROPE_DIM, DSWA_QKV_WIDTH, DSWA_QKV_WIDTH,
             DSWA_QKV_WIDTH, MLA_WIDTH, DSWA_WIDTH, D_MODEL, D_MODEL)
IN_OFFSETS = tuple(int(v) for v in np.cumsum(IN_SPLITS)[:-1])
MLA_A_COLS = IN_OFFSETS[1]
REST_COLS = sum(IN_SPLITS) - MLA_A_COLS

LANES = 128
QK_PAD_DIM = 256
COL_BLOCK = 1024
GATE_COL0 = 3 * DSWA_QKV_WIDTH // COL_BLOCK
GATE_WIDTH = MLA_WIDTH + DSWA_WIDTH + 2 * D_MODEL
VMEM_LIMIT = 56 * 1024 * 1024

BF16 = jnp.bfloat16
F32 = jnp.float32


def _params(sem):
    return pltpu.CompilerParams(dimension_semantics=sem, vmem_limit_bytes=VMEM_LIMIT)


def _dot(a, b):
    return jnp.dot(a, b, preferred_element_type=F32)


def _dot_nt(a, b):
    return lax.dot_general(a, b, (((1,), (1,)), ((), ())), preferred_element_type=F32)


def _sigmoid(x):
    return 0.5 * jnp.tanh(0.5 * x) + 0.5


def _layer_norm_rows(x, g, b):
    mu = jnp.mean(x, axis=-1, keepdims=True)
    xc = x - mu
    var = jnp.mean(xc * xc, axis=-1, keepdims=True)
    return xc * lax.rsqrt(var + LN_EPS) * g + b


def _rms_norm_rows(x, g):
    return x * lax.rsqrt(jnp.mean(x * x, axis=-1, keepdims=True) + RMS_EPS) * g


def _rope_lanes(t, tab):
    c, s1, s2 = tab[:, :LANES], tab[:, LANES:2 * LANES], tab[:, 2 * LANES:]
    return t * c + pltpu.roll(t, 32, 1) * s1 + pltpu.roll(t, LANES - 32, 1) * s2


def _class_gather_matrix(tm, dil):
    per_class = tm // dil
    i = np.arange(tm)
    p = np.zeros((tm, tm), np.float32)
    p[i, (i % per_class) * dil + i // per_class] = 1.0
    return jnp.asarray(p, BF16)


def _ln_kernel(x_ref, g_ref, b_ref, p4_ref, p16_ref, wq_ref, wkv_ref,
               h1_ref, h4_ref, h16_ref, wq_bf_ref, wkv_bf_ref):
    wq_bf_ref[...] = wq_ref[...].astype(BF16)
    wkv_bf_ref[...] = wkv_ref[...].astype(BF16)
    d = x_ref.shape[1]
    y = _layer_norm_rows(x_ref[...], g_ref[...], b_ref[...]).astype(BF16)
    h1_ref[...] = y
    for dil, p_ref, slab_ref in ((DILATIONS[1], p4_ref, h4_ref), (DILATIONS[2], p16_ref, h16_ref)):
        rows = slab_ref.shape[0]
        grouped = _dot(p_ref[...], y).astype(BF16)
        for r in range(dil):
            slab_ref[:, r * d:(r + 1) * d] = grouped[r * rows:(r + 1) * rows, :]


def _input_layer_norm(x, g, b, w_q_t, w_kv, tm=256):
    s, d = x.shape
    d4, d16 = DILATIONS[1], DILATIONS[2]
    steps = s // tm
    rq, rkv = w_q_t.shape[0] // steps, w_kv.shape[0] // steps
    assert rq * steps == w_q_t.shape[0] and rkv * steps == w_kv.shape[0]
    assert rq % 16 == 0 and rkv % 16 == 0
    return pl.pallas_call(
        _ln_kernel,
        out_shape=(jax.ShapeDtypeStruct((s, d), BF16),
                   jax.ShapeDtypeStruct((s // d4, d4 * d), BF16),
                   jax.ShapeDtypeStruct((s // d16, d16 * d), BF16),
                   jax.ShapeDtypeStruct(w_q_t.shape, BF16),
                   jax.ShapeDtypeStruct(w_kv.shape, BF16)),
        grid=(steps,),
        in_specs=[pl.BlockSpec((tm, d), lambda i: (i, 0)),
                  pl.BlockSpec((1, d), lambda i: (0, 0)),
                  pl.BlockSpec((1, d), lambda i: (0, 0)),
                  pl.BlockSpec((tm, tm), lambda i: (0, 0)),
                  pl.BlockSpec((tm, tm), lambda i: (0, 0)),
                  pl.BlockSpec((rq, w_q_t.shape[1]), lambda i: (i, 0)),
                  pl.BlockSpec((rkv, w_kv.shape[1]), lambda i: (i, 0))],
        out_specs=(pl.BlockSpec((tm, d), lambda i: (i, 0)),
                   pl.BlockSpec((tm // d4, d4 * d), lambda i: (i, 0)),
                   pl.BlockSpec((tm // d16, d16 * d), lambda i: (i, 0)),
                   pl.BlockSpec((rq, w_q_t.shape[1]), lambda i: (i, 0)),
                   pl.BlockSpec((rkv, w_kv.shape[1]), lambda i: (i, 0))),
        compiler_params=_params(("parallel",)),
        name="input_ln",
    )(x, g.reshape(1, d), b.reshape(1, d),
      _class_gather_matrix(tm, d4), _class_gather_matrix(tm, d16), w_q_t, w_kv)


def _proj_kernel(a_ref, w_ref, o_ref, *, n_silu_blocks, gated, w_transposed):
    acc = (_dot_nt if w_transposed else _dot)(a_ref[...], w_ref[...])
    if gated:
        sig = _sigmoid(acc)
        acc = jnp.where(pl.program_id(0) < n_silu_blocks, acc * sig, sig)
    o_ref[...] = acc.astype(o_ref.dtype)


def _proj(a, w, *, n_out, w_block, dil=1, tm=1024, tn=COL_BLOCK, gated=False, n_silu_blocks=0,
          w_transposed=False, out_dtype=BF16, name):
    rows, kd = a.shape
    k = kd // dil
    nj = n_out // tn
    tm = min(tm, rows)
    kern = functools.partial(_proj_kernel, n_silu_blocks=n_silu_blocks, gated=gated,
                             w_transposed=w_transposed)
    if w_transposed:
        w_spec = pl.BlockSpec((tn, k), lambda j, r, i: (w_block(j), 0))
    else:
        w_spec = pl.BlockSpec((k, tn), lambda j, r, i: (0, w_block(j)))
    return pl.pallas_call(
        kern,
        out_shape=jax.ShapeDtypeStruct((rows, dil * n_out), out_dtype),
        grid=(nj, dil, rows // tm),
        in_specs=[pl.BlockSpec((tm, k), lambda j, r, i: (i, r)), w_spec],
        out_specs=pl.BlockSpec((tm, tn), lambda j, r, i: (i, r * nj + j)),
        compiler_params=_params(("parallel", "parallel", "parallel")),
        name=name,
    )(a, w)


def _mla_a_kernel(h_ref, w_ref, gq_ref, gkv_ref, tab_ref, qn_ref, ckvn_ref, kpe_ref):
    acc = _dot_nt(h_ref[...], w_ref[...])
    qn_ref[...] = _rms_norm_rows(acc[:, :Q_LORA_RANK], gq_ref[...]).astype(BF16)
    ckv = acc[:, Q_LORA_RANK:Q_LORA_RANK + KV_LORA_RANK]
    ckvn_ref[...] = _rms_norm_rows(ckv, gkv_ref[...]).astype(BF16)
    k_pe = acc[:, Q_LORA_RANK + KV_LORA_RANK:]
    k_pe = jnp.concatenate([k_pe, jnp.zeros((k_pe.shape[0], LANES - QK_ROPE_DIM), F32)], axis=1)
    kpe_ref[...] = _rope_lanes(k_pe, tab_ref[...]).astype(BF16)


def _mla_a_proj(h, w, gq, gkv, k_tab, tm=512):
    s, k = h.shape
    return pl.pallas_call(
        _mla_a_kernel,
        out_shape=(jax.ShapeDtypeStruct((s, Q_LORA_RANK), BF16),
                   jax.ShapeDtypeStruct((s, KV_LORA_RANK), BF16),
                   jax.ShapeDtypeStruct((s, LANES), BF16)),
        grid=(s // tm,),
        in_specs=[pl.BlockSpec((tm, k), lambda i: (i, 0)),
                  pl.BlockSpec((MLA_A_COLS, k), lambda i: (0, 0)),
                  pl.BlockSpec((1, Q_LORA_RANK), lambda i: (0, 0)),
                  pl.BlockSpec((1, KV_LORA_RANK), lambda i: (0, 0)),
                  pl.BlockSpec((tm, 3 * LANES), lambda i: (i, 0))],
        out_specs=(pl.BlockSpec((tm, Q_LORA_RANK), lambda i: (i, 0)),
                   pl.BlockSpec((tm, KV_LORA_RANK), lambda i: (i, 0)),
                   pl.BlockSpec((tm, LANES), lambda i: (i, 0))),
        compiler_params=_params(("parallel",)),
        name="mla_a_proj",
    )(h, w, gq.reshape(1, -1), gkv.reshape(1, -1), k_tab)


def _mla_q_kernel(qn_ref, w_ref, tab_ref, q_ref, *, q_scale):
    qn = qn_ref[...]
    tab = tab_ref[...]
    pad = jnp.zeros((qn.shape[0], LANES - QK_ROPE_DIM), F32)
    for hd in range(MLA_HEADS):
        lo = hd * QK_PAD_DIM
        acc = _dot_nt(qn, w_ref[hd * QK_HEAD_DIM:(hd + 1) * QK_HEAD_DIM, :]) * q_scale
        q_ref[:, lo:lo + LANES] = acc[:, :QK_NOPE_DIM].astype(BF16)
        q_pe = jnp.concatenate([acc[:, QK_NOPE_DIM:], pad], axis=1)
        q_ref[:, lo + LANES:lo + QK_PAD_DIM] = _rope_lanes(q_pe, tab).astype(BF16)


def _mla_q_proj(qn, w_t, rope_tab, q_scale, tm=1024):
    s, k = qn.shape
    n = MLA_HEADS * QK_PAD_DIM
    return pl.pallas_call(
        functools.partial(_mla_q_kernel, q_scale=q_scale),
        out_shape=jax.ShapeDtypeStruct((s, n), BF16),
        grid=(s // tm,),
        in_specs=[pl.BlockSpec((tm, k), lambda i: (i, 0)),
                  pl.BlockSpec(w_t.shape, lambda i: (0, 0)),
                  pl.BlockSpec((tm, 3 * LANES), lambda i: (i, 0))],
        out_specs=pl.BlockSpec((tm, n), lambda i: (i, 0)),
        compiler_params=_params(("parallel",)),
        name="mla_q_proj",
    )(qn, w_t, rope_tab)


def _mla_attn_kernel(q_ref, k_ref, v_ref, kpe_ref, *rest, tk, sub, n_cast):
    o_ref = rest[n_cast]
    kcat_ref, vcat_ref = rest[2 * n_cast + 1:]
    for src_ref, dst_ref in zip(rest[:n_cast], rest[n_cast + 1:2 * n_cast + 1]):
        dst_ref[...] = src_ref[...].astype(dst_ref.dtype)

    @pl.when(pl.program_id(1) == 0)
    def _():
        kcat_ref[:, :LANES] = k_ref[...]
        kcat_ref[:, LANES:] = kpe_ref[...]
        vcat_ref[:, :V_HEAD_DIM] = v_ref[...]
        vcat_ref[:, V_HEAD_DIM:] = jnp.ones((v_ref.shape[0], LANES), BF16)

    n_kv = k_ref.shape[0] // tk
    n_sub = q_ref.shape[0] // sub
    qs = [q_ref[c * sub:(c + 1) * sub, :] for c in range(n_sub)]
    ms = [jnp.full((sub, 1), -jnp.inf, F32) for _ in range(n_sub)]
    accs = [jnp.zeros((sub, V_HEAD_DIM + LANES), F32) for _ in range(n_sub)]
    for j in range(n_kv):
        kblk = kcat_ref[j * tk:(j + 1) * tk, :]
        vblk = vcat_ref[j * tk:(j + 1) * tk, :]
        for c in range(n_sub):
            s = _dot_nt(qs[c], kblk)
            m_new = jnp.maximum(ms[c], jnp.max(s, axis=1, keepdims=True))
            alpha = jnp.exp2(ms[c] - m_new)
            p = jnp.exp2(s - m_new).astype(BF16)
            accs[c] = alpha * accs[c] + _dot(p, vblk)
            ms[c] = m_new
    for c in range(n_sub):
        rows = slice(c * sub, (c + 1) * sub)
        attn = accs[c][:, :V_HEAD_DIM] / accs[c][:, V_HEAD_DIM:]
        o_ref[rows, :] = attn.astype(o_ref.dtype)


def _mla_attention(q, kv, kpe, casts, tq=2048, tk=1024, sub=512):
    s = q.shape[0]
    n_q = s // tq
    steps = MLA_HEADS * n_q
    cast_in, cast_out, cast_shapes = [], [], []
    for w, row0, rows in casts:
        tr = rows // steps
        assert tr * steps == rows and row0 % tr == 0 and tr % 16 == 0
        cast_in.append(pl.BlockSpec(
            (tr, w.shape[1]), lambda h, i, row0=row0, tr=tr: (row0 // tr + h * n_q + i, 0)))
        cast_out.append(pl.BlockSpec((tr, w.shape[1]), lambda h, i: (h * n_q + i, 0)))
        cast_shapes.append(jax.ShapeDtypeStruct((rows, w.shape[1]), BF16))
    return pl.pallas_call(
        functools.partial(_mla_attn_kernel, tk=tk, sub=sub, n_cast=len(casts)),
        out_shape=(jax.ShapeDtypeStruct((s, MLA_WIDTH), BF16), *cast_shapes),
        grid=(MLA_HEADS, n_q),
        in_specs=[pl.BlockSpec((tq, QK_PAD_DIM), lambda h, i: (i, h)),
                  pl.BlockSpec((s, QK_NOPE_DIM), lambda h, i: (0, 2 * h)),
                  pl.BlockSpec((s, V_HEAD_DIM), lambda h, i: (0, 2 * h + 1)),
                  pl.BlockSpec((s, LANES), lambda h, i: (0, 0)),
                  *cast_in],
        out_specs=(pl.BlockSpec((tq, V_HEAD_DIM), lambda h, i: (i, h)), *cast_out),
        scratch_shapes=[pltpu.VMEM((s, QK_PAD_DIM), BF16),
                        pltpu.VMEM((s, V_HEAD_DIM + LANES), BF16)],
        compiler_params=_params(("parallel", "arbitrary")),
        name="mla_attention",
    )(q, kv, kv, kpe, *[w for w, _, _ in casts])


def _band_bucket_index(dil, tq):
    nb = REL_BUCKETS // 2
    max_exact = nb // 2
    rel = (np.arange(tq + 2 * BAND_HALF)[None, :] - BAND_HALF - np.arange(tq)[:, None]) * dil
    n = np.abs(rel)
    nf = np.maximum(n, 1).astype(np.float64)
    large = max_exact + (np.log(nf / max_exact) / math.log(REL_MAX_DISTANCE / max_exact)
                         * (nb - max_exact)).astype(np.int32)
    large = np.minimum(large, nb - 1)
    return (np.where(rel > 0, nb, 0) + np.where(n < max_exact, n, large)).astype(np.int32)


def _band_bias_kernel(rb_ref, idx_ref, o_ref):
    hd = pl.program_id(0)
    idx = idx_ref[0]
    acc = jnp.zeros(idx.shape, F32)
    for b in range(REL_BUCKETS):
        acc = jnp.where(idx == b, rb_ref[b, hd], acc)
    o_ref[0] = acc


def _band_bias(rel_bias, tq):
    idx = jnp.asarray(np.stack([_band_bucket_index(dil, tq) for dil in DILATIONS]))
    tw = idx.shape[2]
    return pl.pallas_call(
        _band_bias_kernel,
        out_shape=jax.ShapeDtypeStruct((DSWA_HEADS, tq, tw), F32),
        grid=(DSWA_HEADS,),
        in_specs=[pl.BlockSpec(memory_space=pltpu.SMEM),
                  pl.BlockSpec((1, tq, tw), lambda h: (h // DSWA_HEADS_PER_GROUP, 0, 0))],
        out_specs=pl.BlockSpec((1, tq, tw), lambda h: (h, 0, 0)),
        compiler_params=_params(("parallel",)),
        name="band_bias",
    )(rel_bias, idx)


def _band_attn_kernel(q_ref, kp_ref, ko_ref, kn_ref, vp_ref, vo_ref, vn_ref, bias_ref,
                      o_ref, lse_ref):
    tq = q_ref.shape[0]
    tw = tq + 2 * BAND_HALF
    tile = pl.program_id(1)
    prev_ok = tile != 0
    next_ok = tile != pl.num_programs(1) - 1
    row = lax.broadcasted_iota(jnp.int32, (tq, tw), 0)
    col = lax.broadcasted_iota(jnp.int32, (tq, tw), 1)
    rel = col - BAND_HALF - row
    mask = ((jnp.abs(rel) <= BAND_HALF) & ((col >= BAND_HALF) | prev_ok)
            & ((col < tq + BAND_HALF) | next_ok))
    lane = lax.broadcasted_iota(jnp.int32, (tq, LANES), 1)
    scale = DSWA_HEAD_DIM ** -0.5
    heads = range(DSWA_HEADS_PER_GROUP)
    cols = [slice(hd * DSWA_HEAD_DIM, (hd + 1) * DSWA_HEAD_DIM) for hd in heads]
    kc = [jnp.concatenate([kp_ref[:, c], ko_ref[:, c], kn_ref[:, c]], axis=0) for c in cols]
    s = [_dot_nt(q_ref[:, cols[hd]], kc[hd]) * scale + bias_ref[hd] for hd in heads]
    s = [jnp.where(mask, sh, NEG_INF) for sh in s]
    m = [jnp.max(sh, axis=1, keepdims=True) for sh in s]
    p = [jnp.exp(sh - mh).astype(BF16) for sh, mh in zip(s, m)]
    ones = jnp.ones((tw, LANES), BF16)
    vc = [jnp.concatenate(
        [jnp.concatenate([vp_ref[:, c], vo_ref[:, c], vn_ref[:, c]], axis=0), ones], axis=1)
        for c in cols]
    pv = [_dot(p[hd], vc[hd]) for hd in heads]
    lse_tile = jnp.zeros((tq, LANES), F32)
    for hd in heads:
        den = pv[hd][:, DSWA_HEAD_DIM:]
        o_ref[:, cols[hd]] = (pv[hd][:, :DSWA_HEAD_DIM] / den).astype(o_ref.dtype)
        lse_tile = jnp.where(lane == hd, m[hd] + jnp.log(den), lse_tile)
    lse_ref[...] = lse_tile


def _band_attention(qkv, bias, group, dil):
    rows = qkv.shape[0]
    tq = bias.shape[1]
    nt = rows // tq
    per_tile = tq // BAND_HALF
    n_edge = rows // BAND_HALF
    tw = tq + 2 * BAND_HALF

    def own(part):
        return pl.BlockSpec((tq, DSWA_WIDTH), lambda r, n: (n, 3 * r + part))

    def prev(part):
        return pl.BlockSpec((BAND_HALF, DSWA_WIDTH),
                            lambda r, n: (jnp.maximum(n * per_tile - 1, 0), 3 * r + part))

    def nxt(part):
        return pl.BlockSpec((BAND_HALF, DSWA_WIDTH),
                            lambda r, n: (jnp.minimum((n + 1) * per_tile, n_edge - 1), 3 * r + part))

    return pl.pallas_call(
        _band_attn_kernel,
        out_shape=(jax.ShapeDtypeStruct((rows, dil * DSWA_WIDTH), BF16),
                   jax.ShapeDtypeStruct((rows, dil * LANES), F32)),
        grid=(dil, nt),
        in_specs=[own(0),
                  prev(1), own(1), nxt(1),
                  prev(2), own(2), nxt(2),
                  pl.BlockSpec((DSWA_HEADS_PER_GROUP, tq, tw), lambda r, n: (group, 0, 0))],
        out_specs=(pl.BlockSpec((tq, DSWA_WIDTH), lambda r, n: (n, r)),
                   pl.BlockSpec((tq, LANES), lambda r, n: (n, r))),
        compiler_params=_params(("parallel", "parallel")),
        name=f"band_attention_d{dil}",
    )(qkv, qkv, qkv, qkv, qkv, qkv, qkv, bias)


def _mix_kernel(o1_ref, o4_ref, o16_ref, l1_ref, l4_ref, l16_ref, sg_ref, out_ref,
                o4n_ref, l4n_ref, o16n_ref, l16n_ref):
    for dil, o_ref, l_ref, on_ref, ln_ref in ((DILATIONS[1], o4_ref, l4_ref, o4n_ref, l4n_ref),
                                               (DILATIONS[2], o16_ref, l16_ref, o16n_ref, l16n_ref)):
        rows = o_ref.shape[0]
        for r in range(dil):
            for hd in range(DSWA_HEADS_PER_GROUP):
                lo = r * DSWA_WIDTH + hd * DSWA_HEAD_DIM
                on_ref[hd, pl.ds(r, rows, stride=dil), :] = (
                    o_ref[:, lo:lo + DSWA_HEAD_DIM].astype(F32))
            ln_ref[pl.ds(r, rows, stride=dil), :] = l_ref[:, r * LANES:(r + 1) * LANES]
    for hd in range(DSWA_HEADS_PER_GROUP):
        cols = slice(hd * DSWA_HEAD_DIM, (hd + 1) * DSWA_HEAD_DIM)
        l1 = l1_ref[:, hd:hd + 1]
        l2 = l4n_ref[:, hd:hd + 1]
        l3 = l16n_ref[:, hd:hd + 1]
        mx = jnp.maximum(jnp.maximum(l1, l2), l3)
        e1, e2, e3 = jnp.exp(l1 - mx), jnp.exp(l2 - mx), jnp.exp(l3 - mx)
        den = e1 + e2 + e3
        mix = ((e1 / den) * o1_ref[:, cols].astype(F32)
               + (e2 / den) * o4n_ref[hd]
               + (e3 / den) * o16n_ref[hd])
        out_ref[:, cols] = (mix * sg_ref[:, cols].astype(F32)).astype(out_ref.dtype)


def _group_mixture(outs, lses, gates, tm=256):
    o1, o4, o16 = outs
    l1, l4, l16 = lses
    s = o1.shape[0]
    d4, d16 = DILATIONS[1], DILATIONS[2]
    w = DSWA_WIDTH
    return pl.pallas_call(
        _mix_kernel,
        out_shape=jax.ShapeDtypeStruct((s, w), BF16),
        grid=(s // tm,),
        in_specs=[pl.BlockSpec((tm, w), lambda i: (i, 0)),
                  pl.BlockSpec((tm // d4, d4 * w), lambda i: (i, 0)),
                  pl.BlockSpec((tm // d16, d16 * w), lambda i: (i, 0)),
                  pl.BlockSpec((tm, LANES), lambda i: (i, 0)),
                  pl.BlockSpec((tm // d4, d4 * LANES), lambda i: (i, 0)),
                  pl.BlockSpec((tm // d16, d16 * LANES), lambda i: (i, 0)),
                  pl.BlockSpec((tm, w), lambda i: (i, MLA_WIDTH // w))],
        out_specs=pl.BlockSpec((tm, w), lambda i: (i, 0)),
        scratch_shapes=[pltpu.VMEM((DSWA_HEADS_PER_GROUP, tm, DSWA_HEAD_DIM), F32),
                        pltpu.VMEM((tm, LANES), F32),
                        pltpu.VMEM((DSWA_HEADS_PER_GROUP, tm, DSWA_HEAD_DIM), F32),
                        pltpu.VMEM((tm, LANES), F32)],
        compiler_params=_params(("parallel",)),
        name="group_mixture",
    )(o1, o4, o16, l1, l4, l16, gates)


def _merge_kernel(a_ref, sg_ref, b_ref, wm_ref, wd_ref, rm_ref, rd_ref, o_ref):
    a_gated = (a_ref[...].astype(F32) * sg_ref[...].astype(F32)).astype(BF16)
    y_mla = _dot(a_gated, wm_ref[...])
    y_dswa = _dot(b_ref[...], wd_ref[...])
    merged = rm_ref[...].astype(F32) * y_mla + rd_ref[...].astype(F32) * y_dswa
    o_ref[...] = merged.astype(o_ref.dtype)


def _merge_branches(a, b_gated, gates, w_o_mla, w_o_dswa, tm=512, tn=COL_BLOCK):
    s = a.shape[0]
    d = w_o_mla.shape[1]
    r_mla_block = (MLA_WIDTH + DSWA_WIDTH) // tn
    r_dswa_block = (MLA_WIDTH + DSWA_WIDTH + D_MODEL) // tn
    return pl.pallas_call(
        _merge_kernel,
        out_shape=jax.ShapeDtypeStruct((s, d), BF16),
        grid=(d // tn, s // tm),
        in_specs=[pl.BlockSpec((tm, MLA_WIDTH), lambda j, i: (i, 0)),
                  pl.BlockSpec((tm, MLA_WIDTH), lambda j, i: (i, 0)),
                  pl.BlockSpec((tm, DSWA_WIDTH), lambda j, i: (i, 0)),
                  pl.BlockSpec((MLA_WIDTH, tn), lambda j, i: (0, j)),
                  pl.BlockSpec((DSWA_WIDTH, tn), lambda j, i: (0, j)),
                  pl.BlockSpec((tm, tn), lambda j, i: (i, r_mla_block + j)),
                  pl.BlockSpec((tm, tn), lambda j, i: (i, r_dswa_block + j))],
        out_specs=pl.BlockSpec((tm, tn), lambda j, i: (i, j)),
        compiler_params=_params(("parallel", "parallel")),
        name="merge_branches",
    )(a, gates, b_gated, w_o_mla, w_o_dswa, gates, gates)


def _final_ln_kernel(x_ref, o_ref, g0_ref, b0_ref, g_ref, b_ref, out_ref):
    h = _layer_norm_rows(x_ref[...], g0_ref[...], b0_ref[...])
    out_ref[...] = _layer_norm_rows(DEEPNORM_ALPHA * h + o_ref[...], g_ref[...], b_ref[...])


def _final_layer_norm(x, out, g0, b0, g, b, tm=256):
    s, d = x.shape
    row = pl.BlockSpec((tm, d), lambda i: (i, 0))
    vec = pl.BlockSpec((1, d), lambda i: (0, 0))
    return pl.pallas_call(
        _final_ln_kernel,
        out_shape=jax.ShapeDtypeStruct((s, d), F32),
        grid=(s // tm,),
        in_specs=[row, row, vec, vec, vec, vec],
        out_specs=row,
        compiler_params=_params(("parallel",)),
        name="final_ln",
    )(x, out, g0.reshape(1, d), b0.reshape(1, d), g.reshape(1, d), b.reshape(1, d))


def _cast_kernel(w_ref, o_ref):
    o_ref[...] = w_ref[...].astype(o_ref.dtype)


def _cast_rows_bf16(w_t, row0, n_rows, name):
    k = w_t.shape[1]
    tr = math.gcd(MLA_A_COLS, REST_COLS)
    assert row0 % tr == 0 and n_rows % tr == 0 and tr % 16 == 0
    return pl.pallas_call(
        _cast_kernel,
        out_shape=jax.ShapeDtypeStruct((n_rows, k), BF16),
        grid=(n_rows // tr,),
        in_specs=[pl.BlockSpec((tr, k), lambda j: (row0 // tr + j, 0))],
        out_specs=pl.BlockSpec((tr, k), lambda j: (j, 0)),
        compiler_params=_params(("parallel",)),
        name=name,
    )(w_t)


def _rope_tables(seq):
    half = QK_ROPE_DIM // 2
    pos = jnp.arange(seq, dtype=F32)
    inv_freq = 1.0 / (ROPE_THETA ** (jnp.arange(0, QK_ROPE_DIM, 2, dtype=F32) / QK_ROPE_DIM))
    lane_freq = jnp.concatenate([inv_freq, inv_freq, jnp.zeros((LANES - QK_ROPE_DIM,), F32)])
    ang = pos[:, None] * lane_freq[None, :]
    lane = np.arange(LANES)
    on_t1 = jnp.asarray((lane < half).astype(np.float32))
    on_t2 = jnp.asarray(((lane >= half) & (lane < QK_ROPE_DIM)).astype(np.float32))
    cos, sin = jnp.cos(ang), jnp.sin(ang)
    return jnp.concatenate([cos * (on_t1 + on_t2), sin * on_t2, -sin * on_t1], axis=1)


def kernel(x, emb_ln_g, emb_ln_b, rel_bias, w_in, q_a_norm_g, w_q_b, kv_a_norm_g, w_kv_b,
           w_o_mla, w_o_dswa, w_out, ln_g, ln_b):
    assert DEPTH == 1 and x.shape == (1, SEQ, D_MODEL)
    x2 = x[0]

    w_in_t = jnp.swapaxes(w_in, 1, 2)[0]
    w_mla_a = _cast_rows_bf16(w_in_t, 0, MLA_A_COLS, "mla_a_weights_bf16")

    q_scale = QK_HEAD_DIM ** -0.5 * math.log2(math.e)
    rope_tab = _rope_tables(SEQ)

    *h_slabs, w_q, w_kv = _input_layer_norm(x2, emb_ln_g, emb_ln_b,
                                            jnp.swapaxes(w_q_b, 1, 2)[0], w_kv_b[0])
    h = h_slabs[0]

    qn, ckvn, kpe = _mla_a_proj(h, w_mla_a, q_a_norm_g[0], kv_a_norm_g[0], rope_tab)
    q = _mla_q_proj(qn, w_q, rope_tab, q_scale)
    kv = _proj(ckvn, w_kv, n_out=2 * MLA_WIDTH, w_block=lambda j: j, tm=2048, tn=2 * COL_BLOCK,
               name="mla_kv_proj")
    a, w_rest, w_om, w_od, w_o = _mla_attention(
        q, kv, kpe, [(w_in_t, MLA_A_COLS, REST_COLS), (w_o_mla[0], 0, MLA_WIDTH),
                     (w_o_dswa[0], 0, DSWA_WIDTH), (w_out[0], 0, D_MODEL)])

    gates = _proj(h, w_rest, n_out=GATE_WIDTH, w_block=lambda j: GATE_COL0 + j, gated=True,
                  n_silu_blocks=(MLA_WIDTH + DSWA_WIDTH) // COL_BLOCK, w_transposed=True,
                  name="gate_proj")

    bias = _band_bias(rel_bias, BAND_TILE)
    outs, lses = [], []
    blocks_per_part = DSWA_QKV_WIDTH // COL_BLOCK
    for g, dil in enumerate(DILATIONS):
        qkv = _proj(h_slabs[g], w_rest, n_out=3 * DSWA_WIDTH, dil=dil, w_transposed=True,
                    w_block=lambda j, g=g: j * blocks_per_part + g, name=f"dswa_proj_d{dil}")
        o, lse = _band_attention(qkv, bias, g, dil)
        outs.append(o)
        lses.append(lse)
    b_gated = _group_mixture(outs, lses, gates)

    merged = _merge_branches(a, b_gated, gates, w_om, w_od)
    out = _proj(merged, w_o, n_out=D_MODEL, w_block=lambda j: j, out_dtype=F32, name="out_proj")
    y = _final_layer_norm(x2, out, emb_ln_g, emb_ln_b, ln_g[0], ln_b[0])
    return y[None]
```

```python
import functools
import math

import jax
import jax.numpy as jnp
import numpy as np
from jax import lax
from jax.experimental import pallas as pl
from jax.experimental.pallas import tpu as pltpu

D_MODEL = 4096
SEQ = 8192
DEPTH = 1

MLA_HEADS = 16
Q_LORA_RANK = 1024
KV_LORA_RANK = 512
QK_NOPE_DIM = 128
QK_ROPE_DIM = 64
QK_HEAD_DIM = QK_NOPE_DIM + QK_ROPE_DIM
V_HEAD_DIM = 128
MLA_WIDTH = MLA_HEADS * V_HEAD_DIM
ROPE_THETA = 10000.0

DIL_GROUPS = ((128, 1), (512, 4), (2048, 16))
DILATIONS = tuple(d for _, d in DIL_GROUPS)
DSWA_HEADS_PER_GROUP = 8
DSWA_HEADS = 24
DSWA_HEAD_DIM = 128
DSWA_QKV_WIDTH = DSWA_HEADS * DSWA_HEAD_DIM
DSWA_WIDTH = DSWA_HEADS_PER_GROUP * DSWA_HEAD_DIM
BAND_HALF = 64
BAND_TILE = 256

REL_BUCKETS = 32
REL_MAX_DISTANCE = 1024

DEEPNORM_ALPHA = (2.0 * DEPTH) ** 0.25
LN_EPS = 1e-5
RMS_EPS = 1e-6
NEG_INF = -1e30

IN_SPLITS = (Q_LORA_RANK, KV_LORA_RANK + QK_ROPE_DIM, DSWA_QKV_WIDTH, DSWA_QKV_WIDTH,
             DSWA_QKV_WIDTH, MLA_WIDTH, DSWA_WIDTH, D_MODEL, D_MODEL)
IN_OFFSETS = tuple(int(v) for v in np.cumsum(IN_SPLITS)[:-1])
MLA_A_COLS = IN_OFFSETS[1]
REST_COLS = sum(IN_SPLITS) - MLA_A_COLS

LANES = 128
QK_PAD_DIM = 256
COL_BLOCK = 1024
GATE_COL0 = 3 * DSWA_QKV_WIDTH // COL_BLOCK
GATE_WIDTH = MLA_WIDTH + DSWA_WIDTH + 2 * D_MODEL
VMEM_LIMIT = 56 * 1024 * 1024

LOG2_E = math.log2(math.e)
LN_2 = math.log(2.0)

BF16 = jnp.bfloat16
F32 = jnp.float32


def _params(sem):
    return pltpu.CompilerParams(dimension_semantics=sem, vmem_limit_bytes=VMEM_LIMIT)


def _dot(a, b):
    return jnp.dot(a, b, preferred_element_type=F32)


def _dot_nt(a, b):
    return lax.dot_general(a, b, (((1,), (1,)), ((), ())), preferred_element_type=F32)


def _sigmoid(x):
    return 0.5 * jnp.tanh(0.5 * x) + 0.5


def _layer_norm_rows(x, g, b):
    mu = jnp.mean(x, axis=-1, keepdims=True)
    xc = x - mu
    var = jnp.mean(xc * xc, axis=-1, keepdims=True)
    return xc * lax.rsqrt(var + LN_EPS) * g + b


def _rms_norm_rows(x, g):
    return x * lax.rsqrt(jnp.mean(x * x, axis=-1, keepdims=True) + RMS_EPS) * g


def _rope_lanes(t, tab):
    c, s1, s2 = tab[:, :LANES], tab[:, LANES:2 * LANES], tab[:, 2 * LANES:]
    return t * c + pltpu.roll(t, 32, 1) * s1 + pltpu.roll(t, LANES - 32, 1) * s2


def _class_gather_matrix(tm, dil):
    per_class = tm // dil
    i = np.arange(tm)
    p = np.zeros((tm, tm), np.float32)
    p[i, (i % per_class) * dil + i // per_class] = 1.0
    return jnp.asarray(p, BF16)


def _ln_kernel(x_ref, g_ref, b_ref, p4_ref, p16_ref, wq_ref, wkv_ref,
               h1_ref, h4_ref, h16_ref, wq_bf_ref, wkv_bf_ref):
    wq_bf_ref[...] = wq_ref[...].astype(BF16)
    wkv_bf_ref[...] = wkv_ref[...].astype(BF16)
    d = x_ref.shape[1]
    y = _layer_norm_rows(x_ref[...], g_ref[...], b_ref[...]).astype(BF16)
    h1_ref[...] = y
    for dil, p_ref, slab_ref in ((DILATIONS[1], p4_ref, h4_ref), (DILATIONS[2], p16_ref, h16_ref)):
        rows = slab_ref.shape[0]
        grouped = _dot(p_ref[...], y).astype(BF16)
        for r in range(dil):
            slab_ref[:, r * d:(r + 1) * d] = grouped[r * rows:(r + 1) * rows, :]


def _input_layer_norm(x, g, b, w_q_t, w_kv, tm=256):
    s, d = x.shape
    d4, d16 = DILATIONS[1], DILATIONS[2]
    steps = s // tm
    rq, rkv = w_q_t.shape[0] // steps, w_kv.shape[0] // steps
    assert rq * steps == w_q_t.shape[0] and rkv * steps == w_kv.shape[0]
    assert rq % 16 == 0 and rkv % 16 == 0
    return pl.pallas_call(
        _ln_kernel,
        out_shape=(jax.ShapeDtypeStruct((s, d), BF16),
                   jax.ShapeDtypeStruct((s // d4, d4 * d), BF16),
                   jax.ShapeDtypeStruct((s // d16, d16 * d), BF16),
                   jax.ShapeDtypeStruct(w_q_t.shape, BF16),
                   jax.ShapeDtypeStruct(w_kv.shape, BF16)),
        grid=(steps,),
        in_specs=[pl.BlockSpec((tm, d), lambda i: (i, 0)),
                  pl.BlockSpec((1, d), lambda i: (0, 0)),
                  pl.BlockSpec((1, d), lambda i: (0, 0)),
                  pl.BlockSpec((tm, tm), lambda i: (0, 0)),
                  pl.BlockSpec((tm, tm), lambda i: (0, 0)),
                  pl.BlockSpec((rq, w_q_t.shape[1]), lambda i: (i, 0)),
                  pl.BlockSpec((rkv, w_kv.shape[1]), lambda i: (i, 0))],
        out_specs=(pl.BlockSpec((tm, d), lambda i: (i, 0)),
                   pl.BlockSpec((tm // d4, d4 * d), lambda i: (i, 0)),
                   pl.BlockSpec((tm // d16, d16 * d), lambda i: (i, 0)),
                   pl.BlockSpec((rq, w_q_t.shape[1]), lambda i: (i, 0)),
                   pl.BlockSpec((rkv, w_kv.shape[1]), lambda i: (i, 0))),
        compiler_params=_params(("parallel",)),
        name="input_ln",
    )(x, g.reshape(1, d), b.reshape(1, d),
      _class_gather_matrix(tm, d4), _class_gather_matrix(tm, d16), w_q_t, w_kv)


def _proj_kernel(a_ref, w_ref, o_ref, *, n_silu_blocks, gated, w_transposed):
    acc = (_dot_nt if w_transposed else _dot)(a_ref[...], w_ref[...])
    if gated:
        sig = _sigmoid(acc)
        acc = jnp.where(pl.program_id(0) < n_silu_blocks, acc * sig, sig)
    o_ref[...] = acc.astype(o_ref.dtype)


def _proj(a, w, *, n_out, w_block, dil=1, tm=1024, tn=COL_BLOCK, gated=False, n_silu_blocks=0,
          w_transposed=False, out_dtype=BF16, name):
    rows, kd = a.shape
    k = kd // dil
    nj = n_out // tn
    tm = min(tm, rows)
    kern = functools.partial(_proj_kernel, n_silu_blocks=n_silu_blocks, gated=gated,
                             w_transposed=w_transposed)
    if w_transposed:
        w_spec = pl.BlockSpec((tn, k), lambda j, r, i: (w_block(j), 0))
    else:
        w_spec = pl.BlockSpec((k, tn), lambda j, r, i: (0, w_block(j)))
    return pl.pallas_call(
        kern,
        out_shape=jax.ShapeDtypeStruct((rows, dil * n_out), out_dtype),
        grid=(nj, dil, rows // tm),
        in_specs=[pl.BlockSpec((tm, k), lambda j, r, i: (i, r)), w_spec],
        out_specs=pl.BlockSpec((tm, tn), lambda j, r, i: (i, r * nj + j)),
        compiler_params=_params(("parallel", "parallel", "parallel")),
        name=name,
    )(a, w)


def _mla_a_kernel(h_ref, w_ref, gq_ref, gkv_ref, tab_ref, qn_ref, ckvn_ref, kpe_ref):
    acc = _dot_nt(h_ref[...], w_ref[...])
    qn_ref[...] = _rms_norm_rows(acc[:, :Q_LORA_RANK], gq_ref[...]).astype(BF16)
    ckv = acc[:, Q_LORA_RANK:Q_LORA_RANK + KV_LORA_RANK]
    ckvn_ref[...] = _rms_norm_rows(ckv, gkv_ref[...]).astype(BF16)
    k_pe = acc[:, Q_LORA_RANK + KV_LORA_RANK:]
    k_pe = jnp.concatenate([k_pe, jnp.zeros((k_pe.shape[0], LANES - QK_ROPE_DIM), F32)], axis=1)
    kpe_ref[...] = _rope_lanes(k_pe, tab_ref[...]).astype(BF16)


def _mla_a_proj(h, w, gq, gkv, k_tab, tm=512):
    s, k = h.shape
    return pl.pallas_call(
        _mla_a_kernel,
        out_shape=(jax.ShapeDtypeStruct((s, Q_LORA_RANK), BF16),
                   jax.ShapeDtypeStruct((s, KV_LORA_RANK), BF16),
                   jax.ShapeDtypeStruct((s, LANES), BF16)),
        grid=(s // tm,),
        in_specs=[pl.BlockSpec((tm, k), lambda i: (i, 0)),
                  pl.BlockSpec((MLA_A_COLS, k), lambda i: (0, 0)),
                  pl.BlockSpec((1, Q_LORA_RANK), lambda i: (0, 0)),
                  pl.BlockSpec((1, KV_LORA_RANK), lambda i: (0, 0)),
                  pl.BlockSpec((tm, 3 * LANES), lambda i: (i, 0))],
        out_specs=(pl.BlockSpec((tm, Q_LORA_RANK), lambda i: (i, 0)),
                   pl.BlockSpec((tm, KV_LORA_RANK), lambda i: (i, 0)),
                   pl.BlockSpec((tm, LANES), lambda i: (i, 0))),
        compiler_params=_params(("parallel",)),
        name="mla_a_proj",
    )(h, w, gq.reshape(1, -1), gkv.reshape(1, -1), k_tab)


def _mla_q_kernel(qn_ref, w_ref, tab_ref, q_ref, *, q_scale):
    qn = qn_ref[...]
    tab = tab_ref[...]
    pad = jnp.zeros((qn.shape[0], LANES - QK_ROPE_DIM), F32)
    for hd in range(MLA_HEADS):
        lo = hd * QK_PAD_DIM
        acc = _dot_nt(qn, w_ref[hd * QK_HEAD_DIM:(hd + 1) * QK_HEAD_DIM, :]) * q_scale
        q_ref[:, lo:lo + LANES] = acc[:, :QK_NOPE_DIM].astype(BF16)
        q_pe = jnp.concatenate([acc[:, QK_NOPE_DIM:], pad], axis=1)
        q_ref[:, lo + LANES:lo + QK_PAD_DIM] = _rope_lanes(q_pe, tab).astype(BF16)


def _mla_q_proj(qn, w_t, rope_tab, q_scale, tm=1024):
    s, k = qn.shape
    n = MLA_HEADS * QK_PAD_DIM
    return pl.pallas_call(
        functools.partial(_mla_q_kernel, q_scale=q_scale),
        out_shape=jax.ShapeDtypeStruct((s, n), BF16),
        grid=(s // tm,),
        in_specs=[pl.BlockSpec((tm, k), lambda i: (i, 0)),
                  pl.BlockSpec(w_t.shape, lambda i: (0, 0)),
                  pl.BlockSpec((tm, 3 * LANES), lambda i: (i, 0))],
        out_specs=pl.BlockSpec((tm, n), lambda i: (i, 0)),
        compiler_params=_params(("parallel",)),
        name="mla_q_proj",
    )(qn, w_t, rope_tab)


def _mla_attn_kernel(q_ref, k_ref, v_ref, kpe_ref, *rest, tk, sub, n_cast):
    o_ref = rest[n_cast]
    kcat_ref, vcat_ref = rest[2 * n_cast + 1:]
    for src_ref, dst_ref in zip(rest[:n_cast], rest[n_cast + 1:2 * n_cast + 1]):
        dst_ref[...] = src_ref[...].astype(dst_ref.dtype)

    @pl.when(pl.program_id(1) == 0)
    def _():
        kcat_ref[:, :LANES] = k_ref[...]
        kcat_ref[:, LANES:] = kpe_ref[...]
        vcat_ref[:, :V_HEAD_DIM] = v_ref[...]
        vcat_ref[:, V_HEAD_DIM:] = jnp.ones((v_ref.shape[0], LANES), BF16)

    n_kv = k_ref.shape[0] // tk
    n_sub = q_ref.shape[0] // sub
    qs = [q_ref[c * sub:(c + 1) * sub, :] for c in range(n_sub)]
    ms = [jnp.full((sub, 1), -jnp.inf, F32) for _ in range(n_sub)]
    accs = [jnp.zeros((sub, V_HEAD_DIM + LANES), F32) for _ in range(n_sub)]
    for j in range(n_kv):
        kblk = kcat_ref[j * tk:(j + 1) * tk, :]
        vblk = vcat_ref[j * tk:(j + 1) * tk, :]
        for c in range(n_sub):
            s = _dot_nt(qs[c], kblk)
            m_new = jnp.maximum(ms[c], jnp.max(s, axis=1, keepdims=True))
            alpha = jnp.exp2(ms[c] - m_new)
            p = jnp.exp2(s - m_new).astype(BF16)
            accs[c] = alpha * accs[c] + _dot(p, vblk)
            ms[c] = m_new
    for c in range(n_sub):
        rows = slice(c * sub, (c + 1) * sub)
        attn = accs[c][:, :V_HEAD_DIM] / accs[c][:, V_HEAD_DIM:]
        o_ref[rows, :] = attn.astype(o_ref.dtype)


def _mla_attention(q, kv, kpe, casts, tq=2048, tk=1024, sub=512):
    s = q.shape[0]
    n_q = s // tq
    steps = MLA_HEADS * n_q
    cast_in, cast_out, cast_shapes = [], [], []
    for w, row0, rows in casts:
        tr = rows // steps
        assert tr * steps == rows and row0 % tr == 0 and tr % 16 == 0
        cast_in.append(pl.BlockSpec(
            (tr, w.shape[1]), lambda h, i, row0=row0, tr=tr: (row0 // tr + h * n_q + i, 0)))
        cast_out.append(pl.BlockSpec((tr, w.shape[1]), lambda h, i: (h * n_q + i, 0)))
        cast_shapes.append(jax.ShapeDtypeStruct((rows, w.shape[1]), BF16))
    return pl.pallas_call(
        functools.partial(_mla_attn_kernel, tk=tk, sub=sub, n_cast=len(casts)),
        out_shape=(jax.ShapeDtypeStruct((s, MLA_WIDTH), BF16), *cast_shapes),
        grid=(MLA_HEADS, n_q),
        in_specs=[pl.BlockSpec((tq, QK_PAD_DIM), lambda h, i: (i, h)),
                  pl.BlockSpec((s, QK_NOPE_DIM), lambda h, i: (0, 2 * h)),
                  pl.BlockSpec((s, V_HEAD_DIM), lambda h, i: (0, 2 * h + 1)),
                  pl.BlockSpec((s, LANES), lambda h, i: (0, 0)),
                  *cast_in],
        out_specs=(pl.BlockSpec((tq, V_HEAD_DIM), lambda h, i: (i, h)), *cast_out),
        scratch_shapes=[pltpu.VMEM((s, QK_PAD_DIM), BF16),
                        pltpu.VMEM((s, V_HEAD_DIM + LANES), BF16)],
        compiler_params=_params(("parallel", "arbitrary")),
        name="mla_attention",
    )(q, kv, kv, kpe, *[w for w, _, _ in casts])


def _band_bucket_index(dil, tq):
    nb = REL_BUCKETS // 2
    max_exact = nb // 2
    rel = (np.arange(tq + 2 * BAND_HALF)[None, :] - BAND_HALF - np.arange(tq)[:, None]) * dil
    n = np.abs(rel)
    nf = np.maximum(n, 1).astype(np.float64)
    large = max_exact + (np.log(nf / max_exact) / math.log(REL_MAX_DISTANCE / max_exact)
                         * (nb - max_exact)).astype(np.int32)
    large = np.minimum(large, nb - 1)
    return (np.where(rel > 0, nb, 0) + np.where(n < max_exact, n, large)).astype(np.int32)


def _band_bias_kernel(rb_ref, idx_ref, o_ref):
    hd = pl.program_id(0)
    idx = idx_ref[0]
    acc = jnp.zeros(idx.shape, F32)
    for b in range(REL_BUCKETS):
        acc = jnp.where(idx == b, rb_ref[b, hd], acc)
    o_ref[0] = acc * LOG2_E


def _band_bias(rel_bias, tq):
    idx = jnp.asarray(np.stack([_band_bucket_index(dil, tq) for dil in DILATIONS]))
    tw = idx.shape[2]
    return pl.pallas_call(
        _band_bias_kernel,
        out_shape=jax.ShapeDtypeStruct((DSWA_HEADS, tq, tw), F32),
        grid=(DSWA_HEADS,),
        in_specs=[pl.BlockSpec(memory_space=pltpu.SMEM),
                  pl.BlockSpec((1, tq, tw), lambda h: (h // DSWA_HEADS_PER_GROUP, 0, 0))],
        out_specs=pl.BlockSpec((1, tq, tw), lambda h: (h, 0, 0)),
        compiler_params=_params(("parallel",)),
        name="band_bias",
    )(rel_bias, idx)


def _band_attn_kernel(q_ref, kp_ref, ko_ref, kn_ref, vp_ref, vo_ref, vn_ref, bias_ref,
                      o_ref, lse_ref):
    tq = q_ref.shape[0]
    tw = tq + 2 * BAND_HALF
    tile = pl.program_id(1)
    prev_ok = tile != 0
    next_ok = tile != pl.num_programs(1) - 1
    row = lax.broadcasted_iota(jnp.int32, (tq, tw), 0)
    col = lax.broadcasted_iota(jnp.int32, (tq, tw), 1)
    rel = col - BAND_HALF - row
    mask = ((jnp.abs(rel) <= BAND_HALF) & ((col >= BAND_HALF) | prev_ok)
            & ((col < tq + BAND_HALF) | next_ok))
    lane = lax.broadcasted_iota(jnp.int32, (tq, LANES), 1)
    scale = DSWA_HEAD_DIM ** -0.5 * LOG2_E
    heads = range(DSWA_HEADS_PER_GROUP)
    cols = [slice(hd * DSWA_HEAD_DIM, (hd + 1) * DSWA_HEAD_DIM) for hd in heads]
    kc = [jnp.concatenate([kp_ref[:, c], ko_ref[:, c], kn_ref[:, c]], axis=0) for c in cols]
    s = [_dot_nt(q_ref[:, cols[hd]], kc[hd]) * scale + bias_ref[hd] for hd in heads]
    s = [jnp.where(mask, sh, NEG_INF) for sh in s]
    m = [jnp.max(sh, axis=1, keepdims=True) for sh in s]
    p = [jnp.exp2(sh - mh).astype(BF16) for sh, mh in zip(s, m)]
    ones = jnp.ones((tw, LANES), BF16)
    vc = [jnp.concatenate(
        [jnp.concatenate([vp_ref[:, c], vo_ref[:, c], vn_ref[:, c]], axis=0), ones], axis=1)
        for c in cols]
    pv = [_dot(p[hd], vc[hd]) for hd in heads]
    lse_tile = jnp.zeros((tq, LANES), F32)
    for hd in heads:
        den = pv[hd][:, DSWA_HEAD_DIM:]
        o_ref[:, cols[hd]] = (pv[hd][:, :DSWA_HEAD_DIM] / den).astype(o_ref.dtype)
        lse_tile = jnp.where(lane == hd, (m[hd] + jnp.log2(den)) * LN_2, lse_tile)
    lse_ref[...] = lse_tile


def _band_attention(qkv, bias, group, dil):
    rows = qkv.shape[0]
    tq = bias.shape[1]
    nt = rows // tq
    per_tile = tq // BAND_HALF
    n_edge = rows // BAND_HALF
    tw = tq + 2 * BAND_HALF

    def own(part):
        return pl.BlockSpec((tq, DSWA_WIDTH), lambda r, n: (n, 3 * r + part))

    def prev(part):
        return pl.BlockSpec((BAND_HALF, DSWA_WIDTH),
                            lambda r, n: (jnp.maximum(n * per_tile - 1, 0), 3 * r + part))

    def nxt(part):
        return pl.BlockSpec((BAND_HALF, DSWA_WIDTH),
                            lambda r, n: (jnp.minimum((n + 1) * per_tile, n_edge - 1), 3 * r + part))

    return pl.pallas_call(
        _band_attn_kernel,
        out_shape=(jax.ShapeDtypeStruct((rows, dil * DSWA_WIDTH), BF16),
                   jax.ShapeDtypeStruct((rows, dil * LANES), F32)),
        grid=(dil, nt),
        in_specs=[own(0),
                  prev(1), own(1), nxt(1),
                  prev(2), own(2), nxt(2),
                  pl.BlockSpec((DSWA_HEADS_PER_GROUP, tq, tw), lambda r, n: (group, 0, 0))],
        out_specs=(pl.BlockSpec((tq, DSWA_WIDTH), lambda r, n: (n, r)),
                   pl.BlockSpec((tq, LANES), lambda r, n: (n, r))),
        compiler_params=_params(("parallel", "parallel")),
        name=f"band_attention_d{dil}",
    )(qkv, qkv, qkv, qkv, qkv, qkv, qkv, bias)


def _mix_kernel(o1_ref, o4_ref, o16_ref, l1_ref, l4_ref, l16_ref, sg_ref, out_ref,
                o4n_ref, l4n_ref, o16n_ref, l16n_ref):
    for dil, o_ref, l_ref, on_ref, ln_ref in ((DILATIONS[1], o4_ref, l4_ref, o4n_ref, l4n_ref),
                                               (DILATIONS[2], o16_ref, l16_ref, o16n_ref, l16n_ref)):
        rows = o_ref.shape[0]
        for r in range(dil):
            for hd in range(DSWA_HEADS_PER_GROUP):
                lo = r * DSWA_WIDTH + hd * DSWA_HEAD_DIM
                on_ref[hd, pl.ds(r, rows, stride=dil), :] = (
                    o_ref[:, lo:lo + DSWA_HEAD_DIM].astype(F32))
            ln_ref[pl.ds(r, rows, stride=dil), :] = l_ref[:, r * LANES:(r + 1) * LANES]
    for hd in range(DSWA_HEADS_PER_GROUP):
        cols = slice(hd * DSWA_HEAD_DIM, (hd + 1) * DSWA_HEAD_DIM)
        l1 = l1_ref[:, hd:hd + 1]
        l2 = l4n_ref[:, hd:hd + 1]
        l3 = l16n_ref[:, hd:hd + 1]
        mx = jnp.maximum(jnp.maximum(l1, l2), l3)
        e1, e2, e3 = jnp.exp(l1 - mx), jnp.exp(l2 - mx), jnp.exp(l3 - mx)
        den = e1 + e2 + e3
        mix = ((e1 / den) * o1_ref[:, cols].astype(F32)
               + (e2 / den) * o4n_ref[hd]
               + (e3 / den) * o16n_ref[hd])
        out_ref[:, cols] = (mix * sg_ref[:, cols].astype(F32)).astype(out_ref.dtype)


def _group_mixture(outs, lses, gates, tm=256):
    o1, o4, o16 = outs
    l1, l4, l16 = lses
    s = o1.shape[0]
    d4, d16 = DILATIONS[1], DILATIONS[2]
    w = DSWA_WIDTH
    return pl.pallas_call(
        _mix_kernel,
        out_shape=jax.ShapeDtypeStruct((s, w), BF16),
        grid=(s // tm,),
        in_specs=[pl.BlockSpec((tm, w), lambda i: (i, 0)),
                  pl.BlockSpec((tm // d4, d4 * w), lambda i: (i, 0)),
                  pl.BlockSpec((tm // d16, d16 * w), lambda i: (i, 0)),
                  pl.BlockSpec((tm, LANES), lambda i: (i, 0)),
                  pl.BlockSpec((tm // d4, d4 * LANES), lambda i: (i, 0)),
                  pl.BlockSpec((tm // d16, d16 * LANES), lambda i: (i, 0)),
                  pl.BlockSpec((tm, w), lambda i: (i, MLA_WIDTH // w))],
        out_specs=pl.BlockSpec((tm, w), lambda i: (i, 0)),
        scratch_shapes=[pltpu.VMEM((DSWA_HEADS_PER_GROUP, tm, DSWA_HEAD_DIM), F32),
                        pltpu.VMEM((tm, LANES), F32),
                        pltpu.VMEM((DSWA_HEADS_PER_GROUP, tm, DSWA_HEAD_DIM), F32),
                        pltpu.VMEM((tm, LANES), F32)],
        compiler_params=_params(("parallel",)),
        name="group_mixture",
    )(o1, o4, o16, l1, l4, l16, gates)


def _merge_kernel(a_ref, sg_ref, b_ref, wm_ref, wd_ref, rm_ref, rd_ref, o_ref):
    a_gated = (a_ref[...].astype(F32) * sg_ref[...].astype(F32)).astype(BF16)
    y_mla = _dot(a_gated, wm_ref[...])
    y_dswa = _dot(b_ref[...], wd_ref[...])
    merged = rm_ref[...].astype(F32) * y_mla + rd_ref[...].astype(F32) * y_dswa
    o_ref[...] = merged.astype(o_ref.dtype)


def _merge_branches(a, b_gated, gates, w_o_mla, w_o_dswa, tm=1024, tn=COL_BLOCK):
    s = a.shape[0]
    d = w_o_mla.shape[1]
    r_mla_block = (MLA_WIDTH + DSWA_WIDTH) // tn
    r_dswa_block = (MLA_WIDTH + DSWA_WIDTH + D_MODEL) // tn
    return pl.pallas_call(
        _merge_kernel,
        out_shape=jax.ShapeDtypeStruct((s, d), BF16),
        grid=(d // tn, s // tm),
        in_specs=[pl.BlockSpec((tm, MLA_WIDTH), lambda j, i: (i, 0)),
                  pl.BlockSpec((tm, MLA_WIDTH), lambda j, i: (i, 0)),
                  pl.BlockSpec((tm, DSWA_WIDTH), lambda j, i: (i, 0)),
                  pl.BlockSpec((MLA_WIDTH, tn), lambda j, i: (0, j)),
                  pl.BlockSpec((DSWA_WIDTH, tn), lambda j, i: (0, j)),
                  pl.BlockSpec((tm, tn), lambda j, i: (i, r_mla_block + j)),
                  pl.BlockSpec((tm, tn), lambda j, i: (i, r_dswa_block + j))],
        out_specs=pl.BlockSpec((tm, tn), lambda j, i: (i, j)),
        compiler_params=_params(("parallel", "parallel")),
        name="merge_branches",
    )(a, gates, b_gated, w_o_mla, w_o_dswa, gates, gates)


def _final_ln_kernel(x_ref, o_ref, g0_ref, b0_ref, g_ref, b_ref, out_ref):
    h = _layer_norm_rows(x_ref[...], g0_ref[...], b0_ref[...])
    out_ref[...] = _layer_norm_rows(DEEPNORM_ALPHA * h + o_ref[...], g_ref[...], b_ref[...])


def _final_layer_norm(x, out, g0, b0, g, b, tm=256):
    s, d = x.shape
    row = pl.BlockSpec((tm, d), lambda i: (i, 0))
    vec = pl.BlockSpec((1, d), lambda i: (0, 0))
    return pl.pallas_call(
        _final_ln_kernel,
        out_shape=jax.ShapeDtypeStruct((s, d), F32),
        grid=(s // tm,),
        in_specs=[row, row, vec, vec, vec, vec],
        out_specs=row,
        compiler_params=_params(("parallel",)),
        name="final_ln",
    )(x, out, g0.reshape(1, d), b0.reshape(1, d), g.reshape(1, d), b.reshape(1, d))


def _cast_kernel(w_ref, o_ref):
    o_ref[...] = w_ref[...].astype(o_ref.dtype)


def _cast_rows_bf16(w_t, row0, n_rows, name):
    k = w_t.shape[1]
    tr = math.gcd(MLA_A_COLS, REST_COLS)
    assert row0 % tr == 0 and n_rows % tr == 0 and tr % 16 == 0
    return pl.pallas_call(
        _cast_kernel,
        out_shape=jax.ShapeDtypeStruct((n_rows, k), BF16),
        grid=(n_rows // tr,),
        in_specs=[pl.BlockSpec((tr, k), lambda j: (row0 // tr + j, 0))],
        out_specs=pl.BlockSpec((tr, k), lambda j: (j, 0)),
        compiler_params=_params(("parallel",)),
        name=name,
    )(w_t)


def _rope_tables(seq):
    half = QK_ROPE_DIM // 2
    pos = jnp.arange(seq, dtype=F32)
    inv_freq = 1.0 / (ROPE_THETA ** (jnp.arange(0, QK_ROPE_DIM, 2, dtype=F32) / QK_ROPE_DIM))
    lane_freq = jnp.concatenate([inv_freq, inv_freq, jnp.zeros((LANES - QK_ROPE_DIM,), F32)])
    ang = pos[:, None] * lane_freq[None, :]
    lane = np.arange(LANES)
    on_t1 = jnp.asarray((lane < half).astype(np.float32))
    on_t2 = jnp.asarray(((lane >= half) & (lane < QK_ROPE_DIM)).astype(np.float32))
    cos, sin = jnp.cos(ang), jnp.sin(ang)
    return jnp.concatenate([cos * (on_t1 + on_t2), sin * on_t2, -sin * on_t1], axis=1)


def kernel(x, emb_ln_g, emb_ln_b, rel_bias, w_in, q_a_norm_g, w_q_b, kv_a_norm_g, w_kv_b,
           w_o_mla, w_o_dswa, w_out, ln_g, ln_b):
    assert DEPTH == 1 and x.shape == (1, SEQ, D_MODEL)
    x2 = x[0]

    w_in_t = jnp.swapaxes(w_in, 1, 2)[0]
    w_mla_a = _cast_rows_bf16(w_in_t, 0, MLA_A_COLS, "mla_a_weights_bf16")

    q_scale = QK_HEAD_DIM ** -0.5 * math.log2(math.e)
    rope_tab = _rope_tables(SEQ)

    *h_slabs, w_q, w_kv = _input_layer_norm(x2, emb_ln_g, emb_ln_b,
                                            jnp.swapaxes(w_q_b, 1, 2)[0], w_kv_b[0])
    h = h_slabs[0]

    qn, ckvn, kpe = _mla_a_proj(h, w_mla_a, q_a_norm_g[0], kv_a_norm_g[0], rope_tab)
    q = _mla_q_proj(qn, w_q, rope_tab, q_scale)
    kv = _proj(ckvn, w_kv, n_out=2 * MLA_WIDTH, w_block=lambda j: j, tm=2048, tn=2 * COL_BLOCK,
               name="mla_kv_proj")
    a, w_rest, w_om, w_od, w_o = _mla_attention(
        q, kv, kpe, [(w_in_t, MLA_A_COLS, REST_COLS), (w_o_mla[0], 0, MLA_WIDTH),
                     (w_o_dswa[0], 0, DSWA_WIDTH), (w_out[0], 0, D_MODEL)])

    gates = _proj(h, w_rest, n_out=GATE_WIDTH, w_block=lambda j: GATE_COL0 + j, gated=True,
                  n_silu_blocks=(MLA_WIDTH + DSWA_WIDTH) // COL_BLOCK, w_transposed=True,
                  name="gate_proj")

    bias = _band_bias(rel_bias, BAND_TILE)
    outs, lses = [], []
    blocks_per_part = DSWA_QKV_WIDTH // COL_BLOCK
    for g, dil in enumerate(DILATIONS):
        qkv = _proj(h_slabs[g], w_rest, n_out=3 * DSWA_WIDTH, dil=dil, w_transposed=True,
                    w_block=lambda j, g=g: j * blocks_per_part + g, name=f"dswa_proj_d{dil}")
        o, lse = _band_attention(qkv, bias, g, dil)
        outs.append(o)
        lses.append(lse)
    b_gated = _group_mixture(outs, lses, gates)

    merged = _merge_branches(a, b_gated, gates, w_om, w_od)
    out = _proj(merged, w_o, n_out=D_MODEL, w_block=lambda j: j, out_dtype=F32, name="out_proj")
    y = _final_layer_norm(x2, out, emb_ln_g, emb_ln_b, ln_g[0], ln_b[0])
    return y[None]
```

```python
import functools
import math

import jax
import jax.numpy as jnp
import numpy as np
from jax import lax
from jax.experimental import pallas as pl
from jax.experimental.pallas import tpu as pltpu

D_MODEL = 4096
SEQ = 8192
DEPTH = 1

MLA_HEADS = 16
Q_LORA_RANK = 1024
KV_LORA_RANK = 512
QK_NOPE_DIM = 128
QK_ROPE_DIM = 64
QK_HEAD_DIM = QK_NOPE_DIM + QK_ROPE_DIM
V_HEAD_DIM = 128
MLA_WIDTH = MLA_HEADS * V_HEAD_DIM
ROPE_THETA = 10000.0

DIL_GROUPS = ((128, 1), (512, 4), (2048, 16))
DILATIONS = tuple(d for _, d in DIL_GROUPS)
DSWA_HEADS_PER_GROUP = 8
DSWA_HEADS = 24
DSWA_HEAD_DIM = 128
DSWA_QKV_WIDTH = DSWA_HEADS * DSWA_HEAD_DIM
DSWA_WIDTH = DSWA_HEADS_PER_GROUP * DSWA_HEAD_DIM
BAND_HALF = 64
BAND_TILE = 256

REL_BUCKETS = 32
REL_MAX_DISTANCE = 1024

DEEPNORM_ALPHA = (2.0 * DEPTH) ** 0.25
LN_EPS = 1e-5
RMS_EPS = 1e-6
NEG_INF = -1e30

IN_SPLITS = (Q_LORA_RANK, KV_LORA_RANK + QK_ROPE_DIM, DSWA_QKV_WIDTH, DSWA_QKV_WIDTH,
             DSWA_QKV_WIDTH, MLA_WIDTH, DSWA_WIDTH, D_MODEL, D_MODEL)
IN_OFFSETS = tuple(int(v) for v in np.cumsum(IN_SPLITS)[:-1])
MLA_A_COLS = IN_OFFSETS[1]
REST_COLS = sum(IN_SPLITS) - MLA_A_COLS

LANES = 128
QK_PAD_DIM = 256
COL_BLOCK = 1024
GATE_COL0 = 3 * DSWA_QKV_WIDTH // COL_BLOCK
GATE_WIDTH = MLA_WIDTH + DSWA_WIDTH + 2 * D_MODEL
VMEM_LIMIT = 56 * 1024 * 1024

LOG2_E = math.log2(math.e)
LN_2 = math.log(2.0)

BF16 = jnp.bfloat16
F32 = jnp.float32


def _params(sem):
    return pltpu.CompilerParams(dimension_semantics=sem, vmem_limit_bytes=VMEM_LIMIT)


def _dot(a, b):
    return jnp.dot(a, b, preferred_element_type=F32)


def _dot_nt(a, b):
    return lax.dot_general(a, b, (((1,), (1,)), ((), ())), preferred_element_type=F32)


def _sigmoid(x):
    return 0.5 * jnp.tanh(0.5 * x) + 0.5


def _layer_norm_rows(x, g, b):
    mu = jnp.mean(x, axis=-1, keepdims=True)
    xc = x - mu
    var = jnp.mean(xc * xc, axis=-1, keepdims=True)
    return xc * lax.rsqrt(var + LN_EPS) * g + b


def _rms_norm_rows(x, g):
    return x * lax.rsqrt(jnp.mean(x * x, axis=-1, keepdims=True) + RMS_EPS) * g


def _rope_lanes(t, tab):
    c, s1, s2 = tab[:, :LANES], tab[:, LANES:2 * LANES], tab[:, 2 * LANES:]
    return t * c + pltpu.roll(t, 32, 1) * s1 + pltpu.roll(t, LANES - 32, 1) * s2


def _class_gather_matrix(tm, dil):
    per_class = tm // dil
    i = np.arange(tm)
    p = np.zeros((tm, tm), np.float32)
    p[i, (i % per_class) * dil + i // per_class] = 1.0
    return jnp.asarray(p, BF16)


def _ln_kernel(x_ref, g_ref, b_ref, p4_ref, p16_ref, wq_ref, wkv_ref,
               h1_ref, h4_ref, h16_ref, wq_bf_ref, wkv_bf_ref):
    wq_bf_ref[...] = wq_ref[...].astype(BF16)
    wkv_bf_ref[...] = wkv_ref[...].astype(BF16)
    d = x_ref.shape[1]
    y = _layer_norm_rows(x_ref[...], g_ref[...], b_ref[...]).astype(BF16)
    h1_ref[...] = y
    for dil, p_ref, slab_ref in ((DILATIONS[1], p4_ref, h4_ref), (DILATIONS[2], p16_ref, h16_ref)):
        rows = slab_ref.shape[0]
        grouped = _dot(p_ref[...], y).astype(BF16)
        for r in range(dil):
            slab_ref[:, r * d:(r + 1) * d] = grouped[r * rows:(r + 1) * rows, :]


def _input_layer_norm(x, g, b, w_q_t, w_kv, tm=256):
    s, d = x.shape
    d4, d16 = DILATIONS[1], DILATIONS[2]
    steps = s // tm
    rq, rkv = w_q_t.shape[0] // steps, w_kv.shape[0] // steps
    assert rq * steps == w_q_t.shape[0] and rkv * steps == w_kv.shape[0]
    assert rq % 16 == 0 and rkv % 16 == 0
    return pl.pallas_call(
        _ln_kernel,
        out_shape=(jax.ShapeDtypeStruct((s, d), BF16),
                   jax.ShapeDtypeStruct((s // d4, d4 * d), BF16),
                   jax.ShapeDtypeStruct((s // d16, d16 * d), BF16),
                   jax.ShapeDtypeStruct(w_q_t.shape, BF16),
                   jax.ShapeDtypeStruct(w_kv.shape, BF16)),
        grid=(steps,),
        in_specs=[pl.BlockSpec((tm, d), lambda i: (i, 0)),
                  pl.BlockSpec((1, d), lambda i: (0, 0)),
                  pl.BlockSpec((1, d), lambda i: (0, 0)),
                  pl.BlockSpec((tm, tm), lambda i: (0, 0)),
                  pl.BlockSpec((tm, tm), lambda i: (0, 0)),
                  pl.BlockSpec((rq, w_q_t.shape[1]), lambda i: (i, 0)),
                  pl.BlockSpec((rkv, w_kv.shape[1]), lambda i: (i, 0))],
        out_specs=(pl.BlockSpec((tm, d), lambda i: (i, 0)),
                   pl.BlockSpec((tm // d4, d4 * d), lambda i: (i, 0)),
                   pl.BlockSpec((tm // d16, d16 * d), lambda i: (i, 0)),
                   pl.BlockSpec((rq, w_q_t.shape[1]), lambda i: (i, 0)),
                   pl.BlockSpec((rkv, w_kv.shape[1]), lambda i: (i, 0))),
        compiler_params=_params(("parallel",)),
        name="input_ln",
    )(x, g.reshape(1, d), b.reshape(1, d),
      _class_gather_matrix(tm, d4), _class_gather_matrix(tm, d16), w_q_t, w_kv)


def _proj_kernel(a_ref, w_ref, o_ref, *, n_silu_blocks, gated, w_transposed):
    acc = (_dot_nt if w_transposed else _dot)(a_ref[...], w_ref[...])
    if gated:
        sig = _sigmoid(acc)
        acc = jnp.where(pl.program_id(0) < n_silu_blocks, acc * sig, sig)
    o_ref[...] = acc.astype(o_ref.dtype)


def _proj(a, w, *, n_out, w_block, dil=1, tm=1024, tn=COL_BLOCK, gated=False, n_silu_blocks=0,
          w_transposed=False, out_dtype=BF16, name):
    rows, kd = a.shape
    k = kd // dil
    nj = n_out // tn
    tm = min(tm, rows)
    kern = functools.partial(_proj_kernel, n_silu_blocks=n_silu_blocks, gated=gated,
                             w_transposed=w_transposed)
    if w_transposed:
        w_spec = pl.BlockSpec((tn, k), lambda j, r, i: (w_block(j), 0))
    else:
        w_spec = pl.BlockSpec((k, tn), lambda j, r, i: (0, w_block(j)))
    return pl.pallas_call(
        kern,
        out_shape=jax.ShapeDtypeStruct((rows, dil * n_out), out_dtype),
        grid=(nj, dil, rows // tm),
        in_specs=[pl.BlockSpec((tm, k), lambda j, r, i: (i, r)), w_spec],
        out_specs=pl.BlockSpec((tm, tn), lambda j, r, i: (i, r * nj + j)),
        compiler_params=_params(("parallel", "parallel", "parallel")),
        name=name,
    )(a, w)


def _mla_a_kernel(h_ref, w_ref, gq_ref, gkv_ref, tab_ref, qn_ref, ckvn_ref, kpe_ref):
    acc = _dot_nt(h_ref[...], w_ref[...])
    qn_ref[...] = _rms_norm_rows(acc[:, :Q_LORA_RANK], gq_ref[...]).astype(BF16)
    ckv = acc[:, Q_LORA_RANK:Q_LORA_RANK + KV_LORA_RANK]
    ckvn_ref[...] = _rms_norm_rows(ckv, gkv_ref[...]).astype(BF16)
    k_pe = acc[:, Q_LORA_RANK + KV_LORA_RANK:]
    k_pe = jnp.concatenate([k_pe, jnp.zeros((k_pe.shape[0], LANES - QK_ROPE_DIM), F32)], axis=1)
    kpe_ref[...] = _rope_lanes(k_pe, tab_ref[...]).astype(BF16)


def _mla_a_proj(h, w, gq, gkv, k_tab, tm=1024):
    s, k = h.shape
    return pl.pallas_call(
        _mla_a_kernel,
        out_shape=(jax.ShapeDtypeStruct((s, Q_LORA_RANK), BF16),
                   jax.ShapeDtypeStruct((s, KV_LORA_RANK), BF16),
                   jax.ShapeDtypeStruct((s, LANES), BF16)),
        grid=(s // tm,),
        in_specs=[pl.BlockSpec((tm, k), lambda i: (i, 0)),
                  pl.BlockSpec((MLA_A_COLS, k), lambda i: (0, 0), pipeline_mode=pl.Buffered(1)),
                  pl.BlockSpec((1, Q_LORA_RANK), lambda i: (0, 0)),
                  pl.BlockSpec((1, KV_LORA_RANK), lambda i: (0, 0)),
                  pl.BlockSpec((tm, 3 * LANES), lambda i: (i, 0))],
        out_specs=(pl.BlockSpec((tm, Q_LORA_RANK), lambda i: (i, 0)),
                   pl.BlockSpec((tm, KV_LORA_RANK), lambda i: (i, 0)),
                   pl.BlockSpec((tm, LANES), lambda i: (i, 0))),
        compiler_params=_params(("parallel",)),
        name="mla_a_proj",
    )(h, w, gq.reshape(1, -1), gkv.reshape(1, -1), k_tab)


def _mla_q_kernel(qn_ref, w_ref, tab_ref, q_ref, *, q_scale):
    qn = qn_ref[...]
    tab = tab_ref[...]
    pad = jnp.zeros((qn.shape[0], LANES - QK_ROPE_DIM), F32)
    for hd in range(MLA_HEADS):
        lo = hd * QK_PAD_DIM
        acc = _dot_nt(qn, w_ref[hd * QK_HEAD_DIM:(hd + 1) * QK_HEAD_DIM, :]) * q_scale
        q_ref[:, lo:lo + LANES] = acc[:, :QK_NOPE_DIM].astype(BF16)
        q_pe = jnp.concatenate([acc[:, QK_NOPE_DIM:], pad], axis=1)
        q_ref[:, lo + LANES:lo + QK_PAD_DIM] = _rope_lanes(q_pe, tab).astype(BF16)


def _mla_q_proj(qn, w_t, rope_tab, q_scale, tm=1024):
    s, k = qn.shape
    n = MLA_HEADS * QK_PAD_DIM
    return pl.pallas_call(
        functools.partial(_mla_q_kernel, q_scale=q_scale),
        out_shape=jax.ShapeDtypeStruct((s, n), BF16),
        grid=(s // tm,),
        in_specs=[pl.BlockSpec((tm, k), lambda i: (i, 0)),
                  pl.BlockSpec(w_t.shape, lambda i: (0, 0)),
                  pl.BlockSpec((tm, 3 * LANES), lambda i: (i, 0))],
        out_specs=pl.BlockSpec((tm, n), lambda i: (i, 0)),
        compiler_params=_params(("parallel",)),
        name="mla_q_proj",
    )(qn, w_t, rope_tab)


def _mla_attn_kernel(q_ref, k_ref, v_ref, kpe_ref, *rest, tk, sub, n_cast):
    o_ref = rest[n_cast]
    kcat_ref, vcat_ref = rest[2 * n_cast + 1:]
    for src_ref, dst_ref in zip(rest[:n_cast], rest[n_cast + 1:2 * n_cast + 1]):
        dst_ref[...] = src_ref[...].astype(dst_ref.dtype)

    @pl.when(pl.program_id(1) == 0)
    def _():
        kcat_ref[:, :LANES] = k_ref[...]
        kcat_ref[:, LANES:] = kpe_ref[...]
        vcat_ref[:, :V_HEAD_DIM] = v_ref[...]
        vcat_ref[:, V_HEAD_DIM:] = jnp.ones((v_ref.shape[0], LANES), BF16)

    n_kv = k_ref.shape[0] // tk
    n_sub = q_ref.shape[0] // sub
    qs = [q_ref[c * sub:(c + 1) * sub, :] for c in range(n_sub)]
    ms = [jnp.full((sub, 1), -jnp.inf, F32) for _ in range(n_sub)]
    accs = [jnp.zeros((sub, V_HEAD_DIM + LANES), F32) for _ in range(n_sub)]
    for j in range(n_kv):
        kblk = kcat_ref[j * tk:(j + 1) * tk, :]
        vblk = vcat_ref[j * tk:(j + 1) * tk, :]
        for c in range(n_sub):
            s = _dot_nt(qs[c], kblk)
            m_new = jnp.maximum(ms[c], jnp.max(s, axis=1, keepdims=True))
            alpha = jnp.exp2(ms[c] - m_new)
            p = jnp.exp2(s - m_new).astype(BF16)
            accs[c] = alpha * accs[c] + _dot(p, vblk)
            ms[c] = m_new
    for c in range(n_sub):
        rows = slice(c * sub, (c + 1) * sub)
        attn = accs[c][:, :V_HEAD_DIM] / accs[c][:, V_HEAD_DIM:]
        o_ref[rows, :] = attn.astype(o_ref.dtype)


def _mla_attention(q, kv, kpe, casts, tq=2048, tk=1024, sub=512):
    s = q.shape[0]
    n_q = s // tq
    steps = MLA_HEADS * n_q
    cast_in, cast_out, cast_shapes = [], [], []
    for w, row0, rows in casts:
        tr = rows // steps
        assert tr * steps == rows and row0 % tr == 0 and tr % 16 == 0
        cast_in.append(pl.BlockSpec(
            (tr, w.shape[1]), lambda h, i, row0=row0, tr=tr: (row0 // tr + h * n_q + i, 0)))
        cast_out.append(pl.BlockSpec((tr, w.shape[1]), lambda h, i: (h * n_q + i, 0)))
        cast_shapes.append(jax.ShapeDtypeStruct((rows, w.shape[1]), BF16))
    return pl.pallas_call(
        functools.partial(_mla_attn_kernel, tk=tk, sub=sub, n_cast=len(casts)),
        out_shape=(jax.ShapeDtypeStruct((s, MLA_WIDTH), BF16), *cast_shapes),
        grid=(MLA_HEADS, n_q),
        in_specs=[pl.BlockSpec((tq, QK_PAD_DIM), lambda h, i: (i, h)),
                  pl.BlockSpec((s, QK_NOPE_DIM), lambda h, i: (0, 2 * h)),
                  pl.BlockSpec((s, V_HEAD_DIM), lambda h, i: (0, 2 * h + 1)),
                  pl.BlockSpec((s, LANES), lambda h, i: (0, 0)),
                  *cast_in],
        out_specs=(pl.BlockSpec((tq, V_HEAD_DIM), lambda h, i: (i, h)), *cast_out),
        scratch_shapes=[pltpu.VMEM((s, QK_PAD_DIM), BF16),
                        pltpu.VMEM((s, V_HEAD_DIM + LANES), BF16)],
        compiler_params=_params(("parallel", "arbitrary")),
        name="mla_attention",
    )(q, kv, kv, kpe, *[w for w, _, _ in casts])


def _band_bucket_index(dil, tq):
    nb = REL_BUCKETS // 2
    max_exact = nb // 2
    rel = (np.arange(tq + 2 * BAND_HALF)[None, :] - BAND_HALF - np.arange(tq)[:, None]) * dil
    n = np.abs(rel)
    nf = np.maximum(n, 1).astype(np.float64)
    large = max_exact + (np.log(nf / max_exact) / math.log(REL_MAX_DISTANCE / max_exact)
                         * (nb - max_exact)).astype(np.int32)
    large = np.minimum(large, nb - 1)
    return (np.where(rel > 0, nb, 0) + np.where(n < max_exact, n, large)).astype(np.int32)


def _band_bias_kernel(rb_ref, idx_ref, o_ref):
    hd = pl.program_id(0)
    idx = idx_ref[0]
    acc = jnp.zeros(idx.shape, F32)
    for b in range(REL_BUCKETS):
        acc = jnp.where(idx == b, rb_ref[b, hd], acc)
    o_ref[0] = acc * LOG2_E


def _band_bias(rel_bias, tq):
    idx = jnp.asarray(np.stack([_band_bucket_index(dil, tq) for dil in DILATIONS]))
    tw = idx.shape[2]
    return pl.pallas_call(
        _band_bias_kernel,
        out_shape=jax.ShapeDtypeStruct((DSWA_HEADS, tq, tw), F32),
        grid=(DSWA_HEADS,),
        in_specs=[pl.BlockSpec(memory_space=pltpu.SMEM),
                  pl.BlockSpec((1, tq, tw), lambda h: (h // DSWA_HEADS_PER_GROUP, 0, 0))],
        out_specs=pl.BlockSpec((1, tq, tw), lambda h: (h, 0, 0)),
        compiler_params=_params(("parallel",)),
        name="band_bias",
    )(rel_bias, idx)


def _band_attn_kernel(q_ref, kp_ref, ko_ref, kn_ref, vp_ref, vo_ref, vn_ref, bias_ref,
                      o_ref, lse_ref):
    tq = q_ref.shape[0]
    tw = tq + 2 * BAND_HALF
    tile = pl.program_id(1)
    prev_ok = tile != 0
    next_ok = tile != pl.num_programs(1) - 1
    row = lax.broadcasted_iota(jnp.int32, (tq, tw), 0)
    col = lax.broadcasted_iota(jnp.int32, (tq, tw), 1)
    rel = col - BAND_HALF - row
    mask = ((jnp.abs(rel) <= BAND_HALF) & ((col >= BAND_HALF) | prev_ok)
            & ((col < tq + BAND_HALF) | next_ok))
    lane = lax.broadcasted_iota(jnp.int32, (tq, LANES), 1)
    scale = DSWA_HEAD_DIM ** -0.5 * LOG2_E
    heads = range(DSWA_HEADS_PER_GROUP)
    cols = [slice(hd * DSWA_HEAD_DIM, (hd + 1) * DSWA_HEAD_DIM) for hd in heads]
    kc = [jnp.concatenate([kp_ref[:, c], ko_ref[:, c], kn_ref[:, c]], axis=0) for c in cols]
    s = [_dot_nt(q_ref[:, cols[hd]], kc[hd]) * scale + bias_ref[hd] for hd in heads]
    s = [jnp.where(mask, sh, NEG_INF) for sh in s]
    m = [jnp.max(sh, axis=1, keepdims=True) for sh in s]
    p = [jnp.exp2(sh - mh).astype(BF16) for sh, mh in zip(s, m)]
    ones = jnp.ones((tw, LANES), BF16)
    vc = [jnp.concatenate(
        [jnp.concatenate([vp_ref[:, c], vo_ref[:, c], vn_ref[:, c]], axis=0), ones], axis=1)
        for c in cols]
    pv = [_dot(p[hd], vc[hd]) for hd in heads]
    lse_tile = jnp.zeros((tq, LANES), F32)
    for hd in heads:
        den = pv[hd][:, DSWA_HEAD_DIM:]
        o_ref[:, cols[hd]] = (pv[hd][:, :DSWA_HEAD_DIM] / den).astype(o_ref.dtype)
        lse_tile = jnp.where(lane == hd, (m[hd] + jnp.log2(den)) * LN_2, lse_tile)
    lse_ref[...] = lse_tile


def _band_attention(qkv, bias, group, dil):
    rows = qkv.shape[0]
    tq = bias.shape[1]
    nt = rows // tq
    per_tile = tq // BAND_HALF
    n_edge = rows // BAND_HALF
    tw = tq + 2 * BAND_HALF

    def own(part):
        return pl.BlockSpec((tq, DSWA_WIDTH), lambda r, n: (n, 3 * r + part))

    def prev(part):
        return pl.BlockSpec((BAND_HALF, DSWA_WIDTH),
                            lambda r, n: (jnp.maximum(n * per_tile - 1, 0), 3 * r + part))

    def nxt(part):
        return pl.BlockSpec((BAND_HALF, DSWA_WIDTH),
                            lambda r, n: (jnp.minimum((n + 1) * per_tile, n_edge - 1), 3 * r + part))

    return pl.pallas_call(
        _band_attn_kernel,
        out_shape=(jax.ShapeDtypeStruct((rows, dil * DSWA_WIDTH), BF16),
                   jax.ShapeDtypeStruct((rows, dil * LANES), F32)),
        grid=(dil, nt),
        in_specs=[own(0),
                  prev(1), own(1), nxt(1),
                  prev(2), own(2), nxt(2),
                  pl.BlockSpec((DSWA_HEADS_PER_GROUP, tq, tw), lambda r, n: (group, 0, 0))],
        out_specs=(pl.BlockSpec((tq, DSWA_WIDTH), lambda r, n: (n, r)),
                   pl.BlockSpec((tq, LANES), lambda r, n: (n, r))),
        compiler_params=_params(("parallel", "parallel")),
        name=f"band_attention_d{dil}",
    )(qkv, qkv, qkv, qkv, qkv, qkv, qkv, bias)


def _mix_kernel(o1_ref, o4_ref, o16_ref, l1_ref, l4_ref, l16_ref, sg_ref, out_ref,
                o4n_ref, l4n_ref, o16n_ref, l16n_ref):
    for dil, o_ref, l_ref, on_ref, ln_ref in ((DILATIONS[1], o4_ref, l4_ref, o4n_ref, l4n_ref),
                                               (DILATIONS[2], o16_ref, l16_ref, o16n_ref, l16n_ref)):
        rows = o_ref.shape[0]
        for r in range(dil):
            for hd in range(DSWA_HEADS_PER_GROUP):
                lo = r * DSWA_WIDTH + hd * DSWA_HEAD_DIM
                on_ref[hd, pl.ds(r, rows, stride=dil), :] = (
                    o_ref[:, lo:lo + DSWA_HEAD_DIM].astype(F32))
            ln_ref[pl.ds(r, rows, stride=dil), :] = l_ref[:, r * LANES:(r + 1) * LANES]
    for hd in range(DSWA_HEADS_PER_GROUP):
        cols = slice(hd * DSWA_HEAD_DIM, (hd + 1) * DSWA_HEAD_DIM)
        l1 = l1_ref[:, hd:hd + 1]
        l2 = l4n_ref[:, hd:hd + 1]
        l3 = l16n_ref[:, hd:hd + 1]
        mx = jnp.maximum(jnp.maximum(l1, l2), l3)
        e1, e2, e3 = jnp.exp(l1 - mx), jnp.exp(l2 - mx), jnp.exp(l3 - mx)
        den = e1 + e2 + e3
        mix = ((e1 / den) * o1_ref[:, cols].astype(F32)
               + (e2 / den) * o4n_ref[hd]
               + (e3 / den) * o16n_ref[hd])
        out_ref[:, cols] = (mix * sg_ref[:, cols].astype(F32)).astype(out_ref.dtype)


def _group_mixture(outs, lses, gates, tm=256):
    o1, o4, o16 = outs
    l1, l4, l16 = lses
    s = o1.shape[0]
    d4, d16 = DILATIONS[1], DILATIONS[2]
    w = DSWA_WIDTH
    return pl.pallas_call(
        _mix_kernel,
        out_shape=jax.ShapeDtypeStruct((s, w), BF16),
        grid=(s // tm,),
        in_specs=[pl.BlockSpec((tm, w), lambda i: (i, 0)),
                  pl.BlockSpec((tm // d4, d4 * w), lambda i: (i, 0)),
                  pl.BlockSpec((tm // d16, d16 * w), lambda i: (i, 0)),
                  pl.BlockSpec((tm, LANES), lambda i: (i, 0)),
                  pl.BlockSpec((tm // d4, d4 * LANES), lambda i: (i, 0)),
                  pl.BlockSpec((tm // d16, d16 * LANES), lambda i: (i, 0)),
                  pl.BlockSpec((tm, w), lambda i: (i, MLA_WIDTH // w))],
        out_specs=pl.BlockSpec((tm, w), lambda i: (i, 0)),
        scratch_shapes=[pltpu.VMEM((DSWA_HEADS_PER_GROUP, tm, DSWA_HEAD_DIM), F32),
                        pltpu.VMEM((tm, LANES), F32),
                        pltpu.VMEM((DSWA_HEADS_PER_GROUP, tm, DSWA_HEAD_DIM), F32),
                        pltpu.VMEM((tm, LANES), F32)],
        compiler_params=_params(("parallel",)),
        name="group_mixture",
    )(o1, o4, o16, l1, l4, l16, gates)


def _merge_kernel(a_ref, sg_ref, b_ref, wm_ref, wd_ref, rm_ref, rd_ref, o_ref):
    a_gated = (a_ref[...].astype(F32) * sg_ref[...].astype(F32)).astype(BF16)
    y_mla = _dot(a_gated, wm_ref[...])
    y_dswa = _dot(b_ref[...], wd_ref[...])
    merged = rm_ref[...].astype(F32) * y_mla + rd_ref[...].astype(F32) * y_dswa
    o_ref[...] = merged.astype(o_ref.dtype)


def _merge_branches(a, b_gated, gates, w_o_mla, w_o_dswa, tm=1024, tn=COL_BLOCK):
    s = a.shape[0]
    d = w_o_mla.shape[1]
    r_mla_block = (MLA_WIDTH + DSWA_WIDTH) // tn
    r_dswa_block = (MLA_WIDTH + DSWA_WIDTH + D_MODEL) // tn
    return pl.pallas_call(
        _merge_kernel,
        out_shape=jax.ShapeDtypeStruct((s, d), BF16),
        grid=(d // tn, s // tm),
        in_specs=[pl.BlockSpec((tm, MLA_WIDTH), lambda j, i: (i, 0)),
                  pl.BlockSpec((tm, MLA_WIDTH), lambda j, i: (i, 0)),
                  pl.BlockSpec((tm, DSWA_WIDTH), lambda j, i: (i, 0)),
                  pl.BlockSpec((MLA_WIDTH, tn), lambda j, i: (0, j)),
                  pl.BlockSpec((DSWA_WIDTH, tn), lambda j, i: (0, j)),
                  pl.BlockSpec((tm, tn), lambda j, i: (i, r_mla_block + j)),
                  pl.BlockSpec((tm, tn), lambda j, i: (i, r_dswa_block + j))],
        out_specs=pl.BlockSpec((tm, tn), lambda j, i: (i, j)),
        compiler_params=_params(("parallel", "parallel")),
        name="merge_branches",
    )(a, gates, b_gated, w_o_mla, w_o_dswa, gates, gates)


def _final_ln_kernel(x_ref, o_ref, g0_ref, b0_ref, g_ref, b_ref, out_ref):
    h = _layer_norm_rows(x_ref[...], g0_ref[...], b0_ref[...])
    out_ref[...] = _layer_norm_rows(DEEPNORM_ALPHA * h + o_ref[...], g_ref[...], b_ref[...])


def _final_layer_norm(x, out, g0, b0, g, b, tm=256):
    s, d = x.shape
    row = pl.BlockSpec((tm, d), lambda i: (i, 0))
    vec = pl.BlockSpec((1, d), lambda i: (0, 0))
    return pl.pallas_call(
        _final_ln_kernel,
        out_shape=jax.ShapeDtypeStruct((s, d), F32),
        grid=(s // tm,),
        in_specs=[row, row, vec, vec, vec, vec],
        out_specs=row,
        compiler_params=_params(("parallel",)),
        name="final_ln",
    )(x, out, g0.reshape(1, d), b0.reshape(1, d), g.reshape(1, d), b.reshape(1, d))


def _cast_kernel(w_ref, o_ref):
    o_ref[...] = w_ref[...].astype(o_ref.dtype)


def _cast_rows_bf16(w_t, row0, n_rows, name):
    k = w_t.shape[1]
    tr = math.gcd(MLA_A_COLS, REST_COLS)
    assert row0 % tr == 0 and n_rows % tr == 0 and tr % 16 == 0
    return pl.pallas_call(
        _cast_kernel,
        out_shape=jax.ShapeDtypeStruct((n_rows, k), BF16),
        grid=(n_rows // tr,),
        in_specs=[pl.BlockSpec((tr, k), lambda j: (row0 // tr + j, 0))],
        out_specs=pl.BlockSpec((tr, k), lambda j: (j, 0)),
        compiler_params=_params(("parallel",)),
        name=name,
    )(w_t)


def _rope_tables(seq):
    half = QK_ROPE_DIM // 2
    pos = jnp.arange(seq, dtype=F32)
    inv_freq = 1.0 / (ROPE_THETA ** (jnp.arange(0, QK_ROPE_DIM, 2, dtype=F32) / QK_ROPE_DIM))
    lane_freq = jnp.concatenate([inv_freq, inv_freq, jnp.zeros((LANES - QK_ROPE_DIM,), F32)])
    ang = pos[:, None] * lane_freq[None, :]
    lane = np.arange(LANES)
    on_t1 = jnp.asarray((lane < half).astype(np.float32))
    on_t2 = jnp.asarray(((lane >= half) & (lane < QK_ROPE_DIM)).astype(np.float32))
    cos, sin = jnp.cos(ang), jnp.sin(ang)
    return jnp.concatenate([cos * (on_t1 + on_t2), sin * on_t2, -sin * on_t1], axis=1)


def kernel(x, emb_ln_g, emb_ln_b, rel_bias, w_in, q_a_norm_g, w_q_b, kv_a_norm_g, w_kv_b,
           w_o_mla, w_o_dswa, w_out, ln_g, ln_b):
    assert DEPTH == 1 and x.shape == (1, SEQ, D_MODEL)
    x2 = x[0]

    w_in_t = jnp.swapaxes(w_in, 1, 2)[0]
    w_mla_a = _cast_rows_bf16(w_in_t, 0, MLA_A_COLS, "mla_a_weights_bf16")

    q_scale = QK_HEAD_DIM ** -0.5 * math.log2(math.e)
    rope_tab = _rope_tables(SEQ)

    *h_slabs, w_q, w_kv = _input_layer_norm(x2, emb_ln_g, emb_ln_b,
                                            jnp.swapaxes(w_q_b, 1, 2)[0], w_kv_b[0])
    h = h_slabs[0]

    qn, ckvn, kpe = _mla_a_proj(h, w_mla_a, q_a_norm_g[0], kv_a_norm_g[0], rope_tab)
    q = _mla_q_proj(qn, w_q, rope_tab, q_scale)
    kv = _proj(ckvn, w_kv, n_out=2 * MLA_WIDTH, w_block=lambda j: j, tm=2048, tn=2 * COL_BLOCK,
               name="mla_kv_proj")
    a, w_rest, w_om, w_od, w_o = _mla_attention(
        q, kv, kpe, [(w_in_t, MLA_A_COLS, REST_COLS), (w_o_mla[0], 0, MLA_WIDTH),
                     (w_o_dswa[0], 0, DSWA_WIDTH), (w_out[0], 0, D_MODEL)])

    gates = _proj(h, w_rest, n_out=GATE_WIDTH, w_block=lambda j: GATE_COL0 + j, gated=True,
                  n_silu_blocks=(MLA_WIDTH + DSWA_WIDTH) // COL_BLOCK, w_transposed=True,
                  name="gate_proj")

    bias = _band_bias(rel_bias, BAND_TILE)
    outs, lses = [], []
    blocks_per_part = DSWA_QKV_WIDTH // COL_BLOCK
    for g, dil in enumerate(DILATIONS):
        qkv = _proj(h_slabs[g], w_rest, n_out=3 * DSWA_WIDTH, dil=dil, w_transposed=True,
                    w_block=lambda j, g=g: j * blocks_per_part + g, name=f"dswa_proj_d{dil}")
        o, lse = _band_attention(qkv, bias, g, dil)
        outs.append(o)
        lses.append(lse)
    b_gated = _group_mixture(outs, lses, gates)

    merged = _merge_branches(a, b_gated, gates, w_om, w_od)
    out = _proj(merged, w_o, n_out=D_MODEL, w_block=lambda j: j, out_dtype=F32, name="out_proj")
    y = _final_layer_norm(x2, out, emb_ln_g, emb_ln_b, ln_g[0], ln_b[0])
    return y[None]
```
